```python
import math
import jax
import jax.numpy as jnp
from jax import lax
import numpy as np

D_MODEL = 2048
BATCH = 8
SEQ = 2048
DEPTH = 2
DEC_BATCH = 32
DEC_SEQ = 8
PAST_LEN = 8192
PAGE_SIZE = 128

D_FF = 5632
NORM_EPS = 1e-6
GLA_HEADS = 4
GLA_DK = 128
GLA_DV = 256
GLA_RANK = 16
GLA_TAU = 16.0
GLA_CHUNK = 64
GLA_KEY = GLA_HEADS * GLA_DK
GLA_VAL = GLA_HEADS * GLA_DV
NSA_HEADS = 16
NSA_KV_HEADS = 4
NSA_GROUP = NSA_HEADS // NSA_KV_HEADS
NSA_D = 64
NSA_OUT = NSA_HEADS * NSA_D
NSA_KVW = NSA_KV_HEADS * NSA_D
CMP_STRIDE = 16
CMP_LEN = 2 * CMP_STRIDE
CMP_HID = 64
SEL_BLOCK = 64
N_SEL = 16
WINDOW = 512
NSA_QBLK = 32
SB_HEADS = 16
SB_D = 64
SB_OUT = SB_HEADS * SB_D
SB_QBLK = 128
N_BUCKETS = 32
RPB_MAX_EXACT = 16
RPB_MAX_DIST = 128
NEG = -1e30
FORCE = 1e9
IN_SPLITS = (GLA_KEY, GLA_KEY, GLA_VAL, GLA_RANK, GLA_VAL, NSA_OUT, 6 * NSA_KVW, 3 * NSA_HEADS, SB_OUT, 2 * SB_OUT, 3 * D_MODEL)
N_IN = sum(IN_SPLITS)

kernel_name = 'hybrid_gla_nsa_stickbreak_decode_step'


def rmsnorm(x, g):
    xf = x.astype(jnp.float32)
    y = xf * lax.rsqrt(jnp.mean(xf * xf, axis=-1, keepdims=True) + NORM_EPS)
    return (y * g.astype(jnp.float32)).astype(x.dtype)


def swiglu(h, w_gate, w_up, w_down):
    return (jax.nn.silu(h @ w_gate) * (h @ w_up)) @ w_down


def split_projection(h, w_in):
    z = h @ w_in
    cuts = [int(c) for c in np.cumsum(IN_SPLITS)[:-1]]
    return jnp.split(z, cuts, axis=-1)


def rel_bucket(dist):
    n = jnp.maximum(dist, 0)
    nf = jnp.maximum(n, 1).astype(jnp.float32)
    large = RPB_MAX_EXACT + (jnp.log(nf / RPB_MAX_EXACT) / math.log(RPB_MAX_DIST / RPB_MAX_EXACT)
                             * (N_BUCKETS - RPB_MAX_EXACT)).astype(jnp.int32)
    return jnp.where(n < RPB_MAX_EXACT, n, jnp.minimum(large, N_BUCKETS - 1))


def gla_scan(q, k, v, log_a, s0):
    B, T = q.shape[:2]
    C = min(GLA_CHUNK, T)
    n_chunks = -(-T // C)
    pad = n_chunks * C - T

    def prep(a):
        a = jnp.pad(a, ((0, 0), (0, pad), (0, 0), (0, 0)))
        return a.reshape(B, n_chunks, C, a.shape[2], a.shape[3]).transpose(1, 0, 3, 2, 4)

    xs = (prep(q), prep(k), prep(v), prep(log_a.astype(jnp.float32)))
    tri = jnp.tril(jnp.ones((C, C), dtype=bool))

    def step(S, inp):
        qb, kb, vb, ab = inp
        qf, kf, vf = qb.astype(jnp.float32), kb.astype(jnp.float32), vb.astype(jnp.float32)
        b = jnp.cumsum(ab, axis=2)
        o_inter = jnp.einsum('bhcd,bhde->bhce', qf * jnp.exp(b), S)
        rel = jnp.where(tri[None, None, :, :, None], b[:, :, :, None, :] - b[:, :, None, :, :], -jnp.inf)
        att = jnp.einsum('bhtd,bhsd,bhtsd->bhts', qf, kf, jnp.exp(rel))
        o = o_inter + jnp.einsum('bhts,bhse->bhte', att, vf)
        b_last = b[:, :, -1:, :]
        S = jnp.exp(b_last[:, :, 0, :])[..., None] * S + jnp.einsum('bhsd,bhse->bhde', kf * jnp.exp(b_last - b), vf)
        return S, o

    S, o = lax.scan(step, s0.astype(jnp.float32), xs)
    o = o.transpose(1, 0, 3, 2, 4).reshape(B, n_chunks * C, GLA_HEADS, GLA_DV)[:, :T]
    return o.astype(q.dtype), S.astype(s0.dtype)


def gla_mixer(q, k, v, a_low, r, s0, w_decay, b_decay, out_norm):
    B, T = q.shape[:2]
    q = q.reshape(B, T, GLA_HEADS, GLA_DK) * (GLA_DK ** -0.5)
    k = k.reshape(B, T, GLA_HEADS, GLA_DK)
    v = v.reshape(B, T, GLA_HEADS, GLA_DV)
    log_a = (jax.nn.log_sigmoid((a_low @ w_decay + b_decay).astype(jnp.float32)) / GLA_TAU).reshape(B, T, GLA_HEADS, GLA_DK)
    o, state = gla_scan(q, k, v, log_a, s0)
    o = rmsnorm(o, out_norm).reshape(B, T, GLA_VAL) * jax.nn.silu(r)
    return o, state


def nsa_compress(rows, pos_emb, w1, w2):
    B, S = rows.shape[:2]
    n_sub = S // CMP_STRIDE
    sub = rows[:, :n_sub * CMP_STRIDE].reshape(B, n_sub, CMP_STRIDE, NSA_KV_HEADS, NSA_D)
    first = jnp.einsum('bnlgd,lde->bnge', sub + pos_emb[None, None, :CMP_STRIDE, None, :], w1[:CMP_STRIDE])
    second = jnp.einsum('bnlgd,lde->bnge', sub + pos_emb[None, None, CMP_STRIDE:, None, :], w1[CMP_STRIDE:])
    hid = jax.nn.silu(first[:, :-1] + second[:, 1:])
    return jnp.einsum('bnge,ed->bngd', hid, w2)


def sel_blocks(rows):
    B, S = rows.shape[:2]
    ns = -(-S // SEL_BLOCK)
    rows = jnp.pad(rows, ((0, 0), (0, ns * SEL_BLOCK - S), (0, 0), (0, 0)))
    return rows.reshape(B, ns, SEL_BLOCK, NSA_KV_HEADS, NSA_D).transpose(0, 3, 1, 2, 4)


def cmp_sel_map(nc, ns):
    c_start = jnp.arange(nc) * CMP_STRIDE
    s_start = jnp.arange(ns) * SEL_BLOCK
    hit = (c_start[:, None] < s_start[None, :] + SEL_BLOCK) & (c_start[:, None] + CMP_LEN > s_start[None, :])
    return hit.astype(jnp.float32)


def nsa_block(q, gates, q_pos, kc, vc, c_end, ksb, vsb, wk, wv, w_pos, rpb):
    B, Q, G, R, d = q.shape
    scale = d ** -0.5
    table = rpb.astype(jnp.float32).reshape(N_BUCKETS, G, R)
    s = jnp.einsum('bqgrd,bngd->bqgrn', q, kc).astype(jnp.float32) * scale
    s = s + table[rel_bucket(q_pos[:, None] - c_end[None, :])].transpose(0, 2, 3, 1)[None]
    m = (c_end[None, :] <= q_pos[:, None])[None, :, None, None, :]
    p_c = jax.nn.softmax(jnp.where(m, s, NEG), axis=-1) * m
    o_c = jnp.einsum('bqgrn,bngd->bqgrd', p_c.astype(vc.dtype), vc)
    ns = ksb.shape[2]
    imp = jnp.einsum('bqgn,nj->bqgj', p_c.sum(axis=3), cmp_sel_map(kc.shape[1], ns))
    j = jnp.arange(ns)[None, :]
    cur = (q_pos // SEL_BLOCK)[:, None]
    valid = (j * SEL_BLOCK <= q_pos[:, None])[None, :, None, :]
    forced = ((j == 0) | (j == cur) | (j == cur - 1))[None, :, None, :]
    score = jnp.where(valid, jnp.where(forced, FORCE, imp), NEG)
    _, idx = lax.top_k(score, min(N_SEL, ns))
    idx = idx.transpose(0, 2, 1, 3)
    bi = jnp.arange(B)[:, None, None, None]
    gi = jnp.arange(G)[None, :, None, None]
    kg = ksb[bi, gi, idx]
    vg = vsb[bi, gi, idx]
    k_pos = idx[..., None] * SEL_BLOCK + jnp.arange(SEL_BLOCK)
    s = jnp.einsum('bqgrd,bgqnld->bgqrnl', q, kg).astype(jnp.float32) * scale
    bias = table.transpose(1, 0, 2)[gi[..., None], rel_bucket(q_pos[None, None, :, None, None] - k_pos)]
    s = s + bias.transpose(0, 1, 2, 5, 3, 4)
    m = (k_pos <= q_pos[None, None, :, None, None])[:, :, :, None]
    shp = s.shape
    p = jax.nn.softmax(jnp.where(m, s, NEG).reshape(shp[0], shp[1], shp[2], shp[3], -1), axis=-1).reshape(shp)
    o_s = jnp.einsum('bgqrnl,bgqnld->bqgrd', p.astype(vg.dtype), vg)
    s = jnp.einsum('bqgrd,bsgd->bqgrs', q, wk).astype(jnp.float32) * scale
    dist = q_pos[:, None] - w_pos[None, :]
    s = s + table[rel_bucket(dist)].transpose(0, 2, 3, 1)[None]
    m = ((dist >= 0) & (dist < WINDOW) & (w_pos[None, :] >= 0))[None, :, None, None, :]
    p = jax.nn.softmax(jnp.where(m, s, NEG), axis=-1)
    o_w = jnp.einsum('bqgrs,bsgd->bqgrd', p.astype(wv.dtype), wv)
    g = jax.nn.sigmoid(gates.astype(jnp.float32))
    o = g[..., 0:1] * o_c + g[..., 1:2] * o_s + g[..., 2:3] * o_w
    return o.reshape(B, Q, G * R * d).astype(q.dtype)


def sb_block(q, q_pos, k, v, k_pos):
    z = jnp.einsum('bqhd,bshd->bhqs', q, k).astype(jnp.float32) * (SB_D ** -0.5)
    m = (k_pos[None, :] < q_pos[:, None])[None, None]
    log_fail = jnp.where(m, jax.nn.log_sigmoid(-z), 0.0)
    later = lax.cumsum(log_fail, axis=3, reverse=True) - log_fail
    a = jnp.where(m, jnp.exp(jax.nn.log_sigmoid(z) + later), 0.0)
    return jnp.einsum('bhqs,bshd->bqhd', a.astype(v.dtype), v)


def merge_branches(o_gla, o_nsa, o_sb, gate_cols, w_br_gla, w_br_nsa, w_br_sb, w_out):
    g_a, g_b, g_c = jnp.split(jax.nn.sigmoid(gate_cols), 3, axis=-1)
    y = g_a * (o_gla @ w_br_gla) + g_b * (o_nsa @ w_br_nsa) + g_c * (o_sb @ w_br_sb)
    return y @ w_out


def mix_prompt(h, w_in, gla_w_decay, gla_b_decay, gla_out_norm, cmp_pos, cmp_w1, cmp_w2, rpb,
               w_br_gla, w_br_nsa, w_br_sb, w_out):
    B, T = h.shape[:2]
    gq, gk, gv, ga, gr, nq, nkv, ng, sq, skv, mg = split_projection(h, w_in)
    s0 = jnp.zeros((B, GLA_HEADS, GLA_DK, GLA_DV), jnp.float32)
    o_gla, gla_state = gla_mixer(gq, gk, gv, ga, gr, s0, gla_w_decay, gla_b_decay, gla_out_norm)
    nkv = nkv.reshape(B, T, 6, NSA_KV_HEADS, NSA_D)
    kc = nsa_compress(nkv[:, :, 0], cmp_pos[0], cmp_w1[0], cmp_w2[0])
    vc = nsa_compress(nkv[:, :, 1], cmp_pos[1], cmp_w1[1], cmp_w2[1])
    c_end = jnp.arange(kc.shape[1]) * CMP_STRIDE + (CMP_LEN - 1)
    ksb, vsb = sel_blocks(nkv[:, :, 2]), sel_blocks(nkv[:, :, 3])
    wpad = ((0, 0), (WINDOW, 0), (0, 0), (0, 0))
    wk, wv = jnp.pad(nkv[:, :, 4], wpad), jnp.pad(nkv[:, :, 5], wpad)
    qb = min(NSA_QBLK, T)
    nb = T // qb
    qs = nq.reshape(B, nb, qb, NSA_KV_HEADS, NSA_GROUP, NSA_D).swapaxes(0, 1)
    gs = ng.reshape(B, nb, qb, NSA_KV_HEADS, NSA_GROUP, 3).swapaxes(0, 1)

    def nsa_q_block(inp):
        q_i, g_i, i = inp
        q0 = i * qb
        w_k = lax.dynamic_slice_in_dim(wk, q0, WINDOW + qb, axis=1)
        w_v = lax.dynamic_slice_in_dim(wv, q0, WINDOW + qb, axis=1)
        return nsa_block(q_i, g_i, q0 + jnp.arange(qb), kc, vc, c_end, ksb, vsb, w_k, w_v,
                         q0 - WINDOW + jnp.arange(WINDOW + qb), rpb)

    o_nsa = lax.map(nsa_q_block, (qs, gs, jnp.arange(nb))).swapaxes(0, 1).reshape(B, T, NSA_OUT)
    skv = skv.reshape(B, T, 2, SB_HEADS, SB_D)
    sbq = min(SB_QBLK, T)
    nsb = T // sbq
    sqs = sq.reshape(B, nsb, sbq, SB_HEADS, SB_D).swapaxes(0, 1)
    k_pos = jnp.arange(T)

    def sb_q_block(inp):
        q_i, i = inp
        return sb_block(q_i, i * sbq + jnp.arange(sbq), skv[:, :, 0], skv[:, :, 1], k_pos)

    o_sb = lax.map(sb_q_block, (sqs, jnp.arange(nsb))).swapaxes(0, 1).reshape(B, T, SB_OUT)
    out = merge_branches(o_gla, o_nsa, o_sb, mg, w_br_gla, w_br_nsa, w_br_sb, w_out)
    keep = min(WINDOW, T)
    return out, nkv[:, :, :4], skv, nkv[:, T - keep:, 4:], gla_state


def mix_sample(h, past_nsa, past_sb, win_buf, s0, past_len, w_in, gla_w_decay, gla_b_decay, gla_out_norm,
               cmp_pos, cmp_w1, cmp_w2, rpb, w_br_gla, w_br_nsa, w_br_sb, w_out):
    B, T = h.shape[:2]
    gq, gk, gv, ga, gr, nq, nkv, ng, sq, skv, mg = split_projection(h, w_in)
    o_gla, gla_state = gla_mixer(gq, gk, gv, ga, gr, s0, gla_w_decay, gla_b_decay, gla_out_norm)
    nkv = nkv.reshape(B, T, 6, NSA_KV_HEADS, NSA_D)
    rows = jnp.concatenate([past_nsa, nkv[:, :, :4]], axis=1)
    kc = nsa_compress(rows[:, :, 0], cmp_pos[0], cmp_w1[0], cmp_w2[0])
    vc = nsa_compress(rows[:, :, 1], cmp_pos[1], cmp_w1[1], cmp_w2[1])
    c_end = jnp.arange(kc.shape[1]) * CMP_STRIDE + (CMP_LEN - 1)
    ksb, vsb = sel_blocks(rows[:, :, 2]), sel_blocks(rows[:, :, 3])
    win = jnp.concatenate([win_buf, nkv[:, :, 4:]], axis=1)
    wb = win_buf.shape[1]
    q_pos = past_len + jnp.arange(T)
    w_pos = (past_len - wb) + jnp.arange(wb + T)
    o_nsa = nsa_block(nq.reshape(B, T, NSA_KV_HEADS, NSA_GROUP, NSA_D), ng.reshape(B, T, NSA_KV_HEADS, NSA_GROUP, 3),
                      q_pos, kc, vc, c_end, ksb, vsb, win[:, :, 0], win[:, :, 1], w_pos, rpb)
    skv = skv.reshape(B, T, 2, SB_HEADS, SB_D)
    sb_rows = jnp.concatenate([past_sb, skv], axis=1)
    o_sb = sb_block(sq.reshape(B, T, SB_HEADS, SB_D), q_pos, sb_rows[:, :, 0], sb_rows[:, :, 1],
                    jnp.arange(past_len + T)).reshape(B, T, SB_OUT)
    out = merge_branches(o_gla, o_nsa, o_sb, mg, w_br_gla, w_br_nsa, w_br_sb, w_out)
    keep = min(WINDOW, wb + T)
    return out, nkv[:, :, :4], skv, win[:, wb + T - keep:], gla_state


def setup_inputs(seed: int = 0) -> dict:
    key = jax.random.key(seed)
    keys = iter(jax.random.split(key, 40))

    def nrm(shape, scale):
        return jax.random.normal(next(keys), shape, jnp.float32) * scale

    def gain(shape):
        return 1.0 + nrm(shape, 0.01)

    n_pages = PAST_LEN // PAGE_SIZE
    n_used = DEC_BATCH * n_pages
    n_pool = n_used + max(1, n_used // 4)
    wb = min(WINDOW, PAST_LEN)
    x_prompt = nrm((BATCH, SEQ, D_MODEL), 1.0)
    x_sample = nrm((DEC_BATCH, DEC_SEQ, D_MODEL), 1.0)
    cache_nsa = nrm((n_pool, DEPTH, PAGE_SIZE, 4, NSA_KV_HEADS, NSA_D), 1.0)
    cache_sb = nrm((n_pool, DEPTH, PAGE_SIZE, 2, SB_HEADS, SB_D), 1.0)
    cache_win = nrm((DEC_BATCH, DEPTH, wb, 2, NSA_KV_HEADS, NSA_D), 1.0)
    state_gla = nrm((DEC_BATCH, DEPTH, GLA_HEADS, GLA_DK, GLA_DV), 0.5)
    page_table = jax.random.permutation(next(keys), n_pool)[:n_used].reshape(DEC_BATCH, n_pages).astype(jnp.int32)
    return {
        'x_prompt': x_prompt,
        'x_sample': x_sample,
        'cache_nsa': cache_nsa,
        'cache_sb': cache_sb,
        'cache_win': cache_win,
        'state_gla': state_gla,
        'page_table': page_table,
        'ffn1_norm': gain((DEPTH, D_MODEL)),
        'ffn1_w_gate': nrm((DEPTH, D_MODEL, D_FF), D_MODEL ** -0.5),
        'ffn1_w_up': nrm((DEPTH, D_MODEL, D_FF), D_MODEL ** -0.5),
        'ffn1_w_down': nrm((DEPTH, D_FF, D_MODEL), D_FF ** -0.5),
        'mix_norm': gain((DEPTH, D_MODEL)),
        'w_in': nrm((DEPTH, D_MODEL, N_IN), D_MODEL ** -0.5),
        'gla_w_decay': nrm((DEPTH, GLA_RANK, GLA_KEY), GLA_RANK ** -0.5),
        'gla_b_decay': nrm((DEPTH, GLA_KEY), 0.1),
        'gla_out_norm': gain((DEPTH, GLA_DV)),
        'nsa_cmp_pos': nrm((DEPTH, 2, CMP_LEN, NSA_D), 0.02),
        'nsa_cmp_w1': nrm((DEPTH, 2, CMP_LEN, NSA_D, CMP_HID), (CMP_LEN * NSA_D) ** -0.5),
        'nsa_cmp_w2': nrm((DEPTH, 2, CMP_HID, NSA_D), CMP_HID ** -0.5),
        'rpb_table': nrm((N_BUCKETS, NSA_HEADS), 0.5),
        'w_branch_gla': nrm((DEPTH, GLA_VAL, D_MODEL), GLA_VAL ** -0.5),
        'w_branch_nsa': nrm((DEPTH, NSA_OUT, D_MODEL), NSA_OUT ** -0.5),
        'w_branch_sb': nrm((DEPTH, SB_OUT, D_MODEL), SB_OUT ** -0.5),
        'w_out': nrm((DEPTH, D_MODEL, D_MODEL), D_MODEL ** -0.5),
        'ffn2_norm': gain((DEPTH, D_MODEL)),
        'ffn2_w_gate': nrm((DEPTH, D_MODEL, D_FF), D_MODEL ** -0.5),
        'ffn2_w_up': nrm((DEPTH, D_MODEL, D_FF), D_MODEL ** -0.5),
        'ffn2_w_down': nrm((DEPTH, D_FF, D_MODEL), D_FF ** -0.5),
        'final_norm': gain((D_MODEL,)),
    }


def reference(x_prompt, x_sample, cache_nsa, cache_sb, cache_win, state_gla, page_table,
              ffn1_norm, ffn1_w_gate, ffn1_w_up, ffn1_w_down, mix_norm, w_in, gla_w_decay, gla_b_decay,
              gla_out_norm, nsa_cmp_pos, nsa_cmp_w1, nsa_cmp_w2, rpb_table, w_branch_gla, w_branch_nsa,
              w_branch_sb, w_out, ffn2_norm, ffn2_w_gate, ffn2_w_up, ffn2_w_down, final_norm):
    dec_batch, n_pages = page_table.shape
    past_len = n_pages * cache_nsa.shape[2]
    xp, xs = x_prompt, x_sample
    nsa_p, nsa_s, sb_p, sb_s = [], [], [], []
    win_p, win_s, gla_p, gla_s = [], [], [], []
    for l in range(DEPTH):
        ffn1 = (ffn1_w_gate[l], ffn1_w_up[l], ffn1_w_down[l])
        xp = xp + 0.5 * swiglu(rmsnorm(xp, ffn1_norm[l]), *ffn1)
        xs = xs + 0.5 * swiglu(rmsnorm(xs, ffn1_norm[l]), *ffn1)
        mix_w = (w_in[l], gla_w_decay[l], gla_b_decay[l], gla_out_norm[l], nsa_cmp_pos[l], nsa_cmp_w1[l],
                 nsa_cmp_w2[l], rpb_table, w_branch_gla[l], w_branch_nsa[l], w_branch_sb[l], w_out[l])
        out_p, r_nsa, r_sb, r_win, r_gla = mix_prompt(rmsnorm(xp, mix_norm[l]), *mix_w)
        nsa_p.append(r_nsa)
        sb_p.append(r_sb)
        win_p.append(r_win)
        gla_p.append(r_gla)
        past_nsa = cache_nsa[page_table, l].reshape(dec_batch, past_len, 4, NSA_KV_HEADS, NSA_D)
        past_sb = cache_sb[page_table, l].reshape(dec_batch, past_len, 2, SB_HEADS, SB_D)
        out_s, r_nsa, r_sb, r_win, r_gla = mix_sample(rmsnorm(xs, mix_norm[l]), past_nsa, past_sb,
                                                      cache_win[:, l], state_gla[:, l], past_len, *mix_w)
        nsa_s.append(r_nsa)
        sb_s.append(r_sb)
        win_s.append(r_win)
        gla_s.append(r_gla)
        xp = xp + out_p
        xs = xs + out_s
        ffn2 = (ffn2_w_gate[l], ffn2_w_up[l], ffn2_w_down[l])
        xp = xp + 0.5 * swiglu(rmsnorm(xp, ffn2_norm[l]), *ffn2)
        xs = xs + 0.5 * swiglu(rmsnorm(xs, ffn2_norm[l]), *ffn2)
    y_prompt = rmsnorm(xp, final_norm)
    y_sample = rmsnorm(xs, final_norm)
    return (y_prompt, y_sample, jnp.stack(nsa_p, axis=1), jnp.stack(nsa_s, axis=1), jnp.stack(sb_p, axis=1),
            jnp.stack(sb_s, axis=1), jnp.stack(win_p, axis=1), jnp.stack(win_s, axis=1), jnp.stack(gla_p, axis=1),
            jnp.stack(gla_s, axis=1))
```

```python
import functools
import math

import jax
import jax.numpy as jnp
import numpy as np
from jax import lax
from jax.experimental import pallas as pl
from jax.experimental.pallas import tpu as pltpu

D_MODEL = 2048
DEPTH = 2
D_FF = 5632
NORM_EPS = 1e-6
GLA_HEADS = 4
GLA_DK = 128
GLA_DV = 256
GLA_RANK = 16
GLA_TAU = 16.0
GLA_CHUNK = 64
GLA_KEY = GLA_HEADS * GLA_DK
GLA_VAL = GLA_HEADS * GLA_DV
NSA_HEADS = 16
NSA_KV_HEADS = 4
NSA_GROUP = NSA_HEADS // NSA_KV_HEADS
NSA_D = 64
NSA_OUT = NSA_HEADS * NSA_D
NSA_KVW = NSA_KV_HEADS * NSA_D
CMP_STRIDE = 16
CMP_LEN = 2 * CMP_STRIDE
CMP_HID = 64
SEL_BLOCK = 64
N_SEL = 16
WINDOW = 512
NSA_QBLK = 32
SB_HEADS = 16
SB_D = 64
SB_OUT = SB_HEADS * SB_D
SB_QBLK = 128
N_BUCKETS = 32
RPB_MAX_EXACT = 16
RPB_MAX_DIST = 128
NEG = -1e30
FORCE = 1e9
IN_SPLITS = (GLA_KEY, GLA_KEY, GLA_VAL, GLA_RANK, GLA_VAL, NSA_OUT, 6 * NSA_KVW, 3 * NSA_HEADS, SB_OUT, 2 * SB_OUT,
             3 * D_MODEL)
N_IN = sum(IN_SPLITS)

BF16 = jnp.bfloat16
F32 = jnp.float32

VMEM_LIMIT_BYTES = 56 * 1024 * 1024
LANES = 128
ROW_TILE = 640
FF_TILE = 512
PROJ_TILE = 1152
N_IN_PAD = 14976
OUT_TILE = 512


def _rms_bf16(x, g):
    y = x * lax.rsqrt(jnp.mean(x * x, axis=-1, keepdims=True) + NORM_EPS)
    return (y * g).astype(BF16)


def _ffn_kernel(x_ref, g_ref, wg_ref, wu_ref, wd_ref, o_ref, h_ref, acc_ref):
    f = pl.program_id(1)

    @pl.when(f == 0)
    def _():
        h_ref[...] = _rms_bf16(x_ref[...], g_ref[...])
        acc_ref[...] = jnp.zeros_like(acc_ref)

    h = h_ref[...]
    a = jnp.dot(h, wg_ref[...], preferred_element_type=F32)
    b = jnp.dot(h, wu_ref[...], preferred_element_type=F32)
    act = (a * jax.nn.sigmoid(a) * b).astype(BF16)
    acc_ref[...] += jnp.dot(act, wd_ref[...], preferred_element_type=F32)

    @pl.when(f == pl.num_programs(1) - 1)
    def _():
        o_ref[...] = x_ref[...] + 0.5 * acc_ref[...]


def ffn_half_step(x, g, wg, wu, wd):
    m = x.shape[0]
    return pl.pallas_call(
        _ffn_kernel,
        grid=(m // ROW_TILE, D_FF // FF_TILE),
        in_specs=[
            pl.BlockSpec((ROW_TILE, D_MODEL), lambda i, f: (i, 0)),
            pl.BlockSpec((1, D_MODEL), lambda i, f: (0, 0)),
            pl.BlockSpec((D_MODEL, FF_TILE), lambda i, f: (0, f)),
            pl.BlockSpec((D_MODEL, FF_TILE), lambda i, f: (0, f)),
            pl.BlockSpec((FF_TILE, D_MODEL), lambda i, f: (f, 0)),
        ],
        out_specs=pl.BlockSpec((ROW_TILE, D_MODEL), lambda i, f: (i, 0)),
        out_shape=jax.ShapeDtypeStruct((m, D_MODEL), F32),
        scratch_shapes=[pltpu.VMEM((ROW_TILE, D_MODEL), BF16), pltpu.VMEM((ROW_TILE, D_MODEL), F32)],
        compiler_params=pltpu.CompilerParams(dimension_semantics=("parallel", "arbitrary"),
                                             vmem_limit_bytes=VMEM_LIMIT_BYTES),
        name="ffn_half_step",
    )(x, g.reshape(1, D_MODEL), wg, wu, wd)


def _proj_kernel(x_ref, g_ref, w_ref, o_ref, h_ref):
    @pl.when(pl.program_id(1) == 0)
    def _():
        h_ref[...] = _rms_bf16(x_ref[...], g_ref[...])

    o_ref[...] = jnp.dot(h_ref[...], w_ref[...], preferred_element_type=F32)


def in_projection(x, g, w):
    m = x.shape[0]
    return pl.pallas_call(
        _proj_kernel,
        grid=(m // ROW_TILE, N_IN_PAD // PROJ_TILE),
        in_specs=[
            pl.BlockSpec((ROW_TILE, D_MODEL), lambda i, n: (i, 0)),
            pl.BlockSpec((1, D_MODEL), lambda i, n: (0, 0)),
            pl.BlockSpec((D_MODEL, PROJ_TILE), lambda i, n: (0, n)),
        ],
        out_specs=pl.BlockSpec((ROW_TILE, PROJ_TILE), lambda i, n: (i, n)),
        out_shape=jax.ShapeDtypeStruct((m, N_IN_PAD), F32),
        scratch_shapes=[pltpu.VMEM((ROW_TILE, D_MODEL), BF16)],
        compiler_params=pltpu.CompilerParams(dimension_semantics=("parallel", "arbitrary"),
                                             vmem_limit_bytes=VMEM_LIMIT_BYTES),
        name="in_projection",
    )(x, g.reshape(1, D_MODEL), w)


def _merge_kernel(oa_ref, ob_ref, oc_ref, ga_ref, gb_ref, gc_ref, wa_ref, wb_ref, wc_ref, y_ref):
    ya = jnp.dot(oa_ref[...], wa_ref[...], preferred_element_type=F32)
    yb = jnp.dot(ob_ref[...], wb_ref[...], preferred_element_type=F32)
    yc = jnp.dot(oc_ref[...], wc_ref[...], preferred_element_type=F32)
    y = jax.nn.sigmoid(ga_ref[...]) * ya + jax.nn.sigmoid(gb_ref[...]) * yb + jax.nn.sigmoid(gc_ref[...]) * yc
    y_ref[...] = y.astype(BF16)


def merge_branches(o_gla, o_nsa, o_sb, gates, wa, wb, wc):
    m = o_gla.shape[0]
    nb = D_MODEL // OUT_TILE
    o_spec = lambda width: pl.BlockSpec((ROW_TILE, width), lambda i, n: (i, 0))
    g_spec = lambda k: pl.BlockSpec((ROW_TILE, OUT_TILE), lambda i, n, k=k: (i, n + k * nb))
    w_spec = lambda width: pl.BlockSpec((width, OUT_TILE), lambda i, n: (0, n))
    return pl.pallas_call(
        _merge_kernel,
        grid=(m // ROW_TILE, nb),
        in_specs=[o_spec(GLA_VAL), o_spec(NSA_OUT), o_spec(SB_OUT), g_spec(0), g_spec(1), g_spec(2),
                  w_spec(GLA_VAL), w_spec(NSA_OUT), w_spec(SB_OUT)],
        out_specs=pl.BlockSpec((ROW_TILE, OUT_TILE), lambda i, n: (i, n)),
        out_shape=jax.ShapeDtypeStruct((m, D_MODEL), BF16),
        compiler_params=pltpu.CompilerParams(dimension_semantics=("parallel", "arbitrary"),
                                             vmem_limit_bytes=VMEM_LIMIT_BYTES),
        name="merge_branches",
    )(o_gla, o_nsa, o_sb, gates, gates, gates, wa, wb, wc)


def _out_kernel(x_ref, y_ref, w_ref, o_ref):
    o_ref[...] = x_ref[...] + jnp.dot(y_ref[...], w_ref[...], preferred_element_type=F32)


def out_projection(x, y, w):
    m = x.shape[0]
    return pl.pallas_call(
        _out_kernel,
        grid=(m // ROW_TILE, D_MODEL // OUT_TILE),
        in_specs=[
            pl.BlockSpec((ROW_TILE, OUT_TILE), lambda i, n: (i, n)),
            pl.BlockSpec((ROW_TILE, D_MODEL), lambda i, n: (i, 0)),
            pl.BlockSpec((D_MODEL, OUT_TILE), lambda i, n: (0, n)),
        ],
        out_specs=pl.BlockSpec((ROW_TILE, OUT_TILE), lambda i, n: (i, n)),
        out_shape=jax.ShapeDtypeStruct((m, D_MODEL), F32),
        compiler_params=pltpu.CompilerParams(dimension_semantics=("parallel", "arbitrary"),
                                             vmem_limit_bytes=VMEM_LIMIT_BYTES),
        name="out_projection",
    )(x, y, w)


def _norm_kernel(x_ref, g_ref, o_ref):
    x = x_ref[...]
    o_ref[...] = x * lax.rsqrt(jnp.mean(x * x, axis=-1, keepdims=True) + NORM_EPS) * g_ref[...]


def final_rmsnorm(x, g):
    m = x.shape[0]
    return pl.pallas_call(
        _norm_kernel,
        grid=(m // ROW_TILE,),
        in_specs=[pl.BlockSpec((ROW_TILE, D_MODEL), lambda i: (i, 0)), pl.BlockSpec((1, D_MODEL), lambda i: (0, 0))],
        out_specs=pl.BlockSpec((ROW_TILE, D_MODEL), lambda i: (i, 0)),
        out_shape=jax.ShapeDtypeStruct((m, D_MODEL), F32),
        compiler_params=pltpu.CompilerParams(dimension_semantics=("parallel",), vmem_limit_bytes=VMEM_LIMIT_BYTES),
        name="final_rmsnorm",
    )(x, g.reshape(1, D_MODEL))


def _rmsnorm(x, g):
    xf = x.astype(jnp.float32)
    y = xf * lax.rsqrt(jnp.mean(xf * xf, axis=-1, keepdims=True) + NORM_EPS)
    return (y * g.astype(jnp.float32)).astype(x.dtype)


def _rel_bucket(dist):
    n = jnp.maximum(dist, 0)
    nf = jnp.maximum(n, 1).astype(jnp.float32)
    large = RPB_MAX_EXACT + (jnp.log(nf / RPB_MAX_EXACT) / math.log(RPB_MAX_DIST / RPB_MAX_EXACT)
                             * (N_BUCKETS - RPB_MAX_EXACT)).astype(jnp.int32)
    return jnp.where(n < RPB_MAX_EXACT, n, jnp.minimum(large, N_BUCKETS - 1))


def _gla_scan(q, k, v, log_a, s0):
    B, T = q.shape[:2]
    C = min(GLA_CHUNK, T)
    n_chunks = -(-T // C)
    pad = n_chunks * C - T

    def prep(a):
        a = jnp.pad(a, ((0, 0), (0, pad), (0, 0), (0, 0)))
        return a.reshape(B, n_chunks, C, a.shape[2], a.shape[3]).transpose(1, 0, 3, 2, 4)

    xs = (prep(q), prep(k), prep(v), prep(log_a.astype(jnp.float32)))
    tri = jnp.tril(jnp.ones((C, C), dtype=bool))

    def step(S, inp):
        qb, kb, vb, ab = inp
        qf, kf, vf = qb.astype(jnp.float32), kb.astype(jnp.float32), vb.astype(jnp.float32)
        b = jnp.cumsum(ab, axis=2)
        o_inter = jnp.einsum('bhcd,bhde->bhce', qf * jnp.exp(b), S)
        rel = jnp.where(tri[None, None, :, :, None], b[:, :, :, None, :] - b[:, :, None, :, :], -jnp.inf)
        att = jnp.einsum('bhtd,bhsd,bhtsd->bhts', qf, kf, jnp.exp(rel))
        o = o_inter + jnp.einsum('bhts,bhse->bhte', att, vf)
        b_last = b[:, :, -1:, :]
        S = jnp.exp(b_last[:, :, 0, :])[..., None] * S + jnp.einsum('bhsd,bhse->bhde', kf * jnp.exp(b_last - b), vf)
        return S, o

    S, o = lax.scan(step, s0.astype(jnp.float32), xs)
    o = o.transpose(1, 0, 3, 2, 4).reshape(B, n_chunks * C, GLA_HEADS, GLA_DV)[:, :T]
    return o.astype(q.dtype), S.astype(s0.dtype)


def _gla_mixer(q, k, v, a_low, r, s0, w_decay, b_decay, out_norm):
    B, T = q.shape[:2]
    q = q.reshape(B, T, GLA_HEADS, GLA_DK) * (GLA_DK ** -0.5)
    k = k.reshape(B, T, GLA_HEADS, GLA_DK)
    v = v.reshape(B, T, GLA_HEADS, GLA_DV)
    log_a = (jax.nn.log_sigmoid((a_low @ w_decay + b_decay).astype(jnp.float32)) / GLA_TAU).reshape(B, T, GLA_HEADS, GLA_DK)
    o, state = _gla_scan(q, k, v, log_a, s0)
    o = _rmsnorm(o, out_norm).reshape(B, T, GLA_VAL) * jax.nn.silu(r)
    return o, state


def _nsa_compress(rows, pos_emb, w1, w2):
    B, S = rows.shape[:2]
    n_sub = S // CMP_STRIDE
    sub = rows[:, :n_sub * CMP_STRIDE].reshape(B, n_sub, CMP_STRIDE, NSA_KV_HEADS, NSA_D)
    first = jnp.einsum('bnlgd,lde->bnge', sub + pos_emb[None, None, :CMP_STRIDE, None, :], w1[:CMP_STRIDE])
    second = jnp.einsum('bnlgd,lde->bnge', sub + pos_emb[None, None, CMP_STRIDE:, None, :], w1[CMP_STRIDE:])
    hid = jax.nn.silu(first[:, :-1] + second[:, 1:])
    return jnp.einsum('bnge,ed->bngd', hid, w2)


def _sel_blocks(rows):
    B, S = rows.shape[:2]
    ns = -(-S // SEL_BLOCK)
    rows = jnp.pad(rows, ((0, 0), (0, ns * SEL_BLOCK - S), (0, 0), (0, 0)))
    return rows.reshape(B, ns, SEL_BLOCK, NSA_KV_HEADS, NSA_D).transpose(0, 3, 1, 2, 4)


def _cmp_sel_map(nc, ns):
    c_start = jnp.arange(nc) * CMP_STRIDE
    s_start = jnp.arange(ns) * SEL_BLOCK
    hit = (c_start[:, None] < s_start[None, :] + SEL_BLOCK) & (c_start[:, None] + CMP_LEN > s_start[None, :])
    return hit.astype(jnp.float32)


def _nsa_block(q, gates, q_pos, kc, vc, c_end, ksb, vsb, wk, wv, w_pos, rpb):
    B, Q, G, R, d = q.shape
    scale = d ** -0.5
    table = rpb.astype(jnp.float32).reshape(N_BUCKETS, G, R)
    s = jnp.einsum('bqgrd,bngd->bqgrn', q, kc).astype(jnp.float32) * scale
    s = s + table[_rel_bucket(q_pos[:, None] - c_end[None, :])].transpose(0, 2, 3, 1)[None]
    m = (c_end[None, :] <= q_pos[:, None])[None, :, None, None, :]
    p_c = jax.nn.softmax(jnp.where(m, s, NEG), axis=-1) * m
    o_c = jnp.einsum('bqgrn,bngd->bqgrd', p_c.astype(vc.dtype), vc)
    ns = ksb.shape[2]
    imp = jnp.einsum('bqgn,nj->bqgj', p_c.sum(axis=3), _cmp_sel_map(kc.shape[1], ns))
    j = jnp.arange(ns)[None, :]
    cur = (q_pos // SEL_BLOCK)[:, None]
    valid = (j * SEL_BLOCK <= q_pos[:, None])[None, :, None, :]
    forced = ((j == 0) | (j == cur) | (j == cur - 1))[None, :, None, :]
    score = jnp.where(valid, jnp.where(forced, FORCE, imp), NEG)
    _, idx = lax.top_k(score, min(N_SEL, ns))
    idx = idx.transpose(0, 2, 1, 3)
    bi = jnp.arange(B)[:, None, None, None]
    gi = jnp.arange(G)[None, :, None, None]
    kg = ksb[bi, gi, idx]
    vg = vsb[bi, gi, idx]
    k_pos = idx[..., None] * SEL_BLOCK + jnp.arange(SEL_BLOCK)
    s = jnp.einsum('bqgrd,bgqnld->bgqrnl', q, kg).astype(jnp.float32) * scale
    bias = table.transpose(1, 0, 2)[gi[..., None], _rel_bucket(q_pos[None, None, :, None, None] - k_pos)]
    s = s + bias.transpose(0, 1, 2, 5, 3, 4)
    m = (k_pos <= q_pos[None, None, :, None, None])[:, :, :, None]
    shp = s.shape
    p = jax.nn.softmax(jnp.where(m, s, NEG).reshape(shp[0], shp[1], shp[2], shp[3], -1), axis=-1).reshape(shp)
    o_s = jnp.einsum('bgqrnl,bgqnld->bqgrd', p.astype(vg.dtype), vg)
    s = jnp.einsum('bqgrd,bsgd->bqgrs', q, wk).astype(jnp.float32) * scale
    dist = q_pos[:, None] - w_pos[None, :]
    s = s + table[_rel_bucket(dist)].transpose(0, 2, 3, 1)[None]
    m = ((dist >= 0) & (dist < WINDOW) & (w_pos[None, :] >= 0))[None, :, None, None, :]
    p = jax.nn.softmax(jnp.where(m, s, NEG), axis=-1)
    o_w = jnp.einsum('bqgrs,bsgd->bqgrd', p.astype(wv.dtype), wv)
    g = jax.nn.sigmoid(gates.astype(jnp.float32))
    o = g[..., 0:1] * o_c + g[..., 1:2] * o_s + g[..., 2:3] * o_w
    return o.reshape(B, Q, G * R * d).astype(q.dtype)


def _sb_block(q, q_pos, k, v, k_pos):
    z = jnp.einsum('bqhd,bshd->bhqs', q, k).astype(jnp.float32) * (SB_D ** -0.5)
    m = (k_pos[None, :] < q_pos[:, None])[None, None]
    log_fail = jnp.where(m, jax.nn.log_sigmoid(-z), 0.0)
    later = lax.cumsum(log_fail, axis=3, reverse=True) - log_fail
    a = jnp.where(m, jnp.exp(jax.nn.log_sigmoid(z) + later), 0.0)
    return jnp.einsum('bhqs,bshd->bqhd', a.astype(v.dtype), v)


def _split(z):
    cuts = [int(c) for c in np.cumsum(IN_SPLITS)[:-1]]
    return jnp.split(z[..., :N_IN], cuts, axis=-1)


def _mix_prompt(z, gla_w_decay, gla_b_decay, gla_out_norm, cmp_pos, cmp_w1, cmp_w2, rpb):
    B, T = z.shape[:2]
    gq, gk, gv, ga, gr, nq, nkv, ng, sq, skv, mg = _split(z)
    s0 = jnp.zeros((B, GLA_HEADS, GLA_DK, GLA_DV), jnp.float32)
    o_gla, gla_state = _gla_mixer(gq, gk, gv, ga, gr, s0, gla_w_decay, gla_b_decay, gla_out_norm)
    nkv = nkv.reshape(B, T, 6, NSA_KV_HEADS, NSA_D)
    kc = _nsa_compress(nkv[:, :, 0], cmp_pos[0], cmp_w1[0], cmp_w2[0])
    vc = _nsa_compress(nkv[:, :, 1], cmp_pos[1], cmp_w1[1], cmp_w2[1])
    c_end = jnp.arange(kc.shape[1]) * CMP_STRIDE + (CMP_LEN - 1)
    ksb, vsb = _sel_blocks(nkv[:, :, 2]), _sel_blocks(nkv[:, :, 3])
    wpad = ((0, 0), (WINDOW, 0), (0, 0), (0, 0))
    wk, wv = jnp.pad(nkv[:, :, 4], wpad), jnp.pad(nkv[:, :, 5], wpad)
    qb = min(NSA_QBLK, T)
    nb = T // qb
    qs = nq.reshape(B, nb, qb, NSA_KV_HEADS, NSA_GROUP, NSA_D).swapaxes(0, 1)
    gs = ng.reshape(B, nb, qb, NSA_KV_HEADS, NSA_GROUP, 3).swapaxes(0, 1)

    def nsa_q_block(inp):
        q_i, g_i, i = inp
        q0 = i * qb
        w_k = lax.dynamic_slice_in_dim(wk, q0, WINDOW + qb, axis=1)
        w_v = lax.dynamic_slice_in_dim(wv, q0, WINDOW + qb, axis=1)
        return _nsa_block(q_i, g_i, q0 + jnp.arange(qb), kc, vc, c_end, ksb, vsb, w_k, w_v,
                          q0 - WINDOW + jnp.arange(WINDOW + qb), rpb)

    o_nsa = lax.map(nsa_q_block, (qs, gs, jnp.arange(nb))).swapaxes(0, 1).reshape(B, T, NSA_OUT)
    skv = skv.reshape(B, T, 2, SB_HEADS, SB_D)
    sbq = min(SB_QBLK, T)
    nsb = T // sbq
    sqs = sq.reshape(B, nsb, sbq, SB_HEADS, SB_D).swapaxes(0, 1)
    k_pos = jnp.arange(T)

    def sb_q_block(inp):
        q_i, i = inp
        return _sb_block(q_i, i * sbq + jnp.arange(sbq), skv[:, :, 0], skv[:, :, 1], k_pos)

    o_sb = lax.map(sb_q_block, (sqs, jnp.arange(nsb))).swapaxes(0, 1).reshape(B, T, SB_OUT)
    keep = min(WINDOW, T)
    return (o_gla, o_nsa, o_sb, mg), nkv[:, :, :4], skv, nkv[:, T - keep:, 4:], gla_state


def _mix_sample(z, past_nsa, past_sb, win_buf, s0, past_len, gla_w_decay, gla_b_decay, gla_out_norm,
                cmp_pos, cmp_w1, cmp_w2, rpb):
    B, T = z.shape[:2]
    gq, gk, gv, ga, gr, nq, nkv, ng, sq, skv, mg = _split(z)
    o_gla, gla_state = _gla_mixer(gq, gk, gv, ga, gr, s0, gla_w_decay, gla_b_decay, gla_out_norm)
    nkv = nkv.reshape(B, T, 6, NSA_KV_HEADS, NSA_D)
    rows = jnp.concatenate([past_nsa, nkv[:, :, :4]], axis=1)
    kc = _nsa_compress(rows[:, :, 0], cmp_pos[0], cmp_w1[0], cmp_w2[0])
    vc = _nsa_compress(rows[:, :, 1], cmp_pos[1], cmp_w1[1], cmp_w2[1])
    c_end = jnp.arange(kc.shape[1]) * CMP_STRIDE + (CMP_LEN - 1)
    ksb, vsb = _sel_blocks(rows[:, :, 2]), _sel_blocks(rows[:, :, 3])
    win = jnp.concatenate([win_buf, nkv[:, :, 4:]], axis=1)
    wb = win_buf.shape[1]
    q_pos = past_len + jnp.arange(T)
    w_pos = (past_len - wb) + jnp.arange(wb + T)
    o_nsa = _nsa_block(nq.reshape(B, T, NSA_KV_HEADS, NSA_GROUP, NSA_D), ng.reshape(B, T, NSA_KV_HEADS, NSA_GROUP, 3),
                       q_pos, kc, vc, c_end, ksb, vsb, win[:, :, 0], win[:, :, 1], w_pos, rpb)
    skv = skv.reshape(B, T, 2, SB_HEADS, SB_D)
    sb_rows = jnp.concatenate([past_sb, skv], axis=1)
    o_sb = _sb_block(sq.reshape(B, T, SB_HEADS, SB_D), q_pos, sb_rows[:, :, 0], sb_rows[:, :, 1],
                     jnp.arange(past_len + T)).reshape(B, T, SB_OUT)
    keep = min(WINDOW, wb + T)
    return (o_gla, o_nsa, o_sb, mg), nkv[:, :, :4], skv, win[:, wb + T - keep:], gla_state


def kernel(x_prompt, x_sample, cache_nsa, cache_sb, cache_win, state_gla, page_table, ffn1_norm, ffn1_w_gate, ffn1_w_up, ffn1_w_down, mix_norm, w_in, gla_w_decay, gla_b_decay, gla_out_norm, nsa_cmp_pos, nsa_cmp_w1, nsa_cmp_w2, rpb_table, w_branch_gla, w_branch_nsa, w_branch_sb, w_out, ffn2_norm, ffn2_w_gate, ffn2_w_up, ffn2_w_down, final_norm):
    bp, tp, _ = x_prompt.shape
    bs, ts, _ = x_sample.shape
    mp, ms = bp * tp, bs * ts
    dec_batch, n_pages = page_table.shape
    past_len = n_pages * cache_nsa.shape[2]
    x = jnp.concatenate([x_prompt.reshape(mp, D_MODEL), x_sample.reshape(ms, D_MODEL)], axis=0)
    outs = [[] for _ in range(8)]
    for l in range(DEPTH):
        x = ffn_half_step(x, ffn1_norm[l], ffn1_w_gate[l].astype(BF16), ffn1_w_up[l].astype(BF16),
                          ffn1_w_down[l].astype(BF16))
        w_in_l = jnp.pad(w_in[l].astype(BF16), ((0, 0), (0, N_IN_PAD - N_IN)))
        z = in_projection(x, mix_norm[l], w_in_l)
        mix_w = (gla_w_decay[l], gla_b_decay[l], gla_out_norm[l], nsa_cmp_pos[l], nsa_cmp_w1[l], nsa_cmp_w2[l],
                 rpb_table)
        br_p, r_nsa, r_sb, r_win, r_gla = _mix_prompt(z[:mp].reshape(bp, tp, N_IN_PAD), *mix_w)
        for lst, val in zip(outs[0::2], (r_nsa, r_sb, r_win, r_gla)):
            lst.append(val)
        past_nsa = cache_nsa[page_table, l].reshape(dec_batch, past_len, 4, NSA_KV_HEADS, NSA_D)
        past_sb = cache_sb[page_table, l].reshape(dec_batch, past_len, 2, SB_HEADS, SB_D)
        br_s, r_nsa, r_sb, r_win, r_gla = _mix_sample(z[mp:].reshape(bs, ts, N_IN_PAD), past_nsa, past_sb,
                                                      cache_win[:, l], state_gla[:, l], past_len, *mix_w)
        for lst, val in zip(outs[1::2], (r_nsa, r_sb, r_win, r_gla)):
            lst.append(val)
        o_gla, o_nsa, o_sb, gates = (
            jnp.concatenate([a.reshape(mp, a.shape[-1]), b.reshape(ms, b.shape[-1])], axis=0)
            for a, b in zip(br_p, br_s))
        y = merge_branches(o_gla.astype(BF16), o_nsa.astype(BF16), o_sb.astype(BF16), gates,
                           w_branch_gla[l].astype(BF16), w_branch_nsa[l].astype(BF16), w_branch_sb[l].astype(BF16))
        x = out_projection(x, y, w_out[l].astype(BF16))
        x = ffn_half_step(x, ffn2_norm[l], ffn2_w_gate[l].astype(BF16), ffn2_w_up[l].astype(BF16),
                          ffn2_w_down[l].astype(BF16))
    y = final_rmsnorm(x, final_norm)
    return (y[:mp].reshape(bp, tp, D_MODEL), y[mp:].reshape(bs, ts, D_MODEL)) + tuple(
        jnp.stack(o, axis=1) for o in outs)
```

```python
import functools
import math

import jax
import jax.numpy as jnp
import numpy as np
from jax import lax
from jax.experimental import pallas as pl
from jax.experimental.pallas import tpu as pltpu

D_MODEL = 2048
DEPTH = 2
D_FF = 5632
NORM_EPS = 1e-6
GLA_HEADS = 4
GLA_DK = 128
GLA_DV = 256
GLA_RANK = 16
GLA_TAU = 16.0
GLA_CHUNK = 64
GLA_KEY = GLA_HEADS * GLA_DK
GLA_VAL = GLA_HEADS * GLA_DV
NSA_HEADS = 16
NSA_KV_HEADS = 4
NSA_GROUP = NSA_HEADS // NSA_KV_HEADS
NSA_D = 64
NSA_OUT = NSA_HEADS * NSA_D
NSA_KVW = NSA_KV_HEADS * NSA_D
CMP_STRIDE = 16
CMP_LEN = 2 * CMP_STRIDE
CMP_HID = 64
SEL_BLOCK = 64
N_SEL = 16
WINDOW = 512
NSA_QBLK = 32
SB_HEADS = 16
SB_D = 64
SB_OUT = SB_HEADS * SB_D
SB_QBLK = 128
N_BUCKETS = 32
RPB_MAX_EXACT = 16
RPB_MAX_DIST = 128
NEG = -1e30
FORCE = 1e9
IN_SPLITS = (GLA_KEY, GLA_KEY, GLA_VAL, GLA_RANK, GLA_VAL, NSA_OUT, 6 * NSA_KVW, 3 * NSA_HEADS, SB_OUT, 2 * SB_OUT,
             3 * D_MODEL)
N_IN = sum(IN_SPLITS)

BF16 = jnp.bfloat16
F32 = jnp.float32

VMEM_LIMIT_BYTES = 56 * 1024 * 1024
LANES = 128
ROW_TILE = 640
FF_TILE = 512
PROJ_TILE = 1152
N_IN_PAD = 14976
OUT_TILE = 512


def _rms_bf16(x, g):
    y = x * lax.rsqrt(jnp.mean(x * x, axis=-1, keepdims=True) + NORM_EPS)
    return (y * g).astype(BF16)


def _ffn_kernel(x_ref, g_ref, wg_ref, wu_ref, wd_ref, o_ref, h_ref, acc_ref):
    f = pl.program_id(1)

    @pl.when(f == 0)
    def _():
        h_ref[...] = _rms_bf16(x_ref[...], g_ref[...])
        acc_ref[...] = jnp.zeros_like(acc_ref)

    h = h_ref[...]
    a = jnp.dot(h, wg_ref[...], preferred_element_type=F32)
    b = jnp.dot(h, wu_ref[...], preferred_element_type=F32)
    act = (a * jax.nn.sigmoid(a) * b).astype(BF16)
    acc_ref[...] += jnp.dot(act, wd_ref[...], preferred_element_type=F32)

    @pl.when(f == pl.num_programs(1) - 1)
    def _():
        o_ref[...] = x_ref[...] + 0.5 * acc_ref[...]


def ffn_half_step(x, g, wg, wu, wd):
    m = x.shape[0]
    return pl.pallas_call(
        _ffn_kernel,
        grid=(m // ROW_TILE, D_FF // FF_TILE),
        in_specs=[
            pl.BlockSpec((ROW_TILE, D_MODEL), lambda i, f: (i, 0)),
            pl.BlockSpec((1, D_MODEL), lambda i, f: (0, 0)),
            pl.BlockSpec((D_MODEL, FF_TILE), lambda i, f: (0, f)),
            pl.BlockSpec((D_MODEL, FF_TILE), lambda i, f: (0, f)),
            pl.BlockSpec((FF_TILE, D_MODEL), lambda i, f: (f, 0)),
        ],
        out_specs=pl.BlockSpec((ROW_TILE, D_MODEL), lambda i, f: (i, 0)),
        out_shape=jax.ShapeDtypeStruct((m, D_MODEL), F32),
        scratch_shapes=[pltpu.VMEM((ROW_TILE, D_MODEL), BF16), pltpu.VMEM((ROW_TILE, D_MODEL), F32)],
        compiler_params=pltpu.CompilerParams(dimension_semantics=("parallel", "arbitrary"),
                                             vmem_limit_bytes=VMEM_LIMIT_BYTES),
        name="ffn_half_step",
    )(x, g.reshape(1, D_MODEL), wg, wu, wd)


def _proj_kernel(x_ref, g_ref, w_ref, o_ref, h_ref):
    @pl.when(pl.program_id(1) == 0)
    def _():
        h_ref[...] = _rms_bf16(x_ref[...], g_ref[...])

    o_ref[...] = jnp.dot(h_ref[...], w_ref[...], preferred_element_type=F32)


def in_projection(x, g, w):
    m = x.shape[0]
    return pl.pallas_call(
        _proj_kernel,
        grid=(m // ROW_TILE, N_IN_PAD // PROJ_TILE),
        in_specs=[
            pl.BlockSpec((ROW_TILE, D_MODEL), lambda i, n: (i, 0)),
            pl.BlockSpec((1, D_MODEL), lambda i, n: (0, 0)),
            pl.BlockSpec((D_MODEL, PROJ_TILE), lambda i, n: (0, n)),
        ],
        out_specs=pl.BlockSpec((ROW_TILE, PROJ_TILE), lambda i, n: (i, n)),
        out_shape=jax.ShapeDtypeStruct((m, N_IN_PAD), F32),
        scratch_shapes=[pltpu.VMEM((ROW_TILE, D_MODEL), BF16)],
        compiler_params=pltpu.CompilerParams(dimension_semantics=("parallel", "arbitrary"),
                                             vmem_limit_bytes=VMEM_LIMIT_BYTES),
        name="in_projection",
    )(x, g.reshape(1, D_MODEL), w)


def _merge_kernel(oa_ref, ob_ref, oc_ref, ga_ref, gb_ref, gc_ref, wa_ref, wb_ref, wc_ref, y_ref):
    ya = jnp.dot(oa_ref[...], wa_ref[...], preferred_element_type=F32)
    yb = jnp.dot(ob_ref[...], wb_ref[...], preferred_element_type=F32)
    yc = jnp.dot(oc_ref[...], wc_ref[...], preferred_element_type=F32)
    y = jax.nn.sigmoid(ga_ref[...]) * ya + jax.nn.sigmoid(gb_ref[...]) * yb + jax.nn.sigmoid(gc_ref[...]) * yc
    y_ref[...] = y.astype(BF16)


def merge_branches(o_gla, o_nsa, o_sb, gates, wa, wb, wc):
    m = o_gla.shape[0]
    nb = D_MODEL // OUT_TILE
    o_spec = lambda width: pl.BlockSpec((ROW_TILE, width), lambda i, n: (i, 0))
    g_spec = lambda k: pl.BlockSpec((ROW_TILE, OUT_TILE), lambda i, n, k=k: (i, n + k * nb))
    w_spec = lambda width: pl.BlockSpec((width, OUT_TILE), lambda i, n: (0, n))
    return pl.pallas_call(
        _merge_kernel,
        grid=(m // ROW_TILE, nb),
        in_specs=[o_spec(GLA_VAL), o_spec(NSA_OUT), o_spec(SB_OUT), g_spec(0), g_spec(1), g_spec(2),
                  w_spec(GLA_VAL), w_spec(NSA_OUT), w_spec(SB_OUT)],
        out_specs=pl.BlockSpec((ROW_TILE, OUT_TILE), lambda i, n: (i, n)),
        out_shape=jax.ShapeDtypeStruct((m, D_MODEL), BF16),
        compiler_params=pltpu.CompilerParams(dimension_semantics=("parallel", "arbitrary"),
                                             vmem_limit_bytes=VMEM_LIMIT_BYTES),
        name="merge_branches",
    )(o_gla, o_nsa, o_sb, gates, gates, gates, wa, wb, wc)


def _out_kernel(x_ref, y_ref, w_ref, o_ref):
    o_ref[...] = x_ref[...] + jnp.dot(y_ref[...], w_ref[...], preferred_element_type=F32)


def out_projection(x, y, w):
    m = x.shape[0]
    return pl.pallas_call(
        _out_kernel,
        grid=(m // ROW_TILE, D_MODEL // OUT_TILE),
        in_specs=[
            pl.BlockSpec((ROW_TILE, OUT_TILE), lambda i, n: (i, n)),
            pl.BlockSpec((ROW_TILE, D_MODEL), lambda i, n: (i, 0)),
            pl.BlockSpec((D_MODEL, OUT_TILE), lambda i, n: (0, n)),
        ],
        out_specs=pl.BlockSpec((ROW_TILE, OUT_TILE), lambda i, n: (i, n)),
        out_shape=jax.ShapeDtypeStruct((m, D_MODEL), F32),
        compiler_params=pltpu.CompilerParams(dimension_semantics=("parallel", "arbitrary"),
                                             vmem_limit_bytes=VMEM_LIMIT_BYTES),
        name="out_projection",
    )(x, y, w)


def _norm_kernel(x_ref, g_ref, o_ref):
    x = x_ref[...]
    o_ref[...] = x * lax.rsqrt(jnp.mean(x * x, axis=-1, keepdims=True) + NORM_EPS) * g_ref[...]


def final_rmsnorm(x, g):
    m = x.shape[0]
    return pl.pallas_call(
        _norm_kernel,
        grid=(m // ROW_TILE,),
        in_specs=[pl.BlockSpec((ROW_TILE, D_MODEL), lambda i: (i, 0)), pl.BlockSpec((1, D_MODEL), lambda i: (0, 0))],
        out_specs=pl.BlockSpec((ROW_TILE, D_MODEL), lambda i: (i, 0)),
        out_shape=jax.ShapeDtypeStruct((m, D_MODEL), F32),
        compiler_params=pltpu.CompilerParams(dimension_semantics=("parallel",), vmem_limit_bytes=VMEM_LIMIT_BYTES),
        name="final_rmsnorm",
    )(x, g.reshape(1, D_MODEL))


NSA_TQ = 128
NSA_TK = 256
SEL_TOEP = 4
WIN_TOEP = 6


def _toeplitz_bias(rpb, window):
    n = WIN_TOEP if window else SEL_TOEP
    u = jnp.arange(NSA_TQ)[:, None]
    w = jnp.arange(NSA_TK)[None, :]
    d = (jnp.arange(n) * NSA_TQ)[:, None, None] + (u - w)[None]
    b = rpb.astype(F32)[_rel_bucket(d)]
    ok = (d >= 0) & (d < WINDOW) if window else (d >= 0)
    b = jnp.where(ok[..., None], b, NEG)
    return b.transpose(3, 0, 1, 2).reshape(NSA_KV_HEADS, NSA_GROUP, n, NSA_TQ, NSA_TK)


def _split_hi_lo(x):
    hi = x.astype(BF16)
    lo = (x - hi.astype(F32)).astype(BF16)
    return hi, lo


def _nsa_prompt_kernel(q_ref, gate_ref, cmp_ref, kv_ref, posa_ref, posb_ref, w1a_ref, w1b_ref, w2_ref,
                       biasc_ref, stoep_ref, wtoep_ref, mapt_ref, expand_ref, o_ref,
                       kc_ref, vc_ref, m_ref, l_ref, acc_ref, *, n_cmp):
    qi = pl.program_id(2)
    tq, tk, R = NSA_TQ, NSA_TK, NSA_GROUP
    rows = R * tq

    @pl.when(qi == 0)
    def _():
        for t, dst in ((0, kc_ref), (1, vc_ref)):
            x = cmp_ref[0, 0, t]
            first = jnp.dot((x + posa_ref[t]).astype(BF16), w1a_ref[t], preferred_element_type=F32)
            second = jnp.dot((x + posb_ref[t]).astype(BF16), w1b_ref[t], preferred_element_type=F32)
            pre = first + pltpu.roll(second, second.shape[0] - 1, 0)
            hid = (pre * jax.nn.sigmoid(pre)).astype(BF16)
            dst[...] = jnp.dot(hid, w2_ref[t], preferred_element_type=F32).astype(BF16)

    q4 = q_ref[0, 0].reshape(rows, NSA_D)
    q_pos = qi * tq + lax.broadcasted_iota(jnp.int32, (tq, 1), 0)

    ncp = kc_ref.shape[0]
    s = lax.dot_general(q4, kc_ref[...], (((1,), (1,)), ((), ())), preferred_element_type=F32)
    s = s.reshape(R, tq, ncp) + biasc_ref[0]
    n_idx = lax.broadcasted_iota(jnp.int32, (tq, ncp), 1)
    ok = ((n_idx * CMP_STRIDE + (CMP_LEN - 1)) <= q_pos) & (n_idx < n_cmp)
    s = jnp.where(ok[None], s, NEG)
    e = jnp.exp(s - jnp.max(s, axis=-1, keepdims=True))
    p_c = jnp.where(ok[None], e / jnp.sum(e, axis=-1, keepdims=True), 0.0)
    o_c = jnp.dot(p_c.reshape(rows, ncp).astype(BF16), vc_ref[...], preferred_element_type=F32)
    psum = jnp.sum(p_c, axis=0)
    p_hi, p_lo = _split_hi_lo(psum)
    nt = (((1,), (1,)), ((), ()))
    imp = (lax.dot_general(mapt_ref[...], p_hi, nt, preferred_element_type=F32)
           + lax.dot_general(mapt_ref[...], p_lo, nt, preferred_element_type=F32))

    ns = imp.shape[0]
    j_idx = lax.broadcasted_iota(jnp.int32, (ns, tq), 0)
    t_pos = qi * tq + lax.broadcasted_iota(jnp.int32, (ns, tq), 1)
    cur = t_pos // SEL_BLOCK
    valid = j_idx * SEL_BLOCK <= t_pos
    forced = (j_idx == 0) | (j_idx == cur) | (j_idx == cur - 1)
    score = jnp.where(valid, jnp.where(forced, FORCE, imp), NEG)
    rank = jnp.zeros((ns, tq), F32)
    for i in range(ns):
        row = score[i:i + 1, :]
        beats = (row > score) | ((row == score) & (j_idx > i))
        rank = rank + jnp.where(beats, 1.0, 0.0)
    sel_t = jnp.where((rank < float(min(N_SEL, ns))) & valid, 1.0, 0.0)
    sel = sel_t.T.astype(BF16)

    def attend(k_idx, v_idx, toep_ref, n_toep, j_lo, use_sel):
        m_ref[...] = jnp.full(m_ref.shape, NEG, F32)
        l_ref[...] = jnp.zeros(l_ref.shape, F32)
        acc_ref[...] = jnp.zeros(acc_ref.shape, F32)
        j_hi = qi // 2

        def body(step, carry):
            j = j_hi - step
            k = kv_ref[0, 0, k_idx, pl.ds(pl.multiple_of(j * tk, tk), tk), :]
            v = kv_ref[0, 0, v_idx, pl.ds(pl.multiple_of(j * tk, tk), tk), :]
            sc = lax.dot_general(q4, k, (((1,), (1,)), ((), ())), preferred_element_type=F32)
            off = jnp.minimum(qi - 2 * j, n_toep - 1)
            sc = sc.reshape(R, tq, tk) + toep_ref[0, :, off]
            if use_sel:
                hit = jnp.dot(sel, expand_ref[j], preferred_element_type=F32)
                sc = jnp.where((hit > 0.5)[None], sc, NEG)
            sc = sc.reshape(rows, tk)
            m_old = m_ref[...]
            m_new = jnp.maximum(m_old, jnp.max(sc, axis=-1, keepdims=True))
            alpha = jnp.exp(m_old - m_new)
            p = jnp.exp(sc - m_new)
            l_ref[...] = alpha * l_ref[...] + jnp.sum(p, axis=-1, keepdims=True)
            acc_ref[...] = alpha * acc_ref[...] + jnp.dot(p.astype(BF16), v, preferred_element_type=F32)
            m_ref[...] = m_new
            return carry

        lax.fori_loop(0, j_hi - j_lo + 1, body, 0)
        return acc_ref[...] / l_ref[...]

    o_s = attend(0, 1, stoep_ref, SEL_TOEP, 0, True)
    o_w = attend(2, 3, wtoep_ref, WIN_TOEP, jnp.maximum(qi * tq - (WINDOW - 1), 0) // tk, False)

    g = jax.nn.sigmoid(gate_ref[0, 0])
    for r in range(R):
        sl = slice(r * tq, (r + 1) * tq)
        o = (g[:, 3 * r:3 * r + 1] * o_c[sl] + g[:, 3 * r + 1:3 * r + 2] * o_s[sl]
             + g[:, 3 * r + 2:3 * r + 3] * o_w[sl])
        o_ref[0, 0, r] = o.astype(o_ref.dtype)


def nsa_prompt(nq, ng, nkv, cmp_pos, cmp_w1, cmp_w2, rpb):
    B, T = nq.shape[:2]
    G, R, d = NSA_KV_HEADS, NSA_GROUP, NSA_D
    n_sub = T // CMP_STRIDE
    n_cmp = n_sub - 1
    ns = T // SEL_BLOCK
    nq_t = T // NSA_TQ
    nk_t = T // NSA_TK
    q = (nq * (d ** -0.5)).astype(BF16).reshape(B, T, G, R, d).transpose(0, 2, 3, 1, 4)
    gates = ng.reshape(B, T, G, 3 * R).transpose(0, 2, 1, 3)
    cmp_rows = nkv[:, :, 0:2].transpose(0, 3, 2, 1, 4).reshape(B, G, 2, n_sub, CMP_STRIDE * d)
    kv = nkv[:, :, 2:6].astype(BF16).transpose(0, 3, 2, 1, 4)
    pos_a = cmp_pos[:, :CMP_STRIDE].reshape(2, 1, CMP_STRIDE * d)
    pos_b = cmp_pos[:, CMP_STRIDE:].reshape(2, 1, CMP_STRIDE * d)
    w1a = cmp_w1[:, :CMP_STRIDE].reshape(2, CMP_STRIDE * d, CMP_HID).astype(BF16)
    w1b = cmp_w1[:, CMP_STRIDE:].reshape(2, CMP_STRIDE * d, CMP_HID).astype(BF16)
    w2 = cmp_w2.astype(BF16)
    t_pos = jnp.arange(T)[:, None]
    c_end = jnp.arange(n_sub)[None, :] * CMP_STRIDE + (CMP_LEN - 1)
    bias_c = rpb.astype(F32)[_rel_bucket(t_pos - c_end)].transpose(2, 0, 1).reshape(G, R, T, n_sub)
    stoep = _toeplitz_bias(rpb, False)
    wtoep = _toeplitz_bias(rpb, True)
    map_t = _cmp_sel_map(n_sub, ns).T.astype(BF16)
    expand = (jnp.arange(ns)[None, :, None]
              == ((jnp.arange(nk_t)[:, None, None] * NSA_TK + jnp.arange(NSA_TK)[None, None, :]) // SEL_BLOCK)
              ).astype(BF16)
    full = lambda a: pl.BlockSpec(a.shape, lambda b, g, i: (0,) * a.ndim)
    per_g = lambda a: pl.BlockSpec((1,) + a.shape[1:], lambda b, g, i: (g,) + (0,) * (a.ndim - 1))
    o = pl.pallas_call(
        functools.partial(_nsa_prompt_kernel, n_cmp=n_cmp),
        grid=(B, G, nq_t),
        in_specs=[
            pl.BlockSpec((1, 1, R, NSA_TQ, d), lambda b, g, i: (b, g, 0, i, 0)),
            pl.BlockSpec((1, 1, NSA_TQ, 3 * R), lambda b, g, i: (b, g, i, 0)),
            pl.BlockSpec((1, 1, 2, n_sub, CMP_STRIDE * d), lambda b, g, i: (b, g, 0, 0, 0)),
            pl.BlockSpec((1, 1, 4, T, d), lambda b, g, i: (b, g, 0, 0, 0)),
            full(pos_a), full(pos_b), full(w1a), full(w1b), full(w2),
            pl.BlockSpec((1, R, NSA_TQ, n_sub), lambda b, g, i: (g, 0, i, 0)),
            per_g(stoep), per_g(wtoep), full(map_t), full(expand),
        ],
        out_specs=pl.BlockSpec((1, 1, R, NSA_TQ, d), lambda b, g, i: (b, g, 0, i, 0)),
        out_shape=jax.ShapeDtypeStruct((B, G, R, T, d), BF16),
        scratch_shapes=[pltpu.VMEM((n_sub, d), BF16), pltpu.VMEM((n_sub, d), BF16),
                        pltpu.VMEM((R * NSA_TQ, 1), F32), pltpu.VMEM((R * NSA_TQ, 1), F32),
                        pltpu.VMEM((R * NSA_TQ, d), F32)],
        compiler_params=pltpu.CompilerParams(dimension_semantics=("parallel", "parallel", "arbitrary"),
                                             vmem_limit_bytes=VMEM_LIMIT_BYTES),
        name="nsa_prompt",
    )(q, gates, cmp_rows, kv, pos_a, pos_b, w1a, w1b, w2, bias_c, stoep, wtoep, map_t, expand)
    return o.transpose(0, 3, 1, 2, 4).reshape(B, T, G * R * d)


SB_TQ = 256
SB_TK = 128


def _softplus(z):
    return jnp.maximum(z, 0.0) + jnp.log(1.0 + jnp.exp(-jnp.abs(z)))


def _suffix_matrix(n):
    j = jnp.arange(n)[:, None]
    s = jnp.arange(n)[None, :]
    return jnp.concatenate([(j > s), jnp.ones((n, n), bool)], axis=1).astype(BF16)


def _sb_tile(z, ok, v, suffix, c_ref, acc_ref):
    tk = z.shape[1]
    sp = _softplus(z)
    lf = -sp if ok is None else jnp.where(ok, -sp, 0.0)
    hi, lo = _split_hi_lo(lf)
    tot = (jnp.dot(hi, suffix, preferred_element_type=F32) + jnp.dot(lo, suffix, preferred_element_type=F32))
    later = tot[:, :tk] + c_ref[...]
    a = jnp.exp(z - sp + later)
    if ok is not None:
        a = jnp.where(ok, a, 0.0)
    acc_ref[...] += jnp.dot(a.astype(BF16), v, preferred_element_type=F32)
    c_ref[...] += tot[:, tk:]


def _sb_prompt_kernel(q_ref, k_ref, v_ref, suffix_ref, o_ref, c_ref, acc_ref):
    qi = pl.program_id(2)
    tq, tk = SB_TQ, SB_TK
    per = tq // tk
    q = q_ref[0, 0]
    suffix = suffix_ref[...]
    c_ref[...] = jnp.zeros(c_ref.shape, F32)
    acc_ref[...] = jnp.zeros(acc_ref.shape, F32)
    nt = (((1,), (1,)), ((), ()))
    q_pos = qi * tq + lax.broadcasted_iota(jnp.int32, (tq, tk), 0)
    k_off = lax.broadcasted_iota(jnp.int32, (tq, tk), 1)

    def tile(j, masked):
        k = k_ref[0, 0, pl.ds(pl.multiple_of(j * tk, tk), tk), :]
        v = v_ref[0, 0, pl.ds(pl.multiple_of(j * tk, tk), tk), :]
        z = lax.dot_general(q, k, nt, preferred_element_type=F32)
        ok = (j * tk + k_off < q_pos) if masked else None
        _sb_tile(z, ok, v, suffix, c_ref, acc_ref)

    for t in range(per):
        tile(qi * per + (per - 1 - t), True)

    def body(step, carry):
        tile(qi * per - 1 - step, False)
        return carry

    lax.fori_loop(0, qi * per, body, 0)
    o_ref[0, 0] = acc_ref[...].astype(o_ref.dtype)


def sb_prompt(sq, skv):
    B, T = sq.shape[:2]
    H, d = SB_HEADS, SB_D
    q = (sq * (d ** -0.5)).astype(BF16).reshape(B, T, H, d).transpose(0, 2, 1, 3)
    k = skv[:, :, 0].astype(BF16).transpose(0, 2, 1, 3)
    v = skv[:, :, 1].astype(BF16).transpose(0, 2, 1, 3)
    suffix = _suffix_matrix(SB_TK)
    kv_spec = pl.BlockSpec((1, 1, T, d), lambda b, h, i: (b, h, 0, 0))
    q_spec = pl.BlockSpec((1, 1, SB_TQ, d), lambda b, h, i: (b, h, i, 0))
    o = pl.pallas_call(
        _sb_prompt_kernel,
        grid=(B, H, T // SB_TQ),
        in_specs=[q_spec, kv_spec, kv_spec, pl.BlockSpec(suffix.shape, lambda b, h, i: (0, 0))],
        out_specs=q_spec,
        out_shape=jax.ShapeDtypeStruct((B, H, T, d), BF16),
        scratch_shapes=[pltpu.VMEM((SB_TQ, SB_TK), F32), pltpu.VMEM((SB_TQ, d), F32)],
        compiler_params=pltpu.CompilerParams(dimension_semantics=("parallel", "parallel", "arbitrary"),
                                             vmem_limit_bytes=VMEM_LIMIT_BYTES),
        name="sb_prompt",
    )(q, k, v, suffix)
    return o.transpose(0, 2, 1, 3).reshape(B, T, H * d)


GLA_SUB = 16
GLA_EXP_CAP = 60.0


def _gla_kernel(q_ref, k_ref, v_ref, a_ref, r_ref, wd_ref, bd_ref, gn_ref, s0_ref, o_ref, st_ref, s_ref, *, sub):
    c = pl.program_id(2)
    C = q_ref.shape[1]

    @pl.when(c == 0)
    def _():
        s_ref[...] = s0_ref[0, 0]

    q = q_ref[0] * (GLA_DK ** -0.5)
    k = k_ref[0]
    v = v_ref[0].astype(BF16)
    pre = jnp.dot(a_ref[0].astype(BF16), wd_ref[...], preferred_element_type=F32) + bd_ref[...]
    log_a = -_softplus(-pre) * (1.0 / GLA_TAU)
    row = lax.broadcasted_iota(jnp.int32, (C, C), 0)
    col = lax.broadcasted_iota(jnp.int32, (C, C), 1)
    lower = jnp.where(col <= row, 1.0, 0.0).astype(BF16)
    a_hi, a_lo = _split_hi_lo(log_a)
    b = jnp.dot(lower, a_hi, preferred_element_type=F32) + jnp.dot(lower, a_lo, preferred_element_type=F32)

    s_old = s_ref[...]
    o = jnp.dot((q * jnp.exp(b)).astype(BF16), s_old.astype(BF16), preferred_element_type=F32)

    nt = (((1,), (1,)), ((), ()))
    parts = []
    for i in range(C // sub):
        lo, hi = i * sub, (i + 1) * sub
        ref_row = b[lo:lo + 1, :]
        qs = (q[lo:hi] * jnp.exp(b[lo:hi] - ref_row)).astype(BF16)
        ks = (k[:hi] * jnp.exp(jnp.minimum(ref_row - b[:hi], GLA_EXP_CAP))).astype(BF16)
        att = lax.dot_general(qs, ks, nt, preferred_element_type=F32)
        t_idx = lo + lax.broadcasted_iota(jnp.int32, (sub, hi), 0)
        s_idx = lax.broadcasted_iota(jnp.int32, (sub, hi), 1)
        att = jnp.where(s_idx <= t_idx, att, 0.0)
        parts.append(jnp.dot(att.astype(BF16), v[:hi], preferred_element_type=F32))
    o = o + (parts[0] if len(parts) == 1 else jnp.concatenate(parts, axis=0))

    b_t = b.T
    last = b_t[:, C - 1:C]
    kd_t = (k.T * jnp.exp(last - b_t)).astype(BF16)
    s_new = jnp.exp(last) * s_old + jnp.dot(kd_t, v, preferred_element_type=F32)
    s_ref[...] = s_new

    y = o * lax.rsqrt(jnp.mean(o * o, axis=-1, keepdims=True) + NORM_EPS) * gn_ref[...]
    r = r_ref[0]
    o_ref[0] = (y * (r * jax.nn.sigmoid(r))).astype(o_ref.dtype)

    @pl.when(c == pl.num_programs(2) - 1)
    def _():
        st_ref[0, 0] = s_new


def gla_mixer(gq, gk, gv, ga, gr, s0, w_decay, b_decay, out_norm):
    B, T = gq.shape[:2]
    H, dk, dv = GLA_HEADS, GLA_DK, GLA_DV
    C = min(GLA_CHUNK, T)
    sub = min(GLA_SUB, C)
    tok = lambda w: pl.BlockSpec((1, C, w), lambda b, h, c: (b, c, h))
    o, st = pl.pallas_call(
        functools.partial(_gla_kernel, sub=sub),
        grid=(B, H, T // C),
        in_specs=[
            tok(dk), tok(dk), tok(dv),
            pl.BlockSpec((1, C, GLA_RANK), lambda b, h, c: (b, c, 0)),
            tok(dv),
            pl.BlockSpec((GLA_RANK, dk), lambda b, h, c: (0, h)),
            pl.BlockSpec((1, dk), lambda b, h, c: (0, h)),
            pl.BlockSpec((1, dv), lambda b, h, c: (0, 0)),
            pl.BlockSpec((1, 1, dk, dv), lambda b, h, c: (b, h, 0, 0)),
        ],
        out_specs=[tok(dv), pl.BlockSpec((1, 1, dk, dv), lambda b, h, c: (b, h, 0, 0))],
        out_shape=[jax.ShapeDtypeStruct((B, T, H * dv), BF16), jax.ShapeDtypeStruct((B, H, dk, dv), F32)],
        scratch_shapes=[pltpu.VMEM((dk, dv), F32)],
        compiler_params=pltpu.CompilerParams(dimension_semantics=("parallel", "parallel", "arbitrary"),
                                             vmem_limit_bytes=VMEM_LIMIT_BYTES),
        name="gla_mixer",
    )(gq, gk, gv, ga, gr, w_decay.astype(BF16), b_decay.reshape(1, H * dk), out_norm.reshape(1, dv), s0)
    return o, st


def _sb_sample_kernel(pt_ref, q_ref, new_ref, page_ref, suffix_ref, o_ref, c_ref, acc_ref, *, n_new):
    s = pl.program_id(1)
    H, d = SB_HEADS, SB_D
    rows = q_ref.shape[1]
    nt = (((1,), (1,)), ((), ()))

    def tile(blk, ok):
        k = blk[:, :H * d].astype(BF16)
        v = blk[:, H * d:].astype(BF16)
        z = lax.dot_general(q_ref[0], k, nt, preferred_element_type=F32)
        _sb_tile(z, ok, v, suffix_ref[...], c_ref, acc_ref)

    @pl.when(s == 0)
    def _():
        c_ref[...] = jnp.zeros(c_ref.shape, F32)
        acc_ref[...] = jnp.zeros(acc_ref.shape, F32)
        t_idx = lax.broadcasted_iota(jnp.int32, c_ref.shape, 0) % n_new
        k_idx = lax.broadcasted_iota(jnp.int32, c_ref.shape, 1)
        tile(new_ref[0], k_idx < t_idx)

    @pl.when(s > 0)
    def _():
        tile(page_ref[0, 0], None)

    @pl.when(s == pl.num_programs(1) - 1)
    def _():
        acc = acc_ref[...]
        r_idx = lax.broadcasted_iota(jnp.int32, acc.shape, 0) // n_new
        l_idx = lax.broadcasted_iota(jnp.int32, acc.shape, 1) // d
        own = jnp.where(r_idx == l_idx, acc, 0.0)
        o_ref[0] = jnp.sum(own.reshape(H, n_new, H * d), axis=0).astype(o_ref.dtype)


def sb_sample(sq, skv, cache_sb, page_table, layer):
    B, tn = sq.shape[:2]
    H, d = SB_HEADS, SB_D
    n_pages = page_table.shape[1]
    page = cache_sb.shape[2]
    cache = cache_sb.reshape(cache_sb.shape[0], cache_sb.shape[1], page, 2 * H * d)
    q = (sq * (d ** -0.5)).astype(BF16).reshape(B, tn, H, d).transpose(0, 2, 1, 3)
    q_blk = (q[:, :, :, None, :] * jnp.eye(H, dtype=BF16)[None, :, None, :, None]).reshape(B, H * tn, H * d)
    new = jnp.pad(skv.reshape(B, tn, 2 * H * d), ((0, 0), (0, page - tn), (0, 0)))
    suffix = _suffix_matrix(page)
    grid_spec = pltpu.PrefetchScalarGridSpec(
        num_scalar_prefetch=1,
        grid=(B, n_pages + 1),
        in_specs=[
            pl.BlockSpec((1, H * tn, H * d), lambda b, s, pt: (b, 0, 0)),
            pl.BlockSpec((1, page, 2 * H * d), lambda b, s, pt: (b, 0, 0)),
            pl.BlockSpec((1, 1, page, 2 * H * d),
                         lambda b, s, pt: (pt[b, n_pages - jnp.maximum(s, 1)], layer, 0, 0)),
            pl.BlockSpec(suffix.shape, lambda b, s, pt: (0, 0)),
        ],
        out_specs=pl.BlockSpec((1, tn, H * d), lambda b, s, pt: (b, 0, 0)),
        scratch_shapes=[pltpu.VMEM((H * tn, page), F32), pltpu.VMEM((H * tn, H * d), F32)],
    )
    return pl.pallas_call(
        functools.partial(_sb_sample_kernel, n_new=tn),
        grid_spec=grid_spec,
        out_shape=jax.ShapeDtypeStruct((B, tn, H * d), BF16),
        compiler_params=pltpu.CompilerParams(dimension_semantics=("parallel", "arbitrary"),
                                             vmem_limit_bytes=VMEM_LIMIT_BYTES),
        name="sb_sample",
    )(page_table, q_blk, new, cache, suffix)


def _rmsnorm(x, g):
    xf = x.astype(jnp.float32)
    y = xf * lax.rsqrt(jnp.mean(xf * xf, axis=-1, keepdims=True) + NORM_EPS)
    return (y * g.astype(jnp.float32)).astype(x.dtype)


def _rel_bucket(dist):
    n = jnp.maximum(dist, 0)
    nf = jnp.maximum(n, 1).astype(jnp.float32)
    large = RPB_MAX_EXACT + (jnp.log(nf / RPB_MAX_EXACT) / math.log(RPB_MAX_DIST / RPB_MAX_EXACT)
                             * (N_BUCKETS - RPB_MAX_EXACT)).astype(jnp.int32)
    return jnp.where(n < RPB_MAX_EXACT, n, jnp.minimum(large, N_BUCKETS - 1))


def _gla_scan(q, k, v, log_a, s0):
    B, T = q.shape[:2]
    C = min(GLA_CHUNK, T)
    n_chunks = -(-T // C)
    pad = n_chunks * C - T

    def prep(a):
        a = jnp.pad(a, ((0, 0), (0, pad), (0, 0), (0, 0)))
        return a.reshape(B, n_chunks, C, a.shape[2], a.shape[3]).transpose(1, 0, 3, 2, 4)

    xs = (prep(q), prep(k), prep(v), prep(log_a.astype(jnp.float32)))
    tri = jnp.tril(jnp.ones((C, C), dtype=bool))

    def step(S, inp):
        qb, kb, vb, ab = inp
        qf, kf, vf = qb.astype(jnp.float32), kb.astype(jnp.float32), vb.astype(jnp.float32)
        b = jnp.cumsum(ab, axis=2)
        o_inter = jnp.einsum('bhcd,bhde->bhce', qf * jnp.exp(b), S)
        rel = jnp.where(tri[None, None, :, :, None], b[:, :, :, None, :] - b[:, :, None, :, :], -jnp.inf)
        att = jnp.einsum('bhtd,bhsd,bhtsd->bhts', qf, kf, jnp.exp(rel))
        o = o_inter + jnp.einsum('bhts,bhse->bhte', att, vf)
        b_last = b[:, :, -1:, :]
        S = jnp.exp(b_last[:, :, 0, :])[..., None] * S + jnp.einsum('bhsd,bhse->bhde', kf * jnp.exp(b_last - b), vf)
        return S, o

    S, o = lax.scan(step, s0.astype(jnp.float32), xs)
    o = o.transpose(1, 0, 3, 2, 4).reshape(B, n_chunks * C, GLA_HEADS, GLA_DV)[:, :T]
    return o.astype(q.dtype), S.astype(s0.dtype)


def _gla_mixer(q, k, v, a_low, r, s0, w_decay, b_decay, out_norm):
    B, T = q.shape[:2]
    q = q.reshape(B, T, GLA_HEADS, GLA_DK) * (GLA_DK ** -0.5)
    k = k.reshape(B, T, GLA_HEADS, GLA_DK)
    v = v.reshape(B, T, GLA_HEADS, GLA_DV)
    log_a = (jax.nn.log_sigmoid((a_low @ w_decay + b_decay).astype(jnp.float32)) / GLA_TAU).reshape(B, T, GLA_HEADS, GLA_DK)
    o, state = _gla_scan(q, k, v, log_a, s0)
    o = _rmsnorm(o, out_norm).reshape(B, T, GLA_VAL) * jax.nn.silu(r)
    return o, state


def _nsa_compress(rows, pos_emb, w1, w2):
    B, S = rows.shape[:2]
    n_sub = S // CMP_STRIDE
    sub = rows[:, :n_sub * CMP_STRIDE].reshape(B, n_sub, CMP_STRIDE, NSA_KV_HEADS, NSA_D)
    first = jnp.einsum('bnlgd,lde->bnge', sub + pos_emb[None, None, :CMP_STRIDE, None, :], w1[:CMP_STRIDE])
    second = jnp.einsum('bnlgd,lde->bnge', sub + pos_emb[None, None, CMP_STRIDE:, None, :], w1[CMP_STRIDE:])
    hid = jax.nn.silu(first[:, :-1] + second[:, 1:])
    return jnp.einsum('bnge,ed->bngd', hid, w2)


def _sel_blocks(rows):
    B, S = rows.shape[:2]
    ns = -(-S // SEL_BLOCK)
    rows = jnp.pad(rows, ((0, 0), (0, ns * SEL_BLOCK - S), (0, 0), (0, 0)))
    return rows.reshape(B, ns, SEL_BLOCK, NSA_KV_HEADS, NSA_D).transpose(0, 3, 1, 2, 4)


def _cmp_sel_map(nc, ns):
    c_start = jnp.arange(nc) * CMP_STRIDE
    s_start = jnp.arange(ns) * SEL_BLOCK
    hit = (c_start[:, None] < s_start[None, :] + SEL_BLOCK) & (c_start[:, None] + CMP_LEN > s_start[None, :])
    return hit.astype(jnp.float32)


def _nsa_block(q, gates, q_pos, kc, vc, c_end, ksb, vsb, wk, wv, w_pos, rpb):
    B, Q, G, R, d = q.shape
    scale = d ** -0.5
    table = rpb.astype(jnp.float32).reshape(N_BUCKETS, G, R)
    s = jnp.einsum('bqgrd,bngd->bqgrn', q, kc).astype(jnp.float32) * scale
    s = s + table[_rel_bucket(q_pos[:, None] - c_end[None, :])].transpose(0, 2, 3, 1)[None]
    m = (c_end[None, :] <= q_pos[:, None])[None, :, None, None, :]
    p_c = jax.nn.softmax(jnp.where(m, s, NEG), axis=-1) * m
    o_c = jnp.einsum('bqgrn,bngd->bqgrd', p_c.astype(vc.dtype), vc)
    ns = ksb.shape[2]
    imp = jnp.einsum('bqgn,nj->bqgj', p_c.sum(axis=3), _cmp_sel_map(kc.shape[1], ns))
    j = jnp.arange(ns)[None, :]
    cur = (q_pos // SEL_BLOCK)[:, None]
    valid = (j * SEL_BLOCK <= q_pos[:, None])[None, :, None, :]
    forced = ((j == 0) | (j == cur) | (j == cur - 1))[None, :, None, :]
    score = jnp.where(valid, jnp.where(forced, FORCE, imp), NEG)
    _, idx = lax.top_k(score, min(N_SEL, ns))
    idx = idx.transpose(0, 2, 1, 3)
    bi = jnp.arange(B)[:, None, None, None]
    gi = jnp.arange(G)[None, :, None, None]
    kg = ksb[bi, gi, idx]
    vg = vsb[bi, gi, idx]
    k_pos = idx[..., None] * SEL_BLOCK + jnp.arange(SEL_BLOCK)
    s = jnp.einsum('bqgrd,bgqnld->bgqrnl', q, kg).astype(jnp.float32) * scale
    bias = table.transpose(1, 0, 2)[gi[..., None], _rel_bucket(q_pos[None, None, :, None, None] - k_pos)]
    s = s + bias.transpose(0, 1, 2, 5, 3, 4)
    m = (k_pos <= q_pos[None, None, :, None, None])[:, :, :, None]
    shp = s.shape
    p = jax.nn.softmax(jnp.where(m, s, NEG).reshape(shp[0], shp[1], shp[2], shp[3], -1), axis=-1).reshape(shp)
    o_s = jnp.einsum('bgqrnl,bgqnld->bqgrd', p.astype(vg.dtype), vg)
    s = jnp.einsum('bqgrd,bsgd->bqgrs', q, wk).astype(jnp.float32) * scale
    dist = q_pos[:, None] - w_pos[None, :]
    s = s + table[_rel_bucket(dist)].transpose(0, 2, 3, 1)[None]
    m = ((dist >= 0) & (dist < WINDOW) & (w_pos[None, :] >= 0))[None, :, None, None, :]
    p = jax.nn.softmax(jnp.where(m, s, NEG), axis=-1)
    o_w = jnp.einsum('bqgrs,bsgd->bqgrd', p.astype(wv.dtype), wv)
    g = jax.nn.sigmoid(gates.astype(jnp.float32))
    o = g[..., 0:1] * o_c + g[..., 1:2] * o_s + g[..., 2:3] * o_w
    return o.reshape(B, Q, G * R * d).astype(q.dtype)


def _sb_block(q, q_pos, k, v, k_pos):
    z = jnp.einsum('bqhd,bshd->bhqs', q, k).astype(jnp.float32) * (SB_D ** -0.5)
    m = (k_pos[None, :] < q_pos[:, None])[None, None]
    log_fail = jnp.where(m, jax.nn.log_sigmoid(-z), 0.0)
    later = lax.cumsum(log_fail, axis=3, reverse=True) - log_fail
    a = jnp.where(m, jnp.exp(jax.nn.log_sigmoid(z) + later), 0.0)
    return jnp.einsum('bhqs,bshd->bqhd', a.astype(v.dtype), v)


def _split(z):
    cuts = [int(c) for c in np.cumsum(IN_SPLITS)[:-1]]
    return jnp.split(z[..., :N_IN], cuts, axis=-1)


def _mix_prompt(z, gla_w_decay, gla_b_decay, gla_out_norm, cmp_pos, cmp_w1, cmp_w2, rpb):
    B, T = z.shape[:2]
    gq, gk, gv, ga, gr, nq, nkv, ng, sq, skv, mg = _split(z)
    s0 = jnp.zeros((B, GLA_HEADS, GLA_DK, GLA_DV), jnp.float32)
    o_gla, gla_state = gla_mixer(gq, gk, gv, ga, gr, s0, gla_w_decay, gla_b_decay, gla_out_norm)
    nkv = nkv.reshape(B, T, 6, NSA_KV_HEADS, NSA_D)
    o_nsa = nsa_prompt(nq, ng, nkv, cmp_pos, cmp_w1, cmp_w2, rpb)
    skv = skv.reshape(B, T, 2, SB_HEADS, SB_D)
    o_sb = sb_prompt(sq, skv)
    keep = min(WINDOW, T)
    return (o_gla, o_nsa, o_sb, mg), nkv[:, :, :4], skv, nkv[:, T - keep:, 4:], gla_state


def _mix_sample(z, past_nsa, cache_sb, page_table, layer, win_buf, s0, past_len, gla_w_decay, gla_b_decay,
                gla_out_norm, cmp_pos, cmp_w1, cmp_w2, rpb):
    B, T = z.shape[:2]
    gq, gk, gv, ga, gr, nq, nkv, ng, sq, skv, mg = _split(z)
    o_gla, gla_state = gla_mixer(gq, gk, gv, ga, gr, s0, gla_w_decay, gla_b_decay, gla_out_norm)
    nkv = nkv.reshape(B, T, 6, NSA_KV_HEADS, NSA_D)
    rows = jnp.concatenate([past_nsa, nkv[:, :, :4]], axis=1)
    kc = _nsa_compress(rows[:, :, 0], cmp_pos[0], cmp_w1[0], cmp_w2[0])
    vc = _nsa_compress(rows[:, :, 1], cmp_pos[1], cmp_w1[1], cmp_w2[1])
    c_end = jnp.arange(kc.shape[1]) * CMP_STRIDE + (CMP_LEN - 1)
    ksb, vsb = _sel_blocks(rows[:, :, 2]), _sel_blocks(rows[:, :, 3])
    win = jnp.concatenate([win_buf, nkv[:, :, 4:]], axis=1)
    wb = win_buf.shape[1]
    q_pos = past_len + jnp.arange(T)
    w_pos = (past_len - wb) + jnp.arange(wb + T)
    o_nsa = _nsa_block(nq.reshape(B, T, NSA_KV_HEADS, NSA_GROUP, NSA_D), ng.reshape(B, T, NSA_KV_HEADS, NSA_GROUP, 3),
                       q_pos, kc, vc, c_end, ksb, vsb, win[:, :, 0], win[:, :, 1], w_pos, rpb)
    skv = skv.reshape(B, T, 2, SB_HEADS, SB_D)
    o_sb = sb_sample(sq, skv, cache_sb, page_table, layer)
    keep = min(WINDOW, wb + T)
    return (o_gla, o_nsa, o_sb, mg), nkv[:, :, :4], skv, win[:, wb + T - keep:], gla_state


def kernel(x_prompt, x_sample, cache_nsa, cache_sb, cache_win, state_gla, page_table, ffn1_norm, ffn1_w_gate, ffn1_w_up, ffn1_w_down, mix_norm, w_in, gla_w_decay, gla_b_decay, gla_out_norm, nsa_cmp_pos, nsa_cmp_w1, nsa_cmp_w2, rpb_table, w_branch_gla, w_branch_nsa, w_branch_sb, w_out, ffn2_norm, ffn2_w_gate, ffn2_w_up, ffn2_w_down, final_norm):
    bp, tp, _ = x_prompt.shape
    bs, ts, _ = x_sample.shape
    mp, ms = bp * tp, bs * ts
    dec_batch, n_pages = page_table.shape
    past_len = n_pages * cache_nsa.shape[2]
    x = jnp.concatenate([x_prompt.reshape(mp, D_MODEL), x_sample.reshape(ms, D_MODEL)], axis=0)
    outs = [[] for _ in range(8)]
    for l in range(DEPTH):
        x = ffn_half_step(x, ffn1_norm[l], ffn1_w_gate[l].astype(BF16), ffn1_w_up[l].astype(BF16),
                          ffn1_w_down[l].astype(BF16))
        w_in_l = jnp.pad(w_in[l].astype(BF16), ((0, 0), (0, N_IN_PAD - N_IN)))
        z = in_projection(x, mix_norm[l], w_in_l)
        mix_w = (gla_w_decay[l], gla_b_decay[l], gla_out_norm[l], nsa_cmp_pos[l], nsa_cmp_w1[l], nsa_cmp_w2[l],
                 rpb_table)
        br_p, r_nsa, r_sb, r_win, r_gla = _mix_prompt(z[:mp].reshape(bp, tp, N_IN_PAD), *mix_w)
        for lst, val in zip(outs[0::2], (r_nsa, r_sb, r_win, r_gla)):
            lst.append(val)
        past_nsa = cache_nsa[page_table, l].reshape(dec_batch, past_len, 4, NSA_KV_HEADS, NSA_D)
        br_s, r_nsa, r_sb, r_win, r_gla = _mix_sample(z[mp:].reshape(bs, ts, N_IN_PAD), past_nsa, cache_sb,
                                                      page_table, l, cache_win[:, l], state_gla[:, l], past_len,
                                                      *mix_w)
        for lst, val in zip(outs[1::2], (r_nsa, r_sb, r_win, r_gla)):
            lst.append(val)
        o_gla, o_nsa, o_sb, gates = (
            jnp.concatenate([a.reshape(mp, a.shape[-1]), b.reshape(ms, b.shape[-1])], axis=0)
            for a, b in zip(br_p, br_s))
        y = merge_branches(o_gla.astype(BF16), o_nsa.astype(BF16), o_sb.astype(BF16), gates,
                           w_branch_gla[l].astype(BF16), w_branch_nsa[l].astype(BF16), w_branch_sb[l].astype(BF16))
        x = out_projection(x, y, w_out[l].astype(BF16))
        x = ffn_half_step(x, ffn2_norm[l], ffn2_w_gate[l].astype(BF16), ffn2_w_up[l].astype(BF16),
                          ffn2_w_down[l].astype(BF16))
    y = final_rmsnorm(x, final_norm)
    return (y[:mp].reshape(bp, tp, D_MODEL), y[mp:].reshape(bs, ts, D_MODEL)) + tuple(
        jnp.stack(o, axis=1) for o in outs)
```

```python
import functools
import math

import jax
import jax.numpy as jnp
import numpy as np
from jax import lax
from jax.experimental import pallas as pl
from jax.experimental.pallas import tpu as pltpu

D_MODEL = 2048
DEPTH = 2
D_FF = 5632
NORM_EPS = 1e-6
GLA_HEADS = 4
GLA_DK = 128
GLA_DV = 256
GLA_RANK = 16
GLA_TAU = 16.0
GLA_CHUNK = 64
GLA_KEY = GLA_HEADS * GLA_DK
GLA_VAL = GLA_HEADS * GLA_DV
NSA_HEADS = 16
NSA_KV_HEADS = 4
NSA_GROUP = NSA_HEADS // NSA_KV_HEADS
NSA_D = 64
NSA_OUT = NSA_HEADS * NSA_D
NSA_KVW = NSA_KV_HEADS * NSA_D
CMP_STRIDE = 16
CMP_LEN = 2 * CMP_STRIDE
CMP_HID = 64
SEL_BLOCK = 64
N_SEL = 16
WINDOW = 512
NSA_QBLK = 32
SB_HEADS = 16
SB_D = 64
SB_OUT = SB_HEADS * SB_D
SB_QBLK = 128
N_BUCKETS = 32
RPB_MAX_EXACT = 16
RPB_MAX_DIST = 128
NEG = -1e30
FORCE = 1e9
IN_SPLITS = (GLA_KEY, GLA_KEY, GLA_VAL, GLA_RANK, GLA_VAL, NSA_OUT, 6 * NSA_KVW, 3 * NSA_HEADS, SB_OUT, 2 * SB_OUT,
             3 * D_MODEL)
N_IN = sum(IN_SPLITS)

BF16 = jnp.bfloat16
F32 = jnp.float32

VMEM_LIMIT_BYTES = 56 * 1024 * 1024
LANES = 128
ROW_TILE = 640
FF_TILE = 512
PROJ_TILE = 1152
N_IN_PAD = 14976
OUT_TILE = 512


def _rms_bf16(x, g):
    y = x * lax.rsqrt(jnp.mean(x * x, axis=-1, keepdims=True) + NORM_EPS)
    return (y * g).astype(BF16)


def _ffn_kernel(x_ref, g_ref, wg_ref, wu_ref, wd_ref, o_ref, h_ref, acc_ref):
    f = pl.program_id(1)

    @pl.when(f == 0)
    def _():
        h_ref[...] = _rms_bf16(x_ref[...], g_ref[...])
        acc_ref[...] = jnp.zeros_like(acc_ref)

    h = h_ref[...]
    a = jnp.dot(h, wg_ref[...], preferred_element_type=F32)
    b = jnp.dot(h, wu_ref[...], preferred_element_type=F32)
    act = (a * jax.nn.sigmoid(a) * b).astype(BF16)
    acc_ref[...] += jnp.dot(act, wd_ref[...], preferred_element_type=F32)

    @pl.when(f == pl.num_programs(1) - 1)
    def _():
        o_ref[...] = x_ref[...] + 0.5 * acc_ref[...]


def ffn_half_step(x, g, wg, wu, wd):
    m = x.shape[0]
    return pl.pallas_call(
        _ffn_kernel,
        grid=(m // ROW_TILE, D_FF // FF_TILE),
        in_specs=[
            pl.BlockSpec((ROW_TILE, D_MODEL), lambda i, f: (i, 0)),
            pl.BlockSpec((1, D_MODEL), lambda i, f: (0, 0)),
            pl.BlockSpec((D_MODEL, FF_TILE), lambda i, f: (0, f)),
            pl.BlockSpec((D_MODEL, FF_TILE), lambda i, f: (0, f)),
            pl.BlockSpec((FF_TILE, D_MODEL), lambda i, f: (f, 0)),
        ],
        out_specs=pl.BlockSpec((ROW_TILE, D_MODEL), lambda i, f: (i, 0)),
        out_shape=jax.ShapeDtypeStruct((m, D_MODEL), F32),
        scratch_shapes=[pltpu.VMEM((ROW_TILE, D_MODEL), BF16), pltpu.VMEM((ROW_TILE, D_MODEL), F32)],
        compiler_params=pltpu.CompilerParams(dimension_semantics=("parallel", "arbitrary"),
                                             vmem_limit_bytes=VMEM_LIMIT_BYTES),
        name="ffn_half_step",
    )(x, g.reshape(1, D_MODEL), wg, wu, wd)


def _proj_kernel(x_ref, g_ref, w_ref, o_ref, h_ref):
    @pl.when(pl.program_id(1) == 0)
    def _():
        h_ref[...] = _rms_bf16(x_ref[...], g_ref[...])

    o_ref[...] = jnp.dot(h_ref[...], w_ref[...], preferred_element_type=F32)


def in_projection(x, g, w):
    m = x.shape[0]
    return pl.pallas_call(
        _proj_kernel,
        grid=(m // ROW_TILE, N_IN_PAD // PROJ_TILE),
        in_specs=[
            pl.BlockSpec((ROW_TILE, D_MODEL), lambda i, n: (i, 0)),
            pl.BlockSpec((1, D_MODEL), lambda i, n: (0, 0)),
            pl.BlockSpec((D_MODEL, PROJ_TILE), lambda i, n: (0, n)),
        ],
        out_specs=pl.BlockSpec((ROW_TILE, PROJ_TILE), lambda i, n: (i, n)),
        out_shape=jax.ShapeDtypeStruct((m, N_IN_PAD), F32),
        scratch_shapes=[pltpu.VMEM((ROW_TILE, D_MODEL), BF16)],
        compiler_params=pltpu.CompilerParams(dimension_semantics=("parallel", "arbitrary"),
                                             vmem_limit_bytes=VMEM_LIMIT_BYTES),
        name="in_projection",
    )(x, g.reshape(1, D_MODEL), w)


def _merge_kernel(oa_ref, ob_ref, oc_ref, ga_ref, gb_ref, gc_ref, wa_ref, wb_ref, wc_ref, y_ref):
    ya = jnp.dot(oa_ref[...], wa_ref[...], preferred_element_type=F32)
    yb = jnp.dot(ob_ref[...], wb_ref[...], preferred_element_type=F32)
    yc = jnp.dot(oc_ref[...], wc_ref[...], preferred_element_type=F32)
    y = jax.nn.sigmoid(ga_ref[...]) * ya + jax.nn.sigmoid(gb_ref[...]) * yb + jax.nn.sigmoid(gc_ref[...]) * yc
    y_ref[...] = y.astype(BF16)


def merge_branches(o_gla, o_nsa, o_sb, gates, wa, wb, wc):
    m = o_gla.shape[0]
    nb = D_MODEL // OUT_TILE
    o_spec = lambda width: pl.BlockSpec((ROW_TILE, width), lambda i, n: (i, 0))
    g_spec = lambda k: pl.BlockSpec((ROW_TILE, OUT_TILE), lambda i, n, k=k: (i, n + k * nb))
    w_spec = lambda width: pl.BlockSpec((width, OUT_TILE), lambda i, n: (0, n))
    return pl.pallas_call(
        _merge_kernel,
        grid=(m // ROW_TILE, nb),
        in_specs=[o_spec(GLA_VAL), o_spec(NSA_OUT), o_spec(SB_OUT), g_spec(0), g_spec(1), g_spec(2),
                  w_spec(GLA_VAL), w_spec(NSA_OUT), w_spec(SB_OUT)],
        out_specs=pl.BlockSpec((ROW_TILE, OUT_TILE), lambda i, n: (i, n)),
        out_shape=jax.ShapeDtypeStruct((m, D_MODEL), BF16),
        compiler_params=pltpu.CompilerParams(dimension_semantics=("parallel", "arbitrary"),
                                             vmem_limit_bytes=VMEM_LIMIT_BYTES),
        name="merge_branches",
    )(o_gla, o_nsa, o_sb, gates, gates, gates, wa, wb, wc)


def _out_kernel(x_ref, y_ref, w_ref, o_ref):
    o_ref[...] = x_ref[...] + jnp.dot(y_ref[...], w_ref[...], preferred_element_type=F32)


def out_projection(x, y, w):
    m = x.shape[0]
    return pl.pallas_call(
        _out_kernel,
        grid=(m // ROW_TILE, D_MODEL // OUT_TILE),
        in_specs=[
            pl.BlockSpec((ROW_TILE, OUT_TILE), lambda i, n: (i, n)),
            pl.BlockSpec((ROW_TILE, D_MODEL), lambda i, n: (i, 0)),
            pl.BlockSpec((D_MODEL, OUT_TILE), lambda i, n: (0, n)),
        ],
        out_specs=pl.BlockSpec((ROW_TILE, OUT_TILE), lambda i, n: (i, n)),
        out_shape=jax.ShapeDtypeStruct((m, D_MODEL), F32),
        compiler_params=pltpu.CompilerParams(dimension_semantics=("parallel", "arbitrary"),
                                             vmem_limit_bytes=VMEM_LIMIT_BYTES),
        name="out_projection",
    )(x, y, w)


def _norm_kernel(x_ref, g_ref, o_ref):
    x = x_ref[...]
    o_ref[...] = x * lax.rsqrt(jnp.mean(x * x, axis=-1, keepdims=True) + NORM_EPS) * g_ref[...]


def final_rmsnorm(x, g):
    m = x.shape[0]
    return pl.pallas_call(
        _norm_kernel,
        grid=(m // ROW_TILE,),
        in_specs=[pl.BlockSpec((ROW_TILE, D_MODEL), lambda i: (i, 0)), pl.BlockSpec((1, D_MODEL), lambda i: (0, 0))],
        out_specs=pl.BlockSpec((ROW_TILE, D_MODEL), lambda i: (i, 0)),
        out_shape=jax.ShapeDtypeStruct((m, D_MODEL), F32),
        compiler_params=pltpu.CompilerParams(dimension_semantics=("parallel",), vmem_limit_bytes=VMEM_LIMIT_BYTES),
        name="final_rmsnorm",
    )(x, g.reshape(1, D_MODEL))


NSA_TQ = 128
NSA_TK = 256
SEL_TOEP = 4
WIN_TOEP = 6


def _toeplitz_bias(rpb, window):
    n = WIN_TOEP if window else SEL_TOEP
    u = jnp.arange(NSA_TQ)[:, None]
    w = jnp.arange(NSA_TK)[None, :]
    d = (jnp.arange(n) * NSA_TQ)[:, None, None] + (u - w)[None]
    b = rpb.astype(F32)[_rel_bucket(d)]
    ok = (d >= 0) & (d < WINDOW) if window else (d >= 0)
    b = jnp.where(ok[..., None], b, NEG)
    return b.transpose(3, 0, 1, 2).reshape(NSA_KV_HEADS, NSA_GROUP, n, NSA_TQ, NSA_TK)


def _split_hi_lo(x):
    hi = x.astype(BF16)
    lo = (x - hi.astype(F32)).astype(BF16)
    return hi, lo


def _nsa_prompt_kernel(q_ref, gate_ref, cmp_ref, kv_ref, posa_ref, posb_ref, w1a_ref, w1b_ref, w2_ref,
                       biasc_ref, stoep_ref, wtoep_ref, mapt_ref, expand_ref, o_ref,
                       kc_ref, vc_ref, m_ref, l_ref, acc_ref, *, n_cmp):
    qi = pl.program_id(2)
    tq, tk, R = NSA_TQ, NSA_TK, NSA_GROUP
    rows = R * tq

    @pl.when(qi == 0)
    def _():
        for t, dst in ((0, kc_ref), (1, vc_ref)):
            x = cmp_ref[0, 0, t]
            first = jnp.dot((x + posa_ref[t]).astype(BF16), w1a_ref[t], preferred_element_type=F32)
            second = jnp.dot((x + posb_ref[t]).astype(BF16), w1b_ref[t], preferred_element_type=F32)
            pre = first + pltpu.roll(second, second.shape[0] - 1, 0)
            hid = (pre * jax.nn.sigmoid(pre)).astype(BF16)
            dst[...] = jnp.dot(hid, w2_ref[t], preferred_element_type=F32).astype(BF16)

    q4 = q_ref[0, 0].reshape(rows, NSA_D)
    q_pos = qi * tq + lax.broadcasted_iota(jnp.int32, (tq, 1), 0)

    ncp = kc_ref.shape[0]
    s = lax.dot_general(q4, kc_ref[...], (((1,), (1,)), ((), ())), preferred_element_type=F32)
    s = s.reshape(R, tq, ncp) + biasc_ref[0]
    n_idx = lax.broadcasted_iota(jnp.int32, (tq, ncp), 1)
    ok = ((n_idx * CMP_STRIDE + (CMP_LEN - 1)) <= q_pos) & (n_idx < n_cmp)
    s = jnp.where(ok[None], s, NEG)
    e = jnp.exp(s - jnp.max(s, axis=-1, keepdims=True))
    p_c = jnp.where(ok[None], e / jnp.sum(e, axis=-1, keepdims=True), 0.0)
    o_c = jnp.dot(p_c.reshape(rows, ncp).astype(BF16), vc_ref[...], preferred_element_type=F32)
    psum = jnp.sum(p_c, axis=0)
    p_hi, p_lo = _split_hi_lo(psum)
    nt = (((1,), (1,)), ((), ()))
    imp = (lax.dot_general(mapt_ref[...], p_hi, nt, preferred_element_type=F32)
           + lax.dot_general(mapt_ref[...], p_lo, nt, preferred_element_type=F32))

    ns = imp.shape[0]
    j_idx = lax.broadcasted_iota(jnp.int32, (ns, tq), 0)
    t_pos = qi * tq + lax.broadcasted_iota(jnp.int32, (ns, tq), 1)
    cur = t_pos // SEL_BLOCK
    valid = j_idx * SEL_BLOCK <= t_pos
    forced = (j_idx == 0) | (j_idx == cur) | (j_idx == cur - 1)
    score = jnp.where(valid, jnp.where(forced, FORCE, imp), NEG)
    rank = jnp.zeros((ns, tq), F32)
    for i in range(ns):
        row = score[i:i + 1, :]
        beats = (row > score) | ((row == score) & (j_idx > i))
        rank = rank + jnp.where(beats, 1.0, 0.0)
    sel_t = jnp.where((rank < float(min(N_SEL, ns))) & valid, 1.0, 0.0)
    sel = sel_t.T.astype(BF16)

    def attend(k_idx, v_idx, toep_ref, n_toep, j_lo, use_sel):
        m_ref[...] = jnp.full(m_ref.shape, NEG, F32)
        l_ref[...] = jnp.zeros(l_ref.shape, F32)
        acc_ref[...] = jnp.zeros(acc_ref.shape, F32)
        j_hi = qi // 2

        def body(step, carry):
            j = j_hi - step
            k = kv_ref[0, 0, k_idx, pl.ds(pl.multiple_of(j * tk, tk), tk), :]
            v = kv_ref[0, 0, v_idx, pl.ds(pl.multiple_of(j * tk, tk), tk), :]
            sc = lax.dot_general(q4, k, (((1,), (1,)), ((), ())), preferred_element_type=F32)
            off = jnp.minimum(qi - 2 * j, n_toep - 1)
            sc = sc.reshape(R, tq, tk) + toep_ref[0, :, off]
            if use_sel:
                hit = jnp.dot(sel, expand_ref[j], preferred_element_type=F32)
                sc = jnp.where((hit > 0.5)[None], sc, NEG)
            sc = sc.reshape(rows, tk)
            m_old = m_ref[...]
            m_new = jnp.maximum(m_old, jnp.max(sc, axis=-1, keepdims=True))
            alpha = jnp.exp(m_old - m_new)
            p = jnp.exp(sc - m_new)
            l_ref[...] = alpha * l_ref[...] + jnp.sum(p, axis=-1, keepdims=True)
            acc_ref[...] = alpha * acc_ref[...] + jnp.dot(p.astype(BF16), v, preferred_element_type=F32)
            m_ref[...] = m_new
            return carry

        lax.fori_loop(0, j_hi - j_lo + 1, body, 0)
        return acc_ref[...] / l_ref[...]

    o_s = attend(0, 1, stoep_ref, SEL_TOEP, 0, True)
    o_w = attend(2, 3, wtoep_ref, WIN_TOEP, jnp.maximum(qi * tq - (WINDOW - 1), 0) // tk, False)

    g = jax.nn.sigmoid(gate_ref[0, 0])
    for r in range(R):
        sl = slice(r * tq, (r + 1) * tq)
        o = (g[:, 3 * r:3 * r + 1] * o_c[sl] + g[:, 3 * r + 1:3 * r + 2] * o_s[sl]
             + g[:, 3 * r + 2:3 * r + 3] * o_w[sl])
        o_ref[0, 0, r] = o.astype(o_ref.dtype)


def nsa_prompt(nq, ng, nkv, cmp_pos, cmp_w1, cmp_w2, rpb):
    B, T = nq.shape[:2]
    G, R, d = NSA_KV_HEADS, NSA_GROUP, NSA_D
    n_sub = T // CMP_STRIDE
    n_cmp = n_sub - 1
    ns = T // SEL_BLOCK
    nq_t = T // NSA_TQ
    nk_t = T // NSA_TK
    q = (nq * (d ** -0.5)).astype(BF16).reshape(B, T, G, R, d).transpose(0, 2, 3, 1, 4)
    gates = ng.reshape(B, T, G, 3 * R).transpose(0, 2, 1, 3)
    cmp_rows = nkv[:, :, 0:2].transpose(0, 3, 2, 1, 4).reshape(B, G, 2, n_sub, CMP_STRIDE * d)
    kv = nkv[:, :, 2:6].astype(BF16).transpose(0, 3, 2, 1, 4)
    pos_a = cmp_pos[:, :CMP_STRIDE].reshape(2, 1, CMP_STRIDE * d)
    pos_b = cmp_pos[:, CMP_STRIDE:].reshape(2, 1, CMP_STRIDE * d)
    w1a = cmp_w1[:, :CMP_STRIDE].reshape(2, CMP_STRIDE * d, CMP_HID).astype(BF16)
    w1b = cmp_w1[:, CMP_STRIDE:].reshape(2, CMP_STRIDE * d, CMP_HID).astype(BF16)
    w2 = cmp_w2.astype(BF16)
    t_pos = jnp.arange(T)[:, None]
    c_end = jnp.arange(n_sub)[None, :] * CMP_STRIDE + (CMP_LEN - 1)
    bias_c = rpb.astype(F32)[_rel_bucket(t_pos - c_end)].transpose(2, 0, 1).reshape(G, R, T, n_sub)
    stoep = _toeplitz_bias(rpb, False)
    wtoep = _toeplitz_bias(rpb, True)
    map_t = _cmp_sel_map(n_sub, ns).T.astype(BF16)
    expand = (jnp.arange(ns)[None, :, None]
              == ((jnp.arange(nk_t)[:, None, None] * NSA_TK + jnp.arange(NSA_TK)[None, None, :]) // SEL_BLOCK)
              ).astype(BF16)
    full = lambda a: pl.BlockSpec(a.shape, lambda b, g, i: (0,) * a.ndim)
    per_g = lambda a: pl.BlockSpec((1,) + a.shape[1:], lambda b, g, i: (g,) + (0,) * (a.ndim - 1))
    o = pl.pallas_call(
        functools.partial(_nsa_prompt_kernel, n_cmp=n_cmp),
        grid=(B, G, nq_t),
        in_specs=[
            pl.BlockSpec((1, 1, R, NSA_TQ, d), lambda b, g, i: (b, g, 0, i, 0)),
            pl.BlockSpec((1, 1, NSA_TQ, 3 * R), lambda b, g, i: (b, g, i, 0)),
            pl.BlockSpec((1, 1, 2, n_sub, CMP_STRIDE * d), lambda b, g, i: (b, g, 0, 0, 0)),
            pl.BlockSpec((1, 1, 4, T, d), lambda b, g, i: (b, g, 0, 0, 0)),
            full(pos_a), full(pos_b), full(w1a), full(w1b), full(w2),
            pl.BlockSpec((1, R, NSA_TQ, n_sub), lambda b, g, i: (g, 0, i, 0)),
            per_g(stoep), per_g(wtoep), full(map_t), full(expand),
        ],
        out_specs=pl.BlockSpec((1, 1, R, NSA_TQ, d), lambda b, g, i: (b, g, 0, i, 0)),
        out_shape=jax.ShapeDtypeStruct((B, G, R, T, d), BF16),
        scratch_shapes=[pltpu.VMEM((n_sub, d), BF16), pltpu.VMEM((n_sub, d), BF16),
                        pltpu.VMEM((R * NSA_TQ, 1), F32), pltpu.VMEM((R * NSA_TQ, 1), F32),
                        pltpu.VMEM((R * NSA_TQ, d), F32)],
        compiler_params=pltpu.CompilerParams(dimension_semantics=("parallel", "parallel", "arbitrary"),
                                             vmem_limit_bytes=VMEM_LIMIT_BYTES),
        name="nsa_prompt",
    )(q, gates, cmp_rows, kv, pos_a, pos_b, w1a, w1b, w2, bias_c, stoep, wtoep, map_t, expand)
    return o.transpose(0, 3, 1, 2, 4).reshape(B, T, G * R * d)


SB_TQ = 256
SB_TK = 256
SB_HEADS_PER_STEP = 4


def _softplus(z):
    return jnp.maximum(z, 0.0) + jnp.log(1.0 + jnp.exp(-jnp.abs(z)))


def _suffix_matrix(n):
    j = jnp.arange(n)[:, None]
    s = jnp.arange(n)[None, :]
    return jnp.concatenate([(j > s), jnp.ones((n, n), bool)], axis=1).astype(BF16)


def _sb_tile(z, ok, v, suffix, c_ref, acc_ref):
    tk = z.shape[1]
    sp = _softplus(z)
    lf = -sp if ok is None else jnp.where(ok, -sp, 0.0)
    hi, lo = _split_hi_lo(lf)
    tot = (jnp.dot(hi, suffix, preferred_element_type=F32) + jnp.dot(lo, suffix, preferred_element_type=F32))
    later = tot[:, :tk] + c_ref[...]
    a = jnp.exp(z - sp + later)
    if ok is not None:
        a = jnp.where(ok, a, 0.0)
    acc_ref[...] += jnp.dot(a.astype(BF16), v, preferred_element_type=F32)
    c_ref[...] += tot[:, tk:]


def _sb_prompt_kernel(q_ref, k_ref, v_ref, suffix_ref, o_ref, c_ref, acc_ref):
    qi = pl.program_id(2)
    tq, tk = SB_TQ, SB_TK
    per = tq // tk
    heads = q_ref.shape[1]
    suffix = suffix_ref[...]
    c_ref[...] = jnp.zeros(c_ref.shape, F32)
    acc_ref[...] = jnp.zeros(acc_ref.shape, F32)
    nt = (((1,), (1,)), ((), ()))
    q_pos = qi * tq + lax.broadcasted_iota(jnp.int32, (tq, tk), 0)
    k_off = lax.broadcasted_iota(jnp.int32, (tq, tk), 1)

    def tile(j, masked):
        ok = (j * tk + k_off < q_pos) if masked else None
        for h in range(heads):
            k = k_ref[0, h, pl.ds(pl.multiple_of(j * tk, tk), tk), :]
            v = v_ref[0, h, pl.ds(pl.multiple_of(j * tk, tk), tk), :]
            z = lax.dot_general(q_ref[0, h], k, nt, preferred_element_type=F32)
            _sb_tile(z, ok, v, suffix, c_ref.at[h], acc_ref.at[h])

    for t in range(per):
        tile(qi * per + (per - 1 - t), True)

    def body(step, carry):
        tile(qi * per - 1 - step, False)
        return carry

    lax.fori_loop(0, qi * per, body, 0)
    o_ref[0] = acc_ref[...].astype(o_ref.dtype)


def sb_prompt(sq, skv):
    B, T = sq.shape[:2]
    H, d, hp = SB_HEADS, SB_D, SB_HEADS_PER_STEP
    q = (sq * (d ** -0.5)).astype(BF16).reshape(B, T, H, d).transpose(0, 2, 1, 3)
    k = skv[:, :, 0].astype(BF16).transpose(0, 2, 1, 3)
    v = skv[:, :, 1].astype(BF16).transpose(0, 2, 1, 3)
    suffix = _suffix_matrix(SB_TK)
    kv_spec = pl.BlockSpec((1, hp, T, d), lambda b, h, i: (b, h, 0, 0))
    q_spec = pl.BlockSpec((1, hp, SB_TQ, d), lambda b, h, i: (b, h, i, 0))
    o = pl.pallas_call(
        _sb_prompt_kernel,
        grid=(B, H // hp, T // SB_TQ),
        in_specs=[q_spec, kv_spec, kv_spec, pl.BlockSpec(suffix.shape, lambda b, h, i: (0, 0))],
        out_specs=q_spec,
        out_shape=jax.ShapeDtypeStruct((B, H, T, d), BF16),
        scratch_shapes=[pltpu.VMEM((hp, SB_TQ, SB_TK), F32), pltpu.VMEM((hp, SB_TQ, d), F32)],
        compiler_params=pltpu.CompilerParams(dimension_semantics=("parallel", "parallel", "arbitrary"),
                                             vmem_limit_bytes=VMEM_LIMIT_BYTES),
        name="sb_prompt",
    )(q, k, v, suffix)
    return o.transpose(0, 2, 1, 3).reshape(B, T, H * d)


GLA_SUB = 16
GLA_EXP_CAP = 60.0


def _gla_kernel(q_ref, k_ref, v_ref, a_ref, r_ref, wd_ref, bd_ref, gn_ref, s0_ref, o_ref, st_ref, s_ref, *, sub):
    c = pl.program_id(2)
    C = q_ref.shape[1]

    @pl.when(c == 0)
    def _():
        s_ref[...] = s0_ref[0, 0]

    q = q_ref[0] * (GLA_DK ** -0.5)
    k = k_ref[0]
    v = v_ref[0].astype(BF16)
    pre = jnp.dot(a_ref[0].astype(BF16), wd_ref[...], preferred_element_type=F32) + bd_ref[...]
    log_a = -_softplus(-pre) * (1.0 / GLA_TAU)
    row = lax.broadcasted_iota(jnp.int32, (C, C), 0)
    col = lax.broadcasted_iota(jnp.int32, (C, C), 1)
    lower = jnp.where(col <= row, 1.0, 0.0).astype(BF16)
    a_hi, a_lo = _split_hi_lo(log_a)
    b = jnp.dot(lower, a_hi, preferred_element_type=F32) + jnp.dot(lower, a_lo, preferred_element_type=F32)

    s_old = s_ref[...]
    o = jnp.dot((q * jnp.exp(b)).astype(BF16), s_old.astype(BF16), preferred_element_type=F32)

    nt = (((1,), (1,)), ((), ()))
    parts = []
    for i in range(C // sub):
        lo, hi = i * sub, (i + 1) * sub
        ref_row = b[lo:lo + 1, :]
        qs = (q[lo:hi] * jnp.exp(b[lo:hi] - ref_row)).astype(BF16)
        ks = (k[:hi] * jnp.exp(jnp.minimum(ref_row - b[:hi], GLA_EXP_CAP))).astype(BF16)
        att = lax.dot_general(qs, ks, nt, preferred_element_type=F32)
        t_idx = lo + lax.broadcasted_iota(jnp.int32, (sub, hi), 0)
        s_idx = lax.broadcasted_iota(jnp.int32, (sub, hi), 1)
        att = jnp.where(s_idx <= t_idx, att, 0.0)
        parts.append(jnp.dot(att.astype(BF16), v[:hi], preferred_element_type=F32))
    o = o + (parts[0] if len(parts) == 1 else jnp.concatenate(parts, axis=0))

    b_t = b.T
    last = b_t[:, C - 1:C]
    kd_t = (k.T * jnp.exp(last - b_t)).astype(BF16)
    s_new = jnp.exp(last) * s_old + jnp.dot(kd_t, v, preferred_element_type=F32)
    s_ref[...] = s_new

    y = o * lax.rsqrt(jnp.mean(o * o, axis=-1, keepdims=True) + NORM_EPS) * gn_ref[...]
    r = r_ref[0]
    o_ref[0] = (y * (r * jax.nn.sigmoid(r))).astype(o_ref.dtype)

    @pl.when(c == pl.num_programs(2) - 1)
    def _():
        st_ref[0, 0] = s_new


def gla_mixer(gq, gk, gv, ga, gr, s0, w_decay, b_decay, out_norm):
    B, T = gq.shape[:2]
    H, dk, dv = GLA_HEADS, GLA_DK, GLA_DV
    C = min(GLA_CHUNK, T)
    sub = min(GLA_SUB, C)
    tok = lambda w: pl.BlockSpec((1, C, w), lambda b, h, c: (b, c, h))
    o, st = pl.pallas_call(
        functools.partial(_gla_kernel, sub=sub),
        grid=(B, H, T // C),
        in_specs=[
            tok(dk), tok(dk), tok(dv),
            pl.BlockSpec((1, C, GLA_RANK), lambda b, h, c: (b, c, 0)),
            tok(dv),
            pl.BlockSpec((GLA_RANK, dk), lambda b, h, c: (0, h)),
            pl.BlockSpec((1, dk), lambda b, h, c: (0, h)),
            pl.BlockSpec((1, dv), lambda b, h, c: (0, 0)),
            pl.BlockSpec((1, 1, dk, dv), lambda b, h, c: (b, h, 0, 0)),
        ],
        out_specs=[tok(dv), pl.BlockSpec((1, 1, dk, dv), lambda b, h, c: (b, h, 0, 0))],
        out_shape=[jax.ShapeDtypeStruct((B, T, H * dv), BF16), jax.ShapeDtypeStruct((B, H, dk, dv), F32)],
        scratch_shapes=[pltpu.VMEM((dk, dv), F32)],
        compiler_params=pltpu.CompilerParams(dimension_semantics=("parallel", "parallel", "arbitrary"),
                                             vmem_limit_bytes=VMEM_LIMIT_BYTES),
        name="gla_mixer",
    )(gq, gk, gv, ga, gr, w_decay.astype(BF16), b_decay.reshape(1, H * dk), out_norm.reshape(1, dv), s0)
    return o, st


def _sb_sample_kernel(pt_ref, q_ref, new_ref, page_ref, suffix_ref, o_ref, c_ref, acc_ref, *, n_new):
    s = pl.program_id(1)
    H, d = SB_HEADS, SB_D
    rows = q_ref.shape[1]
    nt = (((1,), (1,)), ((), ()))

    def tile(blk, ok):
        k = blk[:, :H * d].astype(BF16)
        v = blk[:, H * d:].astype(BF16)
        z = lax.dot_general(q_ref[0], k, nt, preferred_element_type=F32)
        _sb_tile(z, ok, v, suffix_ref[...], c_ref, acc_ref)

    @pl.when(s == 0)
    def _():
        c_ref[...] = jnp.zeros(c_ref.shape, F32)
        acc_ref[...] = jnp.zeros(acc_ref.shape, F32)
        t_idx = lax.broadcasted_iota(jnp.int32, c_ref.shape, 0) % n_new
        k_idx = lax.broadcasted_iota(jnp.int32, c_ref.shape, 1)
        tile(new_ref[0], k_idx < t_idx)

    @pl.when(s > 0)
    def _():
        tile(page_ref[0, 0], None)

    @pl.when(s == pl.num_programs(1) - 1)
    def _():
        acc = acc_ref[...]
        r_idx = lax.broadcasted_iota(jnp.int32, acc.shape, 0) // n_new
        l_idx = lax.broadcasted_iota(jnp.int32, acc.shape, 1) // d
        own = jnp.where(r_idx == l_idx, acc, 0.0)
        o_ref[0] = jnp.sum(own.reshape(H, n_new, H * d), axis=0).astype(o_ref.dtype)


def sb_sample(sq, skv, cache_sb, page_table, layer):
    B, tn = sq.shape[:2]
    H, d = SB_HEADS, SB_D
    n_pages = page_table.shape[1]
    page = cache_sb.shape[2]
    cache = cache_sb.reshape(cache_sb.shape[0], cache_sb.shape[1], page, 2 * H * d)
    q = (sq * (d ** -0.5)).astype(BF16).reshape(B, tn, H, d).transpose(0, 2, 1, 3)
    q_blk = (q[:, :, :, None, :] * jnp.eye(H, dtype=BF16)[None, :, None, :, None]).reshape(B, H * tn, H * d)
    new = jnp.pad(skv.reshape(B, tn, 2 * H * d), ((0, 0), (0, page - tn), (0, 0)))
    suffix = _suffix_matrix(page)
    grid_spec = pltpu.PrefetchScalarGridSpec(
        num_scalar_prefetch=1,
        grid=(B, n_pages + 1),
        in_specs=[
            pl.BlockSpec((1, H * tn, H * d), lambda b, s, pt: (b, 0, 0)),
            pl.BlockSpec((1, page, 2 * H * d), lambda b, s, pt: (b, 0, 0)),
            pl.BlockSpec((1, 1, page, 2 * H * d),
                         lambda b, s, pt: (pt[b, n_pages - jnp.maximum(s, 1)], layer, 0, 0)),
            pl.BlockSpec(suffix.shape, lambda b, s, pt: (0, 0)),
        ],
        out_specs=pl.BlockSpec((1, tn, H * d), lambda b, s, pt: (b, 0, 0)),
        scratch_shapes=[pltpu.VMEM((H * tn, page), F32), pltpu.VMEM((H * tn, H * d), F32)],
    )
    return pl.pallas_call(
        functools.partial(_sb_sample_kernel, n_new=tn),
        grid_spec=grid_spec,
        out_shape=jax.ShapeDtypeStruct((B, tn, H * d), BF16),
        compiler_params=pltpu.CompilerParams(dimension_semantics=("parallel", "arbitrary"),
                                             vmem_limit_bytes=VMEM_LIMIT_BYTES),
        name="sb_sample",
    )(page_table, q_blk, new, cache, suffix)


def _nsa_sample_cmp_kernel(pt_ref, q_ref, page_ref, posa_ref, posb_ref, w1_ref, w2_ref, biasc_ref, mapt_ref,
                           oc_ref, sel_ref, x_ref, *, n_new, past_len):
    s = pl.program_id(1)
    page = page_ref.shape[2]
    G, R = NSA_KV_HEADS, NSA_GROUP
    gw = G * NSA_D
    per = gw // LANES
    for c in range(x_ref.shape[0]):
        x_ref[c, pl.ds(pl.multiple_of(s * page, page), page), :] = page_ref[0, 0, :, c * LANES:(c + 1) * LANES]

    @pl.when(s == pl.num_programs(1) - 1)
    def _():
        n_sub = x_ref.shape[1] // CMP_STRIDE
        n_cmp = n_sub - 1
        kv = []
        for t in range(2):
            first = jnp.zeros((n_sub, gw), F32)
            second = jnp.zeros((n_sub, gw), F32)
            for l in range(CMP_STRIDE):
                x = jnp.concatenate([x_ref[t * per + c, pl.ds(l, n_sub, stride=CMP_STRIDE), :] for c in range(per)],
                                    axis=1)
                first += jnp.dot((x + posa_ref[t, l]).astype(BF16), w1_ref[t, l], preferred_element_type=F32)
                second += jnp.dot((x + posb_ref[t, l]).astype(BF16), w1_ref[t, CMP_STRIDE + l],
                                  preferred_element_type=F32)
            pre = first + pltpu.roll(second, n_sub - 1, 0)
            hid = (pre * jax.nn.sigmoid(pre)).astype(BF16)
            kv.append(jnp.dot(hid, w2_ref[t], preferred_element_type=F32).astype(BF16))
        kc, vc = kv
        nt = (((1,), (1,)), ((), ()))
        sc = lax.dot_general(q_ref[0], kc, nt, preferred_element_type=F32) + biasc_ref[...]
        ok = lax.broadcasted_iota(jnp.int32, sc.shape, 1) < n_cmp
        sc = jnp.where(ok, sc, NEG)
        e = jnp.exp(sc - jnp.max(sc, axis=-1, keepdims=True))
        p_c = jnp.where(ok, e / jnp.sum(e, axis=-1, keepdims=True), 0.0)
        oc_ref[0] = jnp.dot(p_c.astype(BF16), vc, preferred_element_type=F32)
        psum = jnp.sum(p_c.reshape(G, R, n_new, n_sub), axis=1).reshape(G * n_new, n_sub)
        p_hi, p_lo = _split_hi_lo(psum)
        imp = (lax.dot_general(mapt_ref[...], p_hi, nt, preferred_element_type=F32)
               + lax.dot_general(mapt_ref[...], p_lo, nt, preferred_element_type=F32))
        ns_pad, cols = imp.shape
        ns = past_len // SEL_BLOCK + 1
        j_idx = lax.broadcasted_iota(jnp.int32, imp.shape, 0)
        t_pos = past_len + lax.broadcasted_iota(jnp.int32, imp.shape, 1) % n_new
        cur = t_pos // SEL_BLOCK
        valid = (j_idx * SEL_BLOCK <= t_pos) & (j_idx < ns)
        forced = (j_idx == 0) | (j_idx == cur) | (j_idx == cur - 1)
        score = jnp.where(valid, jnp.where(forced, FORCE, imp), NEG)

        def rank_step(i, rank):
            row = sel_scratch_row(score, i)
            beats = (row > score) | ((row == score) & (j_idx > i))
            return rank + jnp.where(beats, 1.0, 0.0)

        def sel_scratch_row(a, i):
            return jnp.sum(jnp.where(j_idx == i, a, 0.0), axis=0, keepdims=True)

        rank = lax.fori_loop(0, ns, rank_step, jnp.zeros(imp.shape, F32))
        sel_t = jnp.where((rank < float(min(N_SEL, ns))) & valid, 1.0, 0.0)
        sel = sel_t.T.reshape(G, 1, n_new, ns_pad)
        sel_ref[0] = jnp.broadcast_to(sel, (G, R, n_new, ns_pad)).reshape(G * R * n_new, ns_pad).astype(BF16)


def _two_part_softmax_av(s_a, v_a, s_b, v_b):
    m = jnp.maximum(jnp.max(s_a, axis=-1, keepdims=True), jnp.max(s_b, axis=-1, keepdims=True))
    p_a = jnp.exp(s_a - m)
    p_b = jnp.exp(s_b - m)
    den = jnp.sum(p_a, axis=-1, keepdims=True) + jnp.sum(p_b, axis=-1, keepdims=True)
    num = (jnp.dot(p_a.astype(BF16), v_a, preferred_element_type=F32)
           + jnp.dot(p_b.astype(BF16), v_b, preferred_element_type=F32))
    return num / den


def _nsa_sample_attn_kernel(pt_ref, q_ref, sel_ref, oc_ref, gate_ref, new_ref, win_ref, page_ref, bnew_ref,
                            bwin_ref, blast_ref, bfar_ref, o_ref, m_ref, l_ref, acc_ref, ow_ref, *, n_pages):
    s = pl.program_id(1)
    G, d = NSA_KV_HEADS, NSA_D
    gw = G * d
    page = page_ref.shape[2]
    rows = q_ref.shape[1]
    q = q_ref[0]
    nt = (((1,), (1,)), ((), ()))

    def online(sc, v):
        m_old = m_ref[...]
        m_new = jnp.maximum(m_old, jnp.max(sc, axis=-1, keepdims=True))
        alpha = jnp.exp(m_old - m_new)
        p = jnp.exp(sc - m_new)
        l_ref[...] = alpha * l_ref[...] + jnp.sum(p, axis=-1, keepdims=True)
        acc_ref[...] = alpha * acc_ref[...] + jnp.dot(p.astype(BF16), v, preferred_element_type=F32)
        m_ref[...] = m_new

    @pl.when(s == 0)
    def _():
        m_ref[...] = jnp.full(m_ref.shape, NEG, F32)
        l_ref[...] = jnp.zeros(l_ref.shape, F32)
        acc_ref[...] = jnp.zeros(acc_ref.shape, F32)
        new = new_ref[0].astype(BF16)
        online(lax.dot_general(q, new[:, :gw], nt, preferred_element_type=F32) + bnew_ref[...], new[:, gw:2 * gw])
        win = win_ref[0, 0].astype(BF16)
        s_w = lax.dot_general(q, win[:, :gw], nt, preferred_element_type=F32) + bwin_ref[...]
        s_n = lax.dot_general(q, new[:, 2 * gw:3 * gw], nt, preferred_element_type=F32) + bnew_ref[...]
        ow_ref[...] = _two_part_softmax_av(s_w, win[:, gw:], s_n, new[:, 3 * gw:])

    @pl.when(s > 0)
    def _():
        pg = n_pages - s
        blk = page_ref[0, 0].astype(BF16)
        sc = lax.dot_general(q, blk[:, :gw], nt, preferred_element_type=F32)
        sc = sc + jnp.where(s == 1, blast_ref[...], bfar_ref[...])
        ns_pad = sel_ref.shape[2]
        blk_idx = lax.broadcasted_iota(jnp.int32, (ns_pad, page), 0)
        key_blk = pg * (page // SEL_BLOCK) + lax.broadcasted_iota(jnp.int32, (ns_pad, page), 1) // SEL_BLOCK
        expand = jnp.where(blk_idx == key_blk, 1.0, 0.0).astype(BF16)
        hit = jnp.dot(sel_ref[0], expand, preferred_element_type=F32)
        online(jnp.where(hit > 0.5, sc, NEG), blk[:, gw:])

    @pl.when(s == pl.num_programs(1) - 1)
    def _():
        g = jax.nn.sigmoid(gate_ref[0])
        o = g[:, 0:1] * oc_ref[0] + g[:, 1:2] * (acc_ref[...] / l_ref[...]) + g[:, 2:3] * ow_ref[...]
        r_grp = lax.broadcasted_iota(jnp.int32, o.shape, 0) // (rows // G)
        l_grp = lax.broadcasted_iota(jnp.int32, o.shape, 1) // d
        own = jnp.where(r_grp == l_grp, o, 0.0)
        out = own[:, 0:d]
        for gi in range(1, G):
            out = out + own[:, gi * d:(gi + 1) * d]
        o_ref[0] = out.astype(o_ref.dtype)


def nsa_sample(nq, ng, nkv, cache_nsa, cache_win_l, page_table, layer, cmp_pos, cmp_w1, cmp_w2, rpb):
    B, tn = nq.shape[:2]
    G, R, d = NSA_KV_HEADS, NSA_GROUP, NSA_D
    gw = G * d
    n_pages = page_table.shape[1]
    page = cache_nsa.shape[2]
    past_len = n_pages * page
    assert tn < CMP_STRIDE and page % SEL_BLOCK == 0 and past_len % SEL_BLOCK == 0
    wb = cache_win_l.shape[1]
    n_sub = past_len // CMP_STRIDE
    ns = past_len // SEL_BLOCK + 1
    ns_pad = -(-ns // LANES) * LANES
    rows = G * R * tn
    cache = cache_nsa.reshape(cache_nsa.shape[0], cache_nsa.shape[1], page, 4 * gw)
    q = (nq * (d ** -0.5)).astype(BF16).reshape(B, tn, G, R, d).transpose(0, 2, 3, 1, 4)
    q_blk = (q[:, :, :, :, None, :] * jnp.eye(G, dtype=BF16)[None, :, None, None, :, None]).reshape(B, rows, gw)
    gates = ng.reshape(B, tn, G, R, 3).transpose(0, 2, 3, 1, 4).reshape(B, rows, 3)
    new = jnp.pad(nkv[:, :, 2:6].reshape(B, tn, 4 * gw), ((0, 0), (0, page - tn), (0, 0)))
    win = cache_win_l.reshape(B, 1, wb, 2 * gw)
    eye_g = jnp.eye(G, dtype=F32)
    w1 = jnp.einsum('tlde,gh->tlgdhe', cmp_w1, eye_g).reshape(2, CMP_LEN, gw, G * CMP_HID).astype(BF16)
    w2 = jnp.einsum('ted,gh->tgehd', cmp_w2, eye_g).reshape(2, G * CMP_HID, gw).astype(BF16)
    pos = jnp.tile(cmp_pos[:, :, None, :], (1, 1, G, 1)).reshape(2, CMP_LEN, 1, gw)
    pos_a, pos_b = pos[:, :CMP_STRIDE], pos[:, CMP_STRIDE:]
    table = rpb.astype(F32).reshape(N_BUCKETS, G, R).transpose(1, 2, 0)
    q_pos = past_len + jnp.arange(tn)

    def bias_rows(k_pos, ok):
        dist = q_pos[:, None] - k_pos[None, :]
        b = jnp.take(table, _rel_bucket(dist), axis=2)
        return jnp.where(ok(dist, k_pos[None, :]), b, NEG).reshape(rows, k_pos.shape[0])

    c_end = jnp.arange(n_sub) * CMP_STRIDE + (CMP_LEN - 1)
    bias_c = bias_rows(c_end, lambda dist, kp: dist >= 0)
    new_pos = past_len + jnp.arange(page)
    bias_new = bias_rows(new_pos, lambda dist, kp: (dist >= 0) & (kp < past_len + tn))
    win_pos = past_len - wb + jnp.arange(wb)
    bias_win = bias_rows(win_pos, lambda dist, kp: (dist >= 0) & (dist < WINDOW) & (kp >= 0))
    bias_last = bias_rows(past_len - page + jnp.arange(page), lambda dist, kp: dist >= 0)
    bias_far = jnp.broadcast_to(table[:, :, None, N_BUCKETS - 1:], (G, R, tn, 1)).reshape(rows, 1)
    map_t = jnp.pad(_cmp_sel_map(n_sub, ns).T, ((0, ns_pad - ns), (0, 0))).astype(BF16)

    const = lambda a: pl.BlockSpec(a.shape, lambda b, s, pt: (0,) * a.ndim)
    per_b = lambda a: pl.BlockSpec((1,) + a.shape[1:], lambda b, s, pt: (b,) + (0,) * (a.ndim - 1))
    params = pltpu.CompilerParams(dimension_semantics=("parallel", "arbitrary"), vmem_limit_bytes=VMEM_LIMIT_BYTES)
    o_c, sel = pl.pallas_call(
        functools.partial(_nsa_sample_cmp_kernel, n_new=tn, past_len=past_len),
        grid_spec=pltpu.PrefetchScalarGridSpec(
            num_scalar_prefetch=1,
            grid=(B, n_pages),
            in_specs=[
                per_b(q_blk),
                pl.BlockSpec((1, 1, page, 2 * gw), lambda b, s, pt: (pt[b, s], layer, 0, 0)),
                const(pos_a), const(pos_b), const(w1), const(w2), const(bias_c), const(map_t),
            ],
            out_specs=[pl.BlockSpec((1, rows, gw), lambda b, s, pt: (b, 0, 0)),
                       pl.BlockSpec((1, rows, ns_pad), lambda b, s, pt: (b, 0, 0))],
            scratch_shapes=[pltpu.VMEM((2 * gw // LANES, past_len, LANES), F32)],
        ),
        out_shape=[jax.ShapeDtypeStruct((B, rows, gw), F32), jax.ShapeDtypeStruct((B, rows, ns_pad), BF16)],
        compiler_params=params,
        name="nsa_sample_cmp",
    )(page_table, q_blk, cache, pos_a, pos_b, w1, w2, bias_c, map_t)
    o = pl.pallas_call(
        functools.partial(_nsa_sample_attn_kernel, n_pages=n_pages),
        grid_spec=pltpu.PrefetchScalarGridSpec(
            num_scalar_prefetch=1,
            grid=(B, n_pages + 1),
            in_specs=[
                per_b(q_blk), per_b(sel), per_b(o_c), per_b(gates), per_b(new), per_b(win),
                pl.BlockSpec((1, 1, page, 2 * gw),
                             lambda b, s, pt: (pt[b, n_pages - jnp.maximum(s, 1)], layer, 0, 1)),
                const(bias_new), const(bias_win), const(bias_last), const(bias_far),
            ],
            out_specs=pl.BlockSpec((1, rows, d), lambda b, s, pt: (b, 0, 0)),
            scratch_shapes=[pltpu.VMEM((rows, 1), F32), pltpu.VMEM((rows, 1), F32), pltpu.VMEM((rows, gw), F32),
                            pltpu.VMEM((rows, gw), F32)],
        ),
        out_shape=jax.ShapeDtypeStruct((B, rows, d), BF16),
        compiler_params=params,
        name="nsa_sample_attn",
    )(page_table, q_blk, sel, o_c, gates, new, win, cache, bias_new, bias_win, bias_last, bias_far)
    return o.reshape(B, G * R, tn, d).transpose(0, 2, 1, 3).reshape(B, tn, G * R * d)


def _rmsnorm(x, g):
    xf = x.astype(jnp.float32)
    y = xf * lax.rsqrt(jnp.mean(xf * xf, axis=-1, keepdims=True) + NORM_EPS)
    return (y * g.astype(jnp.float32)).astype(x.dtype)


def _rel_bucket(dist):
    n = jnp.maximum(dist, 0)
    nf = jnp.maximum(n, 1).astype(jnp.float32)
    large = RPB_MAX_EXACT + (jnp.log(nf / RPB_MAX_EXACT) / math.log(RPB_MAX_DIST / RPB_MAX_EXACT)
                             * (N_BUCKETS - RPB_MAX_EXACT)).astype(jnp.int32)
    return jnp.where(n < RPB_MAX_EXACT, n, jnp.minimum(large, N_BUCKETS - 1))


def _gla_scan(q, k, v, log_a, s0):
    B, T = q.shape[:2]
    C = min(GLA_CHUNK, T)
    n_chunks = -(-T // C)
    pad = n_chunks * C - T

    def prep(a):
        a = jnp.pad(a, ((0, 0), (0, pad), (0, 0), (0, 0)))
        return a.reshape(B, n_chunks, C, a.shape[2], a.shape[3]).transpose(1, 0, 3, 2, 4)

    xs = (prep(q), prep(k), prep(v), prep(log_a.astype(jnp.float32)))
    tri = jnp.tril(jnp.ones((C, C), dtype=bool))

    def step(S, inp):
        qb, kb, vb, ab = inp
        qf, kf, vf = qb.astype(jnp.float32), kb.astype(jnp.float32), vb.astype(jnp.float32)
        b = jnp.cumsum(ab, axis=2)
        o_inter = jnp.einsum('bhcd,bhde->bhce', qf * jnp.exp(b), S)
        rel = jnp.where(tri[None, None, :, :, None], b[:, :, :, None, :] - b[:, :, None, :, :], -jnp.inf)
        att = jnp.einsum('bhtd,bhsd,bhtsd->bhts', qf, kf, jnp.exp(rel))
        o = o_inter + jnp.einsum('bhts,bhse->bhte', att, vf)
        b_last = b[:, :, -1:, :]
        S = jnp.exp(b_last[:, :, 0, :])[..., None] * S + jnp.einsum('bhsd,bhse->bhde', kf * jnp.exp(b_last - b), vf)
        return S, o

    S, o = lax.scan(step, s0.astype(jnp.float32), xs)
    o = o.transpose(1, 0, 3, 2, 4).reshape(B, n_chunks * C, GLA_HEADS, GLA_DV)[:, :T]
    return o.astype(q.dtype), S.astype(s0.dtype)


def _gla_mixer(q, k, v, a_low, r, s0, w_decay, b_decay, out_norm):
    B, T = q.shape[:2]
    q = q.reshape(B, T, GLA_HEADS, GLA_DK) * (GLA_DK ** -0.5)
    k = k.reshape(B, T, GLA_HEADS, GLA_DK)
    v = v.reshape(B, T, GLA_HEADS, GLA_DV)
    log_a = (jax.nn.log_sigmoid((a_low @ w_decay + b_decay).astype(jnp.float32)) / GLA_TAU).reshape(B, T, GLA_HEADS, GLA_DK)
    o, state = _gla_scan(q, k, v, log_a, s0)
    o = _rmsnorm(o, out_norm).reshape(B, T, GLA_VAL) * jax.nn.silu(r)
    return o, state


def _nsa_compress(rows, pos_emb, w1, w2):
    B, S = rows.shape[:2]
    n_sub = S // CMP_STRIDE
    sub = rows[:, :n_sub * CMP_STRIDE].reshape(B, n_sub, CMP_STRIDE, NSA_KV_HEADS, NSA_D)
    first = jnp.einsum('bnlgd,lde->bnge', sub + pos_emb[None, None, :CMP_STRIDE, None, :], w1[:CMP_STRIDE])
    second = jnp.einsum('bnlgd,lde->bnge', sub + pos_emb[None, None, CMP_STRIDE:, None, :], w1[CMP_STRIDE:])
    hid = jax.nn.silu(first[:, :-1] + second[:, 1:])
    return jnp.einsum('bnge,ed->bngd', hid, w2)


def _sel_blocks(rows):
    B, S = rows.shape[:2]
    ns = -(-S // SEL_BLOCK)
    rows = jnp.pad(rows, ((0, 0), (0, ns * SEL_BLOCK - S), (0, 0), (0, 0)))
    return rows.reshape(B, ns, SEL_BLOCK, NSA_KV_HEADS, NSA_D).transpose(0, 3, 1, 2, 4)


def _cmp_sel_map(nc, ns):
    c_start = jnp.arange(nc) * CMP_STRIDE
    s_start = jnp.arange(ns) * SEL_BLOCK
    hit = (c_start[:, None] < s_start[None, :] + SEL_BLOCK) & (c_start[:, None] + CMP_LEN > s_start[None, :])
    return hit.astype(jnp.float32)


def _nsa_block(q, gates, q_pos, kc, vc, c_end, ksb, vsb, wk, wv, w_pos, rpb):
    B, Q, G, R, d = q.shape
    scale = d ** -0.5
    table = rpb.astype(jnp.float32).reshape(N_BUCKETS, G, R)
    s = jnp.einsum('bqgrd,bngd->bqgrn', q, kc).astype(jnp.float32) * scale
    s = s + table[_rel_bucket(q_pos[:, None] - c_end[None, :])].transpose(0, 2, 3, 1)[None]
    m = (c_end[None, :] <= q_pos[:, None])[None, :, None, None, :]
    p_c = jax.nn.softmax(jnp.where(m, s, NEG), axis=-1) * m
    o_c = jnp.einsum('bqgrn,bngd->bqgrd', p_c.astype(vc.dtype), vc)
    ns = ksb.shape[2]
    imp = jnp.einsum('bqgn,nj->bqgj', p_c.sum(axis=3), _cmp_sel_map(kc.shape[1], ns))
    j = jnp.arange(ns)[None, :]
    cur = (q_pos // SEL_BLOCK)[:, None]
    valid = (j * SEL_BLOCK <= q_pos[:, None])[None, :, None, :]
    forced = ((j == 0) | (j == cur) | (j == cur - 1))[None, :, None, :]
    score = jnp.where(valid, jnp.where(forced, FORCE, imp), NEG)
    _, idx = lax.top_k(score, min(N_SEL, ns))
    idx = idx.transpose(0, 2, 1, 3)
    bi = jnp.arange(B)[:, None, None, None]
    gi = jnp.arange(G)[None, :, None, None]
    kg = ksb[bi, gi, idx]
    vg = vsb[bi, gi, idx]
    k_pos = idx[..., None] * SEL_BLOCK + jnp.arange(SEL_BLOCK)
    s = jnp.einsum('bqgrd,bgqnld->bgqrnl', q, kg).astype(jnp.float32) * scale
    bias = table.transpose(1, 0, 2)[gi[..., None], _rel_bucket(q_pos[None, None, :, None, None] - k_pos)]
    s = s + bias.transpose(0, 1, 2, 5, 3, 4)
    m = (k_pos <= q_pos[None, None, :, None, None])[:, :, :, None]
    shp = s.shape
    p = jax.nn.softmax(jnp.where(m, s, NEG).reshape(shp[0], shp[1], shp[2], shp[3], -1), axis=-1).reshape(shp)
    o_s = jnp.einsum('bgqrnl,bgqnld->bqgrd', p.astype(vg.dtype), vg)
    s = jnp.einsum('bqgrd,bsgd->bqgrs', q, wk).astype(jnp.float32) * scale
    dist = q_pos[:, None] - w_pos[None, :]
    s = s + table[_rel_bucket(dist)].transpose(0, 2, 3, 1)[None]
    m = ((dist >= 0) & (dist < WINDOW) & (w_pos[None, :] >= 0))[None, :, None, None, :]
    p = jax.nn.softmax(jnp.where(m, s, NEG), axis=-1)
    o_w = jnp.einsum('bqgrs,bsgd->bqgrd', p.astype(wv.dtype), wv)
    g = jax.nn.sigmoid(gates.astype(jnp.float32))
    o = g[..., 0:1] * o_c + g[..., 1:2] * o_s + g[..., 2:3] * o_w
    return o.reshape(B, Q, G * R * d).astype(q.dtype)


def _sb_block(q, q_pos, k, v, k_pos):
    z = jnp.einsum('bqhd,bshd->bhqs', q, k).astype(jnp.float32) * (SB_D ** -0.5)
    m = (k_pos[None, :] < q_pos[:, None])[None, None]
    log_fail = jnp.where(m, jax.nn.log_sigmoid(-z), 0.0)
    later = lax.cumsum(log_fail, axis=3, reverse=True) - log_fail
    a = jnp.where(m, jnp.exp(jax.nn.log_sigmoid(z) + later), 0.0)
    return jnp.einsum('bhqs,bshd->bqhd', a.astype(v.dtype), v)


def _split(z):
    cuts = [int(c) for c in np.cumsum(IN_SPLITS)[:-1]]
    return jnp.split(z[..., :N_IN], cuts, axis=-1)


def _mix_prompt(z, gla_w_decay, gla_b_decay, gla_out_norm, cmp_pos, cmp_w1, cmp_w2, rpb):
    B, T = z.shape[:2]
    gq, gk, gv, ga, gr, nq, nkv, ng, sq, skv, mg = _split(z)
    s0 = jnp.zeros((B, GLA_HEADS, GLA_DK, GLA_DV), jnp.float32)
    o_gla, gla_state = gla_mixer(gq, gk, gv, ga, gr, s0, gla_w_decay, gla_b_decay, gla_out_norm)
    nkv = nkv.reshape(B, T, 6, NSA_KV_HEADS, NSA_D)
    o_nsa = nsa_prompt(nq, ng, nkv, cmp_pos, cmp_w1, cmp_w2, rpb)
    skv = skv.reshape(B, T, 2, SB_HEADS, SB_D)
    o_sb = sb_prompt(sq, skv)
    keep = min(WINDOW, T)
    return (o_gla, o_nsa, o_sb, mg), nkv[:, :, :4], skv, nkv[:, T - keep:, 4:], gla_state


def _mix_sample(z, cache_nsa, cache_sb, page_table, layer, win_buf, s0, gla_w_decay, gla_b_decay,
                gla_out_norm, cmp_pos, cmp_w1, cmp_w2, rpb):
    B, T = z.shape[:2]
    gq, gk, gv, ga, gr, nq, nkv, ng, sq, skv, mg = _split(z)
    o_gla, gla_state = gla_mixer(gq, gk, gv, ga, gr, s0, gla_w_decay, gla_b_decay, gla_out_norm)
    nkv = nkv.reshape(B, T, 6, NSA_KV_HEADS, NSA_D)
    o_nsa = nsa_sample(nq, ng, nkv, cache_nsa, win_buf, page_table, layer, cmp_pos, cmp_w1, cmp_w2, rpb)
    win = jnp.concatenate([win_buf, nkv[:, :, 4:]], axis=1)
    wb = win_buf.shape[1]
    skv = skv.reshape(B, T, 2, SB_HEADS, SB_D)
    o_sb = sb_sample(sq, skv, cache_sb, page_table, layer)
    keep = min(WINDOW, wb + T)
    return (o_gla, o_nsa, o_sb, mg), nkv[:, :, :4], skv, win[:, wb + T - keep:], gla_state


def kernel(x_prompt, x_sample, cache_nsa, cache_sb, cache_win, state_gla, page_table, ffn1_norm, ffn1_w_gate, ffn1_w_up, ffn1_w_down, mix_norm, w_in, gla_w_decay, gla_b_decay, gla_out_norm, nsa_cmp_pos, nsa_cmp_w1, nsa_cmp_w2, rpb_table, w_branch_gla, w_branch_nsa, w_branch_sb, w_out, ffn2_norm, ffn2_w_gate, ffn2_w_up, ffn2_w_down, final_norm):
    bp, tp, _ = x_prompt.shape
    bs, ts, _ = x_sample.shape
    mp, ms = bp * tp, bs * ts
    dec_batch, n_pages = page_table.shape
    past_len = n_pages * cache_nsa.shape[2]
    x = jnp.concatenate([x_prompt.reshape(mp, D_MODEL), x_sample.reshape(ms, D_MODEL)], axis=0)
    outs = [[] for _ in range(8)]
    for l in range(DEPTH):
        x = ffn_half_step(x, ffn1_norm[l], ffn1_w_gate[l].astype(BF16), ffn1_w_up[l].astype(BF16),
                          ffn1_w_down[l].astype(BF16))
        w_in_l = jnp.pad(w_in[l].astype(BF16), ((0, 0), (0, N_IN_PAD - N_IN)))
        z = in_projection(x, mix_norm[l], w_in_l)
        mix_w = (gla_w_decay[l], gla_b_decay[l], gla_out_norm[l], nsa_cmp_pos[l], nsa_cmp_w1[l], nsa_cmp_w2[l],
                 rpb_table)
        br_p, r_nsa, r_sb, r_win, r_gla = _mix_prompt(z[:mp].reshape(bp, tp, N_IN_PAD), *mix_w)
        for lst, val in zip(outs[0::2], (r_nsa, r_sb, r_win, r_gla)):
            lst.append(val)
        br_s, r_nsa, r_sb, r_win, r_gla = _mix_sample(z[mp:].reshape(bs, ts, N_IN_PAD), cache_nsa, cache_sb,
                                                      page_table, l, cache_win[:, l], state_gla[:, l], *mix_w)
        for lst, val in zip(outs[1::2], (r_nsa, r_sb, r_win, r_gla)):
            lst.append(val)
        o_gla, o_nsa, o_sb, gates = (
            jnp.concatenate([a.reshape(mp, a.shape[-1]), b.reshape(ms, b.shape[-1])], axis=0)
            for a, b in zip(br_p, br_s))
        y = merge_branches(o_gla.astype(BF16), o_nsa.astype(BF16), o_sb.astype(BF16), gates,
                           w_branch_gla[l].astype(BF16), w_branch_nsa[l].astype(BF16), w_branch_sb[l].astype(BF16))
        x = out_projection(x, y, w_out[l].astype(BF16))
        x = ffn_half_step(x, ffn2_norm[l], ffn2_w_gate[l].astype(BF16), ffn2_w_up[l].astype(BF16),
                          ffn2_w_down[l].astype(BF16))
    y = final_rmsnorm(x, final_norm)
    return (y[:mp].reshape(bp, tp, D_MODEL), y[mp:].reshape(bs, ts, D_MODEL)) + tuple(
        jnp.stack(o, axis=1) for o in outs)
```

```python
import functools
import math

import jax
import jax.numpy as jnp
import numpy as np
from jax import lax
from jax.experimental import pallas as pl
from jax.experimental.pallas import tpu as pltpu

D_MODEL = 2048
DEPTH = 2
D_FF = 5632
NORM_EPS = 1e-6
GLA_HEADS = 4
GLA_DK = 128
GLA_DV = 256
GLA_RANK = 16
GLA_TAU = 16.0
GLA_CHUNK = 64
GLA_KEY = GLA_HEADS * GLA_DK
GLA_VAL = GLA_HEADS * GLA_DV
NSA_HEADS = 16
NSA_KV_HEADS = 4
NSA_GROUP = NSA_HEADS // NSA_KV_HEADS
NSA_D = 64
NSA_OUT = NSA_HEADS * NSA_D
NSA_KVW = NSA_KV_HEADS * NSA_D
CMP_STRIDE = 16
CMP_LEN = 2 * CMP_STRIDE
CMP_HID = 64
SEL_BLOCK = 64
N_SEL = 16
WINDOW = 512
NSA_QBLK = 32
SB_HEADS = 16
SB_D = 64
SB_OUT = SB_HEADS * SB_D
SB_QBLK = 128
N_BUCKETS = 32
RPB_MAX_EXACT = 16
RPB_MAX_DIST = 128
NEG = -1e30
FORCE = 1e9
IN_SPLITS = (GLA_KEY, GLA_KEY, GLA_VAL, GLA_RANK, GLA_VAL, NSA_OUT, 6 * NSA_KVW, 3 * NSA_HEADS, SB_OUT, 2 * SB_OUT,
             3 * D_MODEL)
N_IN = sum(IN_SPLITS)

BF16 = jnp.bfloat16
F32 = jnp.float32

VMEM_LIMIT_BYTES = 56 * 1024 * 1024
LANES = 128
ROW_TILE = 640
FF_TILE = 512
PROJ_TILE = 1152
N_IN_PAD = 14976
OUT_TILE = 512


def _rms_bf16(x, g):
    y = x * lax.rsqrt(jnp.mean(x * x, axis=-1, keepdims=True) + NORM_EPS)
    return (y * g).astype(BF16)


def _ffn_kernel(x_ref, g_ref, wg_ref, wu_ref, wd_ref, o_ref, h_ref, acc_ref):
    f = pl.program_id(1)

    @pl.when(f == 0)
    def _():
        h_ref[...] = _rms_bf16(x_ref[...], g_ref[...])
        acc_ref[...] = jnp.zeros_like(acc_ref)

    h = h_ref[...]
    a = jnp.dot(h, wg_ref[...], preferred_element_type=F32)
    b = jnp.dot(h, wu_ref[...], preferred_element_type=F32)
    act = (a * jax.nn.sigmoid(a) * b).astype(BF16)
    acc_ref[...] += jnp.dot(act, wd_ref[...], preferred_element_type=F32)

    @pl.when(f == pl.num_programs(1) - 1)
    def _():
        o_ref[...] = x_ref[...] + 0.5 * acc_ref[...]


def ffn_half_step(x, g, wg, wu, wd):
    m = x.shape[0]
    return pl.pallas_call(
        _ffn_kernel,
        grid=(m // ROW_TILE, D_FF // FF_TILE),
        in_specs=[
            pl.BlockSpec((ROW_TILE, D_MODEL), lambda i, f: (i, 0)),
            pl.BlockSpec((1, D_MODEL), lambda i, f: (0, 0)),
            pl.BlockSpec((D_MODEL, FF_TILE), lambda i, f: (0, f)),
            pl.BlockSpec((D_MODEL, FF_TILE), lambda i, f: (0, f)),
            pl.BlockSpec((FF_TILE, D_MODEL), lambda i, f: (f, 0)),
        ],
        out_specs=pl.BlockSpec((ROW_TILE, D_MODEL), lambda i, f: (i, 0)),
        out_shape=jax.ShapeDtypeStruct((m, D_MODEL), F32),
        scratch_shapes=[pltpu.VMEM((ROW_TILE, D_MODEL), BF16), pltpu.VMEM((ROW_TILE, D_MODEL), F32)],
        compiler_params=pltpu.CompilerParams(dimension_semantics=("parallel", "arbitrary"),
                                             vmem_limit_bytes=VMEM_LIMIT_BYTES),
        name="ffn_half_step",
    )(x, g.reshape(1, D_MODEL), wg, wu, wd)


def _proj_kernel(x_ref, g_ref, w_ref, o_ref, h_ref):
    @pl.when(pl.program_id(1) == 0)
    def _():
        h_ref[...] = _rms_bf16(x_ref[...], g_ref[...])

    o_ref[...] = jnp.dot(h_ref[...], w_ref[...], preferred_element_type=F32)


def in_projection(x, g, w):
    m = x.shape[0]
    return pl.pallas_call(
        _proj_kernel,
        grid=(m // ROW_TILE, N_IN_PAD // PROJ_TILE),
        in_specs=[
            pl.BlockSpec((ROW_TILE, D_MODEL), lambda i, n: (i, 0)),
            pl.BlockSpec((1, D_MODEL), lambda i, n: (0, 0)),
            pl.BlockSpec((D_MODEL, PROJ_TILE), lambda i, n: (0, n)),
        ],
        out_specs=pl.BlockSpec((ROW_TILE, PROJ_TILE), lambda i, n: (i, n)),
        out_shape=jax.ShapeDtypeStruct((m, N_IN_PAD), F32),
        scratch_shapes=[pltpu.VMEM((ROW_TILE, D_MODEL), BF16)],
        compiler_params=pltpu.CompilerParams(dimension_semantics=("parallel", "arbitrary"),
                                             vmem_limit_bytes=VMEM_LIMIT_BYTES),
        name="in_projection",
    )(x, g.reshape(1, D_MODEL), w)


def _merge_kernel(oa_ref, ob_ref, oc_ref, ga_ref, gb_ref, gc_ref, wa_ref, wb_ref, wc_ref, y_ref):
    ya = jnp.dot(oa_ref[...], wa_ref[...], preferred_element_type=F32)
    yb = jnp.dot(ob_ref[...], wb_ref[...], preferred_element_type=F32)
    yc = jnp.dot(oc_ref[...], wc_ref[...], preferred_element_type=F32)
    y = jax.nn.sigmoid(ga_ref[...]) * ya + jax.nn.sigmoid(gb_ref[...]) * yb + jax.nn.sigmoid(gc_ref[...]) * yc
    y_ref[...] = y.astype(BF16)


def merge_branches(o_gla, o_nsa, o_sb, gates, wa, wb, wc):
    m = o_gla.shape[0]
    nb = D_MODEL // OUT_TILE
    o_spec = lambda width: pl.BlockSpec((ROW_TILE, width), lambda i, n: (i, 0))
    g_spec = lambda k: pl.BlockSpec((ROW_TILE, OUT_TILE), lambda i, n, k=k: (i, n + k * nb))
    w_spec = lambda width: pl.BlockSpec((width, OUT_TILE), lambda i, n: (0, n))
    return pl.pallas_call(
        _merge_kernel,
        grid=(m // ROW_TILE, nb),
        in_specs=[o_spec(GLA_VAL), o_spec(NSA_OUT), o_spec(SB_OUT), g_spec(0), g_spec(1), g_spec(2),
                  w_spec(GLA_VAL), w_spec(NSA_OUT), w_spec(SB_OUT)],
        out_specs=pl.BlockSpec((ROW_TILE, OUT_TILE), lambda i, n: (i, n)),
        out_shape=jax.ShapeDtypeStruct((m, D_MODEL), BF16),
        compiler_params=pltpu.CompilerParams(dimension_semantics=("parallel", "arbitrary"),
                                             vmem_limit_bytes=VMEM_LIMIT_BYTES),
        name="merge_branches",
    )(o_gla, o_nsa, o_sb, gates, gates, gates, wa, wb, wc)


def _out_kernel(x_ref, y_ref, w_ref, o_ref):
    o_ref[...] = x_ref[...] + jnp.dot(y_ref[...], w_ref[...], preferred_element_type=F32)


def out_projection(x, y, w):
    m = x.shape[0]
    return pl.pallas_call(
        _out_kernel,
        grid=(m // ROW_TILE, D_MODEL // OUT_TILE),
        in_specs=[
            pl.BlockSpec((ROW_TILE, OUT_TILE), lambda i, n: (i, n)),
            pl.BlockSpec((ROW_TILE, D_MODEL), lambda i, n: (i, 0)),
            pl.BlockSpec((D_MODEL, OUT_TILE), lambda i, n: (0, n)),
        ],
        out_specs=pl.BlockSpec((ROW_TILE, OUT_TILE), lambda i, n: (i, n)),
        out_shape=jax.ShapeDtypeStruct((m, D_MODEL), F32),
        compiler_params=pltpu.CompilerParams(dimension_semantics=("parallel", "arbitrary"),
                                             vmem_limit_bytes=VMEM_LIMIT_BYTES),
        name="out_projection",
    )(x, y, w)


def _norm_kernel(x_ref, g_ref, o_ref):
    x = x_ref[...]
    o_ref[...] = x * lax.rsqrt(jnp.mean(x * x, axis=-1, keepdims=True) + NORM_EPS) * g_ref[...]


def final_rmsnorm(x, g):
    m = x.shape[0]
    return pl.pallas_call(
        _norm_kernel,
        grid=(m // ROW_TILE,),
        in_specs=[pl.BlockSpec((ROW_TILE, D_MODEL), lambda i: (i, 0)), pl.BlockSpec((1, D_MODEL), lambda i: (0, 0))],
        out_specs=pl.BlockSpec((ROW_TILE, D_MODEL), lambda i: (i, 0)),
        out_shape=jax.ShapeDtypeStruct((m, D_MODEL), F32),
        compiler_params=pltpu.CompilerParams(dimension_semantics=("parallel",), vmem_limit_bytes=VMEM_LIMIT_BYTES),
        name="final_rmsnorm",
    )(x, g.reshape(1, D_MODEL))


NSA_TQ = 128
NSA_TK = 256
SEL_TOEP = 4
WIN_TOEP = 6


def _toeplitz_bias(rpb, window):
    n = WIN_TOEP if window else SEL_TOEP
    u = jnp.arange(NSA_TQ)[:, None]
    w = jnp.arange(NSA_TK)[None, :]
    d = (jnp.arange(n) * NSA_TQ)[:, None, None] + (u - w)[None]
    b = rpb.astype(F32)[_rel_bucket(d)]
    ok = (d >= 0) & (d < WINDOW) if window else (d >= 0)
    b = jnp.where(ok[..., None], b, NEG)
    return b.transpose(3, 0, 1, 2).reshape(NSA_KV_HEADS, NSA_GROUP, n, NSA_TQ, NSA_TK)


def _split_hi_lo(x):
    hi = x.astype(BF16)
    lo = (x - hi.astype(F32)).astype(BF16)
    return hi, lo


def _nsa_prompt_kernel(q_ref, gate_ref, cmp_ref, kv_ref, posa_ref, posb_ref, w1a_ref, w1b_ref, w2_ref,
                       biasc_ref, stoep_ref, wtoep_ref, mapt_ref, expand_ref, o_ref,
                       kc_ref, vc_ref, m_ref, l_ref, acc_ref, *, n_cmp):
    qi = pl.program_id(2)
    tq, tk, R = NSA_TQ, NSA_TK, NSA_GROUP
    rows = R * tq

    @pl.when(qi == 0)
    def _():
        for t, dst in ((0, kc_ref), (1, vc_ref)):
            x = cmp_ref[0, 0, t]
            first = jnp.dot((x + posa_ref[t]).astype(BF16), w1a_ref[t], preferred_element_type=F32)
            second = jnp.dot((x + posb_ref[t]).astype(BF16), w1b_ref[t], preferred_element_type=F32)
            pre = first + pltpu.roll(second, second.shape[0] - 1, 0)
            hid = (pre * jax.nn.sigmoid(pre)).astype(BF16)
            dst[...] = jnp.dot(hid, w2_ref[t], preferred_element_type=F32).astype(BF16)

    q4 = q_ref[0, 0].reshape(rows, NSA_D)
    q_pos = qi * tq + lax.broadcasted_iota(jnp.int32, (tq, 1), 0)

    ncp = kc_ref.shape[0]
    s = lax.dot_general(q4, kc_ref[...], (((1,), (1,)), ((), ())), preferred_element_type=F32)
    s = s.reshape(R, tq, ncp) + biasc_ref[0]
    n_idx = lax.broadcasted_iota(jnp.int32, (tq, ncp), 1)
    ok = ((n_idx * CMP_STRIDE + (CMP_LEN - 1)) <= q_pos) & (n_idx < n_cmp)
    s = jnp.where(ok[None], s, NEG)
    e = jnp.exp(s - jnp.max(s, axis=-1, keepdims=True))
    p_c = jnp.where(ok[None], e / jnp.sum(e, axis=-1, keepdims=True), 0.0)
    o_c = jnp.dot(p_c.reshape(rows, ncp).astype(BF16), vc_ref[...], preferred_element_type=F32)
    psum = jnp.sum(p_c, axis=0)
    p_hi, p_lo = _split_hi_lo(psum)
    nt = (((1,), (1,)), ((), ()))
    imp = (lax.dot_general(mapt_ref[...], p_hi, nt, preferred_element_type=F32)
           + lax.dot_general(mapt_ref[...], p_lo, nt, preferred_element_type=F32))

    ns = imp.shape[0]
    j_idx = lax.broadcasted_iota(jnp.int32, (ns, tq), 0)
    t_pos = qi * tq + lax.broadcasted_iota(jnp.int32, (ns, tq), 1)
    cur = t_pos // SEL_BLOCK
    valid = j_idx * SEL_BLOCK <= t_pos
    forced = (j_idx == 0) | (j_idx == cur) | (j_idx == cur - 1)
    score = jnp.where(valid, jnp.where(forced, FORCE, imp), NEG)
    rank = jnp.zeros((ns, tq), F32)
    for i in range(ns):
        row = score[i:i + 1, :]
        beats = (row > score) | ((row == score) & (j_idx > i))
        rank = rank + jnp.where(beats, 1.0, 0.0)
    sel_t = jnp.where((rank < float(min(N_SEL, ns))) & valid, 1.0, 0.0)
    sel = sel_t.T.astype(BF16)

    def attend(k_idx, v_idx, toep_ref, n_toep, j_lo, use_sel):
        m_ref[...] = jnp.full(m_ref.shape, NEG, F32)
        l_ref[...] = jnp.zeros(l_ref.shape, F32)
        acc_ref[...] = jnp.zeros(acc_ref.shape, F32)
        j_hi = qi // 2

        def body(step, carry):
            j = j_hi - step
            k = kv_ref[0, 0, k_idx, pl.ds(pl.multiple_of(j * tk, tk), tk), :]
            v = kv_ref[0, 0, v_idx, pl.ds(pl.multiple_of(j * tk, tk), tk), :]
            sc = lax.dot_general(q4, k, (((1,), (1,)), ((), ())), preferred_element_type=F32)
            off = jnp.minimum(qi - 2 * j, n_toep - 1)
            sc = sc.reshape(R, tq, tk) + toep_ref[0, :, off]
            if use_sel:
                hit = jnp.dot(sel, expand_ref[j], preferred_element_type=F32)
                sc = jnp.where((hit > 0.5)[None], sc, NEG)
            sc = sc.reshape(rows, tk)
            m_old = m_ref[...]
            m_new = jnp.maximum(m_old, jnp.max(sc, axis=-1, keepdims=True))
            alpha = jnp.exp(m_old - m_new)
            p = jnp.exp(sc - m_new)
            l_ref[...] = alpha * l_ref[...] + jnp.sum(p, axis=-1, keepdims=True)
            acc_ref[...] = alpha * acc_ref[...] + jnp.dot(p.astype(BF16), v, preferred_element_type=F32)
            m_ref[...] = m_new
            return carry

        lax.fori_loop(0, j_hi - j_lo + 1, body, 0)
        return acc_ref[...] / l_ref[...]

    o_s = attend(0, 1, stoep_ref, SEL_TOEP, 0, True)
    o_w = attend(2, 3, wtoep_ref, WIN_TOEP, jnp.maximum(qi * tq - (WINDOW - 1), 0) // tk, False)

    g = jax.nn.sigmoid(gate_ref[0, 0])
    for r in range(R):
        sl = slice(r * tq, (r + 1) * tq)
        o = (g[:, 3 * r:3 * r + 1] * o_c[sl] + g[:, 3 * r + 1:3 * r + 2] * o_s[sl]
             + g[:, 3 * r + 2:3 * r + 3] * o_w[sl])
        o_ref[0, 0, r] = o.astype(o_ref.dtype)


def nsa_prompt(nq, ng, nkv, cmp_pos, cmp_w1, cmp_w2, rpb):
    B, T = nq.shape[:2]
    G, R, d = NSA_KV_HEADS, NSA_GROUP, NSA_D
    n_sub = T // CMP_STRIDE
    n_cmp = n_sub - 1
    ns = T // SEL_BLOCK
    nq_t = T // NSA_TQ
    nk_t = T // NSA_TK
    q = (nq * (d ** -0.5)).astype(BF16).reshape(B, T, G, R, d).transpose(0, 2, 3, 1, 4)
    gates = ng.reshape(B, T, G, 3 * R).transpose(0, 2, 1, 3)
    cmp_rows = nkv[:, :, 0:2].transpose(0, 3, 2, 1, 4).reshape(B, G, 2, n_sub, CMP_STRIDE * d)
    kv = nkv[:, :, 2:6].astype(BF16).transpose(0, 3, 2, 1, 4)
    pos_a = cmp_pos[:, :CMP_STRIDE].reshape(2, 1, CMP_STRIDE * d)
    pos_b = cmp_pos[:, CMP_STRIDE:].reshape(2, 1, CMP_STRIDE * d)
    w1a = cmp_w1[:, :CMP_STRIDE].reshape(2, CMP_STRIDE * d, CMP_HID).astype(BF16)
    w1b = cmp_w1[:, CMP_STRIDE:].reshape(2, CMP_STRIDE * d, CMP_HID).astype(BF16)
    w2 = cmp_w2.astype(BF16)
    t_pos = jnp.arange(T)[:, None]
    c_end = jnp.arange(n_sub)[None, :] * CMP_STRIDE + (CMP_LEN - 1)
    bias_c = rpb.astype(F32)[_rel_bucket(t_pos - c_end)].transpose(2, 0, 1).reshape(G, R, T, n_sub)
    stoep = _toeplitz_bias(rpb, False)
    wtoep = _toeplitz_bias(rpb, True)
    map_t = _cmp_sel_map(n_sub, ns).T.astype(BF16)
    expand = (jnp.arange(ns)[None, :, None]
              == ((jnp.arange(nk_t)[:, None, None] * NSA_TK + jnp.arange(NSA_TK)[None, None, :]) // SEL_BLOCK)
              ).astype(BF16)
    full = lambda a: pl.BlockSpec(a.shape, lambda b, g, i: (0,) * a.ndim)
    per_g = lambda a: pl.BlockSpec((1,) + a.shape[1:], lambda b, g, i: (g,) + (0,) * (a.ndim - 1))
    o = pl.pallas_call(
        functools.partial(_nsa_prompt_kernel, n_cmp=n_cmp),
        grid=(B, G, nq_t),
        in_specs=[
            pl.BlockSpec((1, 1, R, NSA_TQ, d), lambda b, g, i: (b, g, 0, i, 0)),
            pl.BlockSpec((1, 1, NSA_TQ, 3 * R), lambda b, g, i: (b, g, i, 0)),
            pl.BlockSpec((1, 1, 2, n_sub, CMP_STRIDE * d), lambda b, g, i: (b, g, 0, 0, 0)),
            pl.BlockSpec((1, 1, 4, T, d), lambda b, g, i: (b, g, 0, 0, 0)),
            full(pos_a), full(pos_b), full(w1a), full(w1b), full(w2),
            pl.BlockSpec((1, R, NSA_TQ, n_sub), lambda b, g, i: (g, 0, i, 0)),
            per_g(stoep), per_g(wtoep), full(map_t), full(expand),
        ],
        out_specs=pl.BlockSpec((1, 1, R, NSA_TQ, d), lambda b, g, i: (b, g, 0, i, 0)),
        out_shape=jax.ShapeDtypeStruct((B, G, R, T, d), BF16),
        scratch_shapes=[pltpu.VMEM((n_sub, d), BF16), pltpu.VMEM((n_sub, d), BF16),
                        pltpu.VMEM((R * NSA_TQ, 1), F32), pltpu.VMEM((R * NSA_TQ, 1), F32),
                        pltpu.VMEM((R * NSA_TQ, d), F32)],
        compiler_params=pltpu.CompilerParams(dimension_semantics=("parallel", "parallel", "arbitrary"),
                                             vmem_limit_bytes=VMEM_LIMIT_BYTES),
        name="nsa_prompt",
    )(q, gates, cmp_rows, kv, pos_a, pos_b, w1a, w1b, w2, bias_c, stoep, wtoep, map_t, expand)
    return o.transpose(0, 3, 1, 2, 4).reshape(B, T, G * R * d)


def _toeplitz_bias_t(rpb, window):
    n = WIN_TOEP if window else SEL_TOEP
    w = jnp.arange(NSA_TK)[:, None]
    u = jnp.arange(NSA_TQ)[None, :]
    d = (jnp.arange(n) * NSA_TQ)[:, None, None] + (u - w)[None]
    b = rpb.astype(F32)[_rel_bucket(d)]
    ok = (d >= 0) & (d < WINDOW) if window else (d >= 0)
    b = jnp.where(ok[..., None], b, NEG).reshape(n, NSA_TK, NSA_TQ, NSA_KV_HEADS, NSA_GROUP)
    return b.transpose(3, 0, 1, 4, 2).reshape(NSA_KV_HEADS, n, NSA_TK, NSA_GROUP * NSA_TQ)


def _nsa_prompt_t_kernel(q_ref, gate_ref, cmp_ref, k_ref, vt_ref, posa_ref, posb_ref, w1a_ref, w1b_ref, w2_ref,
                         biasc_ref, stoep_ref, wtoep_ref, mapt_ref, expand_ref, o_ref,
                         kc_ref, vct_ref, acc_ref, *, n_cmp):
    qi = pl.program_id(2)
    tq, tk, R = NSA_TQ, NSA_TK, NSA_GROUP
    cols = R * tq
    nt = (((1,), (1,)), ((), ()))

    @pl.when(qi == 0)
    def _():
        for t in range(2):
            x = cmp_ref[0, 0, t]
            first = jnp.dot((x + posa_ref[t]).astype(BF16), w1a_ref[t], preferred_element_type=F32)
            second = jnp.dot((x + posb_ref[t]).astype(BF16), w1b_ref[t], preferred_element_type=F32)
            pre = first + pltpu.roll(second, second.shape[0] - 1, 0)
            hid = (pre * jax.nn.sigmoid(pre)).astype(BF16)
            c = jnp.dot(hid, w2_ref[t], preferred_element_type=F32)
            if t == 0:
                kc_ref[...] = c.astype(BF16)
            else:
                vct_ref[...] = c.T.astype(BF16)

    q4 = q_ref[0, 0].reshape(cols, NSA_D)
    lane_q = lax.broadcasted_iota(jnp.int32, (1, cols), 1) % tq
    q_pos = qi * tq + lane_q

    ncp = kc_ref.shape[0]
    s = lax.dot_general(kc_ref[...], q4, nt, preferred_element_type=F32) + biasc_ref[0, 0]
    n_idx = lax.broadcasted_iota(jnp.int32, (ncp, cols), 0)
    ok = ((n_idx * CMP_STRIDE + (CMP_LEN - 1)) <= q_pos) & (n_idx < n_cmp)
    s = jnp.where(ok, s, NEG)
    e = jnp.exp(s - jnp.max(s, axis=0, keepdims=True))
    p_c = jnp.where(ok, e / jnp.sum(e, axis=0, keepdims=True), 0.0)
    o_c = jnp.dot(vct_ref[...], p_c.astype(BF16), preferred_element_type=F32)
    psum = p_c[:, 0:tq]
    for r in range(1, R):
        psum = psum + p_c[:, r * tq:(r + 1) * tq]
    p_hi, p_lo = _split_hi_lo(psum)
    imp = (jnp.dot(mapt_ref[...], p_hi, preferred_element_type=F32)
           + jnp.dot(mapt_ref[...], p_lo, preferred_element_type=F32))

    ns = imp.shape[0]
    j_idx = lax.broadcasted_iota(jnp.int32, (ns, tq), 0)
    t_pos = qi * tq + lax.broadcasted_iota(jnp.int32, (ns, tq), 1)
    cur = t_pos // SEL_BLOCK
    valid = j_idx * SEL_BLOCK <= t_pos
    forced = (j_idx == 0) | (j_idx == cur) | (j_idx == cur - 1)
    score = jnp.where(valid, jnp.where(forced, FORCE, imp), NEG)
    rank = jnp.zeros((ns, tq), F32)
    for i in range(ns):
        row = score[i:i + 1, :]
        beats = (row > score) | ((row == score) & (j_idx > i))
        rank = rank + jnp.where(beats, 1.0, 0.0)
    sel_t = jnp.where((rank < float(min(N_SEL, ns))) & valid, 1.0, 0.0).astype(BF16)
    sel4 = jnp.concatenate([sel_t] * R, axis=1)

    def attend(idx, toep_ref, n_toep, j_lo, use_sel):
        acc_ref[...] = jnp.zeros(acc_ref.shape, F32)
        j_hi = qi // 2

        def body(step, carry):
            m_old, l_old = carry
            j = j_hi - step
            k = k_ref[0, 0, idx, pl.ds(pl.multiple_of(j * tk, tk), tk), :]
            vt = vt_ref[0, 0, idx, :, pl.ds(pl.multiple_of(j * tk, tk), tk)]
            off = jnp.minimum(qi - 2 * j, n_toep - 1)
            sc = lax.dot_general(k, q4, nt, preferred_element_type=F32) + toep_ref[0, off]
            if use_sel:
                hit = jnp.dot(expand_ref[j], sel4, preferred_element_type=F32)
                sc = jnp.where(hit > 0.5, sc, NEG)
            m_new = jnp.maximum(m_old, jnp.max(sc, axis=0, keepdims=True))
            alpha = jnp.exp(m_old - m_new)
            p = jnp.exp(sc - m_new)
            l_new = alpha * l_old + jnp.sum(p, axis=0, keepdims=True)
            acc_ref[...] = alpha * acc_ref[...] + jnp.dot(vt, p.astype(BF16), preferred_element_type=F32)
            return m_new, l_new

        init = (jnp.full((1, cols), NEG, F32), jnp.zeros((1, cols), F32))
        _, l_fin = lax.fori_loop(0, j_hi - j_lo + 1, body, init)
        return acc_ref[...] / l_fin

    o_s = attend(0, stoep_ref, SEL_TOEP, 0, True)
    o_w = attend(1, wtoep_ref, WIN_TOEP, jnp.maximum(qi * tq - (WINDOW - 1), 0) // tk, False)

    g = jax.nn.sigmoid(gate_ref[0, 0])
    for r in range(R):
        sl = slice(r * tq, (r + 1) * tq)
        o = (g[3 * r:3 * r + 1] * o_c[:, sl] + g[3 * r + 1:3 * r + 2] * o_s[:, sl]
             + g[3 * r + 2:3 * r + 3] * o_w[:, sl])
        o_ref[0, 0, r] = o.astype(o_ref.dtype)


def nsa_prompt_t(nq, ng, nkv, cmp_pos, cmp_w1, cmp_w2, rpb):
    B, T = nq.shape[:2]
    G, R, d = NSA_KV_HEADS, NSA_GROUP, NSA_D
    n_sub = T // CMP_STRIDE
    n_cmp = n_sub - 1
    ns = T // SEL_BLOCK
    nq_t = T // NSA_TQ
    nk_t = T // NSA_TK
    q = (nq * (d ** -0.5)).astype(BF16).reshape(B, T, G, R, d).transpose(0, 2, 3, 1, 4)
    gates = ng.reshape(B, T, G, 3 * R).transpose(0, 2, 3, 1)
    cmp_rows = nkv[:, :, 0:2].transpose(0, 3, 2, 1, 4).reshape(B, G, 2, n_sub, CMP_STRIDE * d)
    k = jnp.stack([nkv[:, :, 2].astype(BF16).transpose(0, 2, 1, 3),
                   nkv[:, :, 4].astype(BF16).transpose(0, 2, 1, 3)], axis=2)
    vt = jnp.stack([nkv[:, :, 3].astype(BF16).transpose(0, 2, 3, 1),
                    nkv[:, :, 5].astype(BF16).transpose(0, 2, 3, 1)], axis=2)
    pos_a = cmp_pos[:, :CMP_STRIDE].reshape(2, 1, CMP_STRIDE * d)
    pos_b = cmp_pos[:, CMP_STRIDE:].reshape(2, 1, CMP_STRIDE * d)
    w1a = cmp_w1[:, :CMP_STRIDE].reshape(2, CMP_STRIDE * d, CMP_HID).astype(BF16)
    w1b = cmp_w1[:, CMP_STRIDE:].reshape(2, CMP_STRIDE * d, CMP_HID).astype(BF16)
    w2 = cmp_w2.astype(BF16)
    t_pos = jnp.arange(T)[None, :]
    c_end = jnp.arange(n_sub)[:, None] * CMP_STRIDE + (CMP_LEN - 1)
    bias_c = rpb.astype(F32)[_rel_bucket(t_pos - c_end)]
    bias_c = bias_c.reshape(n_sub, nq_t, NSA_TQ, G, R).transpose(3, 1, 0, 4, 2).reshape(G, nq_t, n_sub, R * NSA_TQ)
    stoep = _toeplitz_bias_t(rpb, False)
    wtoep = _toeplitz_bias_t(rpb, True)
    map_t = _cmp_sel_map(n_sub, ns).T.astype(BF16)
    expand = ((jnp.arange(nk_t)[:, None, None] * NSA_TK + jnp.arange(NSA_TK)[None, :, None]) // SEL_BLOCK
              == jnp.arange(ns)[None, None, :]).astype(BF16)
    full = lambda a: pl.BlockSpec(a.shape, lambda b, g, i: (0,) * a.ndim)
    per_g = lambda a: pl.BlockSpec((1,) + a.shape[1:], lambda b, g, i: (g,) + (0,) * (a.ndim - 1))
    per_bg = lambda a: pl.BlockSpec((1, 1) + a.shape[2:], lambda b, g, i: (b, g) + (0,) * (a.ndim - 2))
    o = pl.pallas_call(
        functools.partial(_nsa_prompt_t_kernel, n_cmp=n_cmp),
        grid=(B, G, nq_t),
        in_specs=[
            pl.BlockSpec((1, 1, R, NSA_TQ, d), lambda b, g, i: (b, g, 0, i, 0)),
            pl.BlockSpec((1, 1, 3 * R, NSA_TQ), lambda b, g, i: (b, g, 0, i)),
            per_bg(cmp_rows), per_bg(k), per_bg(vt),
            full(pos_a), full(pos_b), full(w1a), full(w1b), full(w2),
            pl.BlockSpec((1, 1, n_sub, R * NSA_TQ), lambda b, g, i: (g, i, 0, 0)),
            per_g(stoep), per_g(wtoep), full(map_t), full(expand),
        ],
        out_specs=pl.BlockSpec((1, 1, R, d, NSA_TQ), lambda b, g, i: (b, g, 0, 0, i)),
        out_shape=jax.ShapeDtypeStruct((B, G, R, d, T), BF16),
        scratch_shapes=[pltpu.VMEM((n_sub, d), BF16), pltpu.VMEM((d, n_sub), BF16),
                        pltpu.VMEM((d, R * NSA_TQ), F32)],
        compiler_params=pltpu.CompilerParams(dimension_semantics=("parallel", "parallel", "arbitrary"),
                                             vmem_limit_bytes=VMEM_LIMIT_BYTES),
        name="nsa_prompt",
    )(q, gates, cmp_rows, k, vt, pos_a, pos_b, w1a, w1b, w2, bias_c, stoep, wtoep, map_t, expand)
    return o.transpose(0, 4, 1, 2, 3).reshape(B, T, G * R * d)


SB_TQ = 256
SB_TK = 256
SB_HEADS_PER_STEP = 4


def _softplus(z):
    return jnp.maximum(z, 0.0) + jnp.log(1.0 + jnp.exp(-jnp.abs(z)))


def _suffix_matrix(n):
    j = jnp.arange(n)[:, None]
    s = jnp.arange(n)[None, :]
    return jnp.concatenate([(j > s), jnp.ones((n, n), bool)], axis=1).astype(BF16)


def _sb_tile(z, ok, v, suffix, c_ref, acc_ref):
    tk = z.shape[1]
    sp = _softplus(z)
    lf = -sp if ok is None else jnp.where(ok, -sp, 0.0)
    hi, lo = _split_hi_lo(lf)
    tot = (jnp.dot(hi, suffix, preferred_element_type=F32) + jnp.dot(lo, suffix, preferred_element_type=F32))
    later = tot[:, :tk] + c_ref[...]
    a = jnp.exp(z - sp + later)
    if ok is not None:
        a = jnp.where(ok, a, 0.0)
    acc_ref[...] += jnp.dot(a.astype(BF16), v, preferred_element_type=F32)
    c_ref[...] += tot[:, tk:]


def _sb_prompt_kernel(q_ref, k_ref, v_ref, suffix_ref, o_ref, c_ref, acc_ref):
    qi = pl.program_id(2)
    tq, tk = SB_TQ, SB_TK
    per = tq // tk
    heads = q_ref.shape[1]
    suffix = suffix_ref[...]
    c_ref[...] = jnp.zeros(c_ref.shape, F32)
    acc_ref[...] = jnp.zeros(acc_ref.shape, F32)
    nt = (((1,), (1,)), ((), ()))
    q_pos = qi * tq + lax.broadcasted_iota(jnp.int32, (tq, tk), 0)
    k_off = lax.broadcasted_iota(jnp.int32, (tq, tk), 1)

    def tile(j, masked):
        ok = (j * tk + k_off < q_pos) if masked else None
        for h in range(heads):
            k = k_ref[0, h, pl.ds(pl.multiple_of(j * tk, tk), tk), :]
            v = v_ref[0, h, pl.ds(pl.multiple_of(j * tk, tk), tk), :]
            z = lax.dot_general(q_ref[0, h], k, nt, preferred_element_type=F32)
            _sb_tile(z, ok, v, suffix, c_ref.at[h], acc_ref.at[h])

    for t in range(per):
        tile(qi * per + (per - 1 - t), True)

    def body(step, carry):
        tile(qi * per - 1 - step, False)
        return carry

    lax.fori_loop(0, qi * per, body, 0)
    o_ref[0] = acc_ref[...].astype(o_ref.dtype)


def sb_prompt(sq, skv):
    B, T = sq.shape[:2]
    H, d, hp = SB_HEADS, SB_D, SB_HEADS_PER_STEP
    q = (sq * (d ** -0.5)).astype(BF16).reshape(B, T, H, d).transpose(0, 2, 1, 3)
    k = skv[:, :, 0].astype(BF16).transpose(0, 2, 1, 3)
    v = skv[:, :, 1].astype(BF16).transpose(0, 2, 1, 3)
    suffix = _suffix_matrix(SB_TK)
    kv_spec = pl.BlockSpec((1, hp, T, d), lambda b, h, i: (b, h, 0, 0))
    q_spec = pl.BlockSpec((1, hp, SB_TQ, d), lambda b, h, i: (b, h, i, 0))
    o = pl.pallas_call(
        _sb_prompt_kernel,
        grid=(B, H // hp, T // SB_TQ),
        in_specs=[q_spec, kv_spec, kv_spec, pl.BlockSpec(suffix.shape, lambda b, h, i: (0, 0))],
        out_specs=q_spec,
        out_shape=jax.ShapeDtypeStruct((B, H, T, d), BF16),
        scratch_shapes=[pltpu.VMEM((hp, SB_TQ, SB_TK), F32), pltpu.VMEM((hp, SB_TQ, d), F32)],
        compiler_params=pltpu.CompilerParams(dimension_semantics=("parallel", "parallel", "arbitrary"),
                                             vmem_limit_bytes=VMEM_LIMIT_BYTES),
        name="sb_prompt",
    )(q, k, v, suffix)
    return o.transpose(0, 2, 1, 3).reshape(B, T, H * d)


GLA_SUB = 16
GLA_EXP_CAP = 60.0


def _gla_kernel(q_ref, k_ref, v_ref, a_ref, r_ref, wd_ref, bd_ref, gn_ref, s0_ref, o_ref, st_ref, s_ref, *, sub):
    c = pl.program_id(2)
    C = q_ref.shape[1]

    @pl.when(c == 0)
    def _():
        s_ref[...] = s0_ref[0, 0]

    q = q_ref[0] * (GLA_DK ** -0.5)
    k = k_ref[0]
    v = v_ref[0].astype(BF16)
    pre = jnp.dot(a_ref[0].astype(BF16), wd_ref[...], preferred_element_type=F32) + bd_ref[...]
    log_a = -_softplus(-pre) * (1.0 / GLA_TAU)
    row = lax.broadcasted_iota(jnp.int32, (C, C), 0)
    col = lax.broadcasted_iota(jnp.int32, (C, C), 1)
    lower = jnp.where(col <= row, 1.0, 0.0).astype(BF16)
    a_hi, a_lo = _split_hi_lo(log_a)
    b = jnp.dot(lower, a_hi, preferred_element_type=F32) + jnp.dot(lower, a_lo, preferred_element_type=F32)

    s_old = s_ref[...]
    o = jnp.dot((q * jnp.exp(b)).astype(BF16), s_old.astype(BF16), preferred_element_type=F32)

    nt = (((1,), (1,)), ((), ()))
    parts = []
    for i in range(C // sub):
        lo, hi = i * sub, (i + 1) * sub
        ref_row = b[lo:lo + 1, :]
        qs = (q[lo:hi] * jnp.exp(b[lo:hi] - ref_row)).astype(BF16)
        ks = (k[:hi] * jnp.exp(jnp.minimum(ref_row - b[:hi], GLA_EXP_CAP))).astype(BF16)
        att = lax.dot_general(qs, ks, nt, preferred_element_type=F32)
        t_idx = lo + lax.broadcasted_iota(jnp.int32, (sub, hi), 0)
        s_idx = lax.broadcasted_iota(jnp.int32, (sub, hi), 1)
        att = jnp.where(s_idx <= t_idx, att, 0.0)
        parts.append(jnp.dot(att.astype(BF16), v[:hi], preferred_element_type=F32))
    o = o + (parts[0] if len(parts) == 1 else jnp.concatenate(parts, axis=0))

    b_t = b.T
    last = b_t[:, C - 1:C]
    kd_t = (k.T * jnp.exp(last - b_t)).astype(BF16)
    s_new = jnp.exp(last) * s_old + jnp.dot(kd_t, v, preferred_element_type=F32)
    s_ref[...] = s_new

    y = o * lax.rsqrt(jnp.mean(o * o, axis=-1, keepdims=True) + NORM_EPS) * gn_ref[...]
    r = r_ref[0]
    o_ref[0] = (y * (r * jax.nn.sigmoid(r))).astype(o_ref.dtype)

    @pl.when(c == pl.num_programs(2) - 1)
    def _():
        st_ref[0, 0] = s_new


def gla_mixer(gq, gk, gv, ga, gr, s0, w_decay, b_decay, out_norm):
    B, T = gq.shape[:2]
    H, dk, dv = GLA_HEADS, GLA_DK, GLA_DV
    C = min(GLA_CHUNK, T)
    sub = min(GLA_SUB, C)
    tok = lambda w: pl.BlockSpec((1, C, w), lambda b, h, c: (b, c, h))
    o, st = pl.pallas_call(
        functools.partial(_gla_kernel, sub=sub),
        grid=(B, H, T // C),
        in_specs=[
            tok(dk), tok(dk), tok(dv),
            pl.BlockSpec((1, C, GLA_RANK), lambda b, h, c: (b, c, 0)),
            tok(dv),
            pl.BlockSpec((GLA_RANK, dk), lambda b, h, c: (0, h)),
            pl.BlockSpec((1, dk), lambda b, h, c: (0, h)),
            pl.BlockSpec((1, dv), lambda b, h, c: (0, 0)),
            pl.BlockSpec((1, 1, dk, dv), lambda b, h, c: (b, h, 0, 0)),
        ],
        out_specs=[tok(dv), pl.BlockSpec((1, 1, dk, dv), lambda b, h, c: (b, h, 0, 0))],
        out_shape=[jax.ShapeDtypeStruct((B, T, H * dv), BF16), jax.ShapeDtypeStruct((B, H, dk, dv), F32)],
        scratch_shapes=[pltpu.VMEM((dk, dv), F32)],
        compiler_params=pltpu.CompilerParams(dimension_semantics=("parallel", "parallel", "arbitrary"),
                                             vmem_limit_bytes=VMEM_LIMIT_BYTES),
        name="gla_mixer",
    )(gq, gk, gv, ga, gr, w_decay.astype(BF16), b_decay.reshape(1, H * dk), out_norm.reshape(1, dv), s0)
    return o, st


def _sb_sample_kernel(pt_ref, q_ref, new_ref, page_ref, suffix_ref, o_ref, c_ref, acc_ref, *, n_new):
    s = pl.program_id(1)
    H, d = SB_HEADS, SB_D
    rows = q_ref.shape[1]
    nt = (((1,), (1,)), ((), ()))

    def tile(blk, ok):
        k = blk[:, :H * d].astype(BF16)
        v = blk[:, H * d:].astype(BF16)
        z = lax.dot_general(q_ref[0], k, nt, preferred_element_type=F32)
        _sb_tile(z, ok, v, suffix_ref[...], c_ref, acc_ref)

    @pl.when(s == 0)
    def _():
        c_ref[...] = jnp.zeros(c_ref.shape, F32)
        acc_ref[...] = jnp.zeros(acc_ref.shape, F32)
        t_idx = lax.broadcasted_iota(jnp.int32, c_ref.shape, 0) % n_new
        k_idx = lax.broadcasted_iota(jnp.int32, c_ref.shape, 1)
        tile(new_ref[0], k_idx < t_idx)

    @pl.when(s > 0)
    def _():
        tile(page_ref[0, 0], None)

    @pl.when(s == pl.num_programs(1) - 1)
    def _():
        acc = acc_ref[...]
        r_idx = lax.broadcasted_iota(jnp.int32, acc.shape, 0) // n_new
        l_idx = lax.broadcasted_iota(jnp.int32, acc.shape, 1) // d
        own = jnp.where(r_idx == l_idx, acc, 0.0)
        o_ref[0] = jnp.sum(own.reshape(H, n_new, H * d), axis=0).astype(o_ref.dtype)


def sb_sample(sq, skv, cache_sb, page_table, layer):
    B, tn = sq.shape[:2]
    H, d = SB_HEADS, SB_D
    n_pages = page_table.shape[1]
    page = cache_sb.shape[2]
    cache = cache_sb.astype(BF16).reshape(cache_sb.shape[0], cache_sb.shape[1], page, 2 * H * d)
    q = (sq * (d ** -0.5)).astype(BF16).reshape(B, tn, H, d).transpose(0, 2, 1, 3)
    q_blk = (q[:, :, :, None, :] * jnp.eye(H, dtype=BF16)[None, :, None, :, None]).reshape(B, H * tn, H * d)
    new = jnp.pad(skv.reshape(B, tn, 2 * H * d), ((0, 0), (0, page - tn), (0, 0)))
    suffix = _suffix_matrix(page)
    grid_spec = pltpu.PrefetchScalarGridSpec(
        num_scalar_prefetch=1,
        grid=(B, n_pages + 1),
        in_specs=[
            pl.BlockSpec((1, H * tn, H * d), lambda b, s, pt: (b, 0, 0)),
            pl.BlockSpec((1, page, 2 * H * d), lambda b, s, pt: (b, 0, 0)),
            pl.BlockSpec((1, 1, page, 2 * H * d),
                         lambda b, s, pt: (pt[b, n_pages - jnp.maximum(s, 1)], layer, 0, 0)),
            pl.BlockSpec(suffix.shape, lambda b, s, pt: (0, 0)),
        ],
        out_specs=pl.BlockSpec((1, tn, H * d), lambda b, s, pt: (b, 0, 0)),
        scratch_shapes=[pltpu.VMEM((H * tn, page), F32), pltpu.VMEM((H * tn, H * d), F32)],
    )
    return pl.pallas_call(
        functools.partial(_sb_sample_kernel, n_new=tn),
        grid_spec=grid_spec,
        out_shape=jax.ShapeDtypeStruct((B, tn, H * d), BF16),
        compiler_params=pltpu.CompilerParams(dimension_semantics=("parallel", "arbitrary"),
                                             vmem_limit_bytes=VMEM_LIMIT_BYTES),
        name="sb_sample",
    )(page_table, q_blk, new, cache, suffix)


def _nsa_sample_cmp_kernel(pt_ref, q_ref, *refs, n_pg, n_new, past_len):
    page_refs = refs[:n_pg]
    posa_ref, posb_ref, w1_ref, w2_ref, biasc_ref, mapt_ref, oc_ref, sel_ref, x_ref = refs[n_pg:]
    s = pl.program_id(1)
    page = page_refs[0].shape[2]
    G, R = NSA_KV_HEADS, NSA_GROUP
    gw = G * NSA_D
    per = gw // LANES
    for i, page_ref in enumerate(page_refs):
        row0 = pl.multiple_of((s * n_pg + i) * page, page)
        for c in range(x_ref.shape[0]):
            x_ref[c, pl.ds(row0, page), :] = page_ref[0, 0, :, c * LANES:(c + 1) * LANES]

    @pl.when(s == pl.num_programs(1) - 1)
    def _():
        n_sub = x_ref.shape[1] // CMP_STRIDE
        n_cmp = n_sub - 1
        kv = []
        for t in range(2):
            first = jnp.zeros((n_sub, gw), F32)
            second = jnp.zeros((n_sub, gw), F32)
            for l in range(CMP_STRIDE):
                x = jnp.concatenate([x_ref[t * per + c, pl.ds(l, n_sub, stride=CMP_STRIDE), :] for c in range(per)],
                                    axis=1)
                first += jnp.dot((x + posa_ref[t, l]).astype(BF16), w1_ref[t, l], preferred_element_type=F32)
                second += jnp.dot((x + posb_ref[t, l]).astype(BF16), w1_ref[t, CMP_STRIDE + l],
                                  preferred_element_type=F32)
            pre = first + pltpu.roll(second, n_sub - 1, 0)
            hid = (pre * jax.nn.sigmoid(pre)).astype(BF16)
            kv.append(jnp.dot(hid, w2_ref[t], preferred_element_type=F32).astype(BF16))
        kc, vc = kv
        nt = (((1,), (1,)), ((), ()))
        sc = lax.dot_general(q_ref[0], kc, nt, preferred_element_type=F32) + biasc_ref[...]
        ok = lax.broadcasted_iota(jnp.int32, sc.shape, 1) < n_cmp
        sc = jnp.where(ok, sc, NEG)
        e = jnp.exp(sc - jnp.max(sc, axis=-1, keepdims=True))
        p_c = jnp.where(ok, e / jnp.sum(e, axis=-1, keepdims=True), 0.0)
        oc_ref[0] = jnp.dot(p_c.astype(BF16), vc, preferred_element_type=F32)
        psum = jnp.sum(p_c.reshape(G, R, n_new, n_sub), axis=1).reshape(G * n_new, n_sub)
        p_hi, p_lo = _split_hi_lo(psum)
        imp = (lax.dot_general(mapt_ref[...], p_hi, nt, preferred_element_type=F32)
               + lax.dot_general(mapt_ref[...], p_lo, nt, preferred_element_type=F32))
        ns_pad, cols = imp.shape
        ns = past_len // SEL_BLOCK + 1
        j_idx = lax.broadcasted_iota(jnp.int32, imp.shape, 0)
        t_pos = past_len + lax.broadcasted_iota(jnp.int32, imp.shape, 1) % n_new
        cur = t_pos // SEL_BLOCK
        valid = (j_idx * SEL_BLOCK <= t_pos) & (j_idx < ns)
        forced = (j_idx == 0) | (j_idx == cur) | (j_idx == cur - 1)
        score = jnp.where(valid, jnp.where(forced, FORCE, imp), NEG)

        def rank_step(i, rank):
            row = sel_scratch_row(score, i)
            beats = (row > score) | ((row == score) & (j_idx > i))
            return rank + jnp.where(beats, 1.0, 0.0)

        def sel_scratch_row(a, i):
            return jnp.sum(jnp.where(j_idx == i, a, 0.0), axis=0, keepdims=True)

        rank = lax.fori_loop(0, ns, rank_step, jnp.zeros(imp.shape, F32))
        sel_t = jnp.where((rank < float(min(N_SEL, ns))) & valid, 1.0, 0.0)
        sel = sel_t.T.reshape(G, 1, n_new, ns_pad)
        sel_ref[0] = jnp.broadcast_to(sel, (G, R, n_new, ns_pad)).reshape(G * R * n_new, ns_pad).astype(BF16)


def _two_part_softmax_av(s_a, v_a, s_b, v_b):
    m = jnp.maximum(jnp.max(s_a, axis=-1, keepdims=True), jnp.max(s_b, axis=-1, keepdims=True))
    p_a = jnp.exp(s_a - m)
    p_b = jnp.exp(s_b - m)
    den = jnp.sum(p_a, axis=-1, keepdims=True) + jnp.sum(p_b, axis=-1, keepdims=True)
    num = (jnp.dot(p_a.astype(BF16), v_a, preferred_element_type=F32)
           + jnp.dot(p_b.astype(BF16), v_b, preferred_element_type=F32))
    return num / den


def _nsa_sample_attn_kernel(pt_ref, q_ref, sel_ref, oc_ref, gate_ref, new_ref, win_ref, *refs, n_pg, n_pages):
    page_refs = refs[:n_pg]
    bnew_ref, bwin_ref, blast_ref, bfar_ref, o_ref, m_ref, l_ref, acc_ref, ow_ref = refs[n_pg:]
    s = pl.program_id(1)
    G, d = NSA_KV_HEADS, NSA_D
    gw = G * d
    page = page_refs[0].shape[2]
    rows = q_ref.shape[1]
    q = q_ref[0]
    nt = (((1,), (1,)), ((), ()))

    def online(sc, v):
        m_old = m_ref[...]
        m_new = jnp.maximum(m_old, jnp.max(sc, axis=-1, keepdims=True))
        alpha = jnp.exp(m_old - m_new)
        p = jnp.exp(sc - m_new)
        l_ref[...] = alpha * l_ref[...] + jnp.sum(p, axis=-1, keepdims=True)
        acc_ref[...] = alpha * acc_ref[...] + jnp.dot(p.astype(BF16), v, preferred_element_type=F32)
        m_ref[...] = m_new

    @pl.when(s == 0)
    def _():
        m_ref[...] = jnp.full(m_ref.shape, NEG, F32)
        l_ref[...] = jnp.zeros(l_ref.shape, F32)
        acc_ref[...] = jnp.zeros(acc_ref.shape, F32)
        new = new_ref[0].astype(BF16)
        online(lax.dot_general(q, new[:, :gw], nt, preferred_element_type=F32) + bnew_ref[...], new[:, gw:2 * gw])
        win = win_ref[0, 0].astype(BF16)
        s_w = lax.dot_general(q, win[:, :gw], nt, preferred_element_type=F32) + bwin_ref[...]
        s_n = lax.dot_general(q, new[:, 2 * gw:3 * gw], nt, preferred_element_type=F32) + bnew_ref[...]
        ow_ref[...] = _two_part_softmax_av(s_w, win[:, gw:], s_n, new[:, 3 * gw:])

    @pl.when(s > 0)
    def _():
        pg = n_pages - s * n_pg
        blk = jnp.concatenate([r[0, 0] for r in page_refs], axis=0)
        keys = blk.shape[0]
        sc = lax.dot_general(q, blk[:, :gw], nt, preferred_element_type=F32)
        sc = sc + jnp.where(s == 1, blast_ref[...], bfar_ref[...])
        ns_pad = sel_ref.shape[2]
        blk_idx = lax.broadcasted_iota(jnp.int32, (ns_pad, keys), 0)
        key_blk = pg * (page // SEL_BLOCK) + lax.broadcasted_iota(jnp.int32, (ns_pad, keys), 1) // SEL_BLOCK
        expand = jnp.where(blk_idx == key_blk, 1.0, 0.0).astype(BF16)
        hit = jnp.dot(sel_ref[0], expand, preferred_element_type=F32)
        online(jnp.where(hit > 0.5, sc, NEG), blk[:, gw:])

    @pl.when(s == pl.num_programs(1) - 1)
    def _():
        g = jax.nn.sigmoid(gate_ref[0])
        o = g[:, 0:1] * oc_ref[0] + g[:, 1:2] * (acc_ref[...] / l_ref[...]) + g[:, 2:3] * ow_ref[...]
        r_grp = lax.broadcasted_iota(jnp.int32, o.shape, 0) // (rows // G)
        l_grp = lax.broadcasted_iota(jnp.int32, o.shape, 1) // d
        own = jnp.where(r_grp == l_grp, o, 0.0)
        out = own[:, 0:d]
        for gi in range(1, G):
            out = out + own[:, gi * d:(gi + 1) * d]
        o_ref[0] = out.astype(o_ref.dtype)


def nsa_sample(nq, ng, nkv, cache_nsa, cache_win_l, page_table, layer, cmp_pos, cmp_w1, cmp_w2, rpb):
    B, tn = nq.shape[:2]
    G, R, d = NSA_KV_HEADS, NSA_GROUP, NSA_D
    gw = G * d
    n_pages = page_table.shape[1]
    page = cache_nsa.shape[2]
    past_len = n_pages * page
    assert tn < CMP_STRIDE and page % SEL_BLOCK == 0 and past_len % SEL_BLOCK == 0
    wb = cache_win_l.shape[1]
    n_sub = past_len // CMP_STRIDE
    ns = past_len // SEL_BLOCK + 1
    ns_pad = -(-ns // LANES) * LANES
    rows = G * R * tn
    pool_shape = cache_nsa.shape[:2] + (page, 2 * gw)
    cache_cmp = cache_nsa[:, :, :, 0:2].reshape(pool_shape)
    cache_sel = cache_nsa[:, :, :, 2:4].astype(BF16).reshape(pool_shape)
    q = (nq * (d ** -0.5)).astype(BF16).reshape(B, tn, G, R, d).transpose(0, 2, 3, 1, 4)
    q_blk = (q[:, :, :, :, None, :] * jnp.eye(G, dtype=BF16)[None, :, None, None, :, None]).reshape(B, rows, gw)
    gates = ng.reshape(B, tn, G, R, 3).transpose(0, 2, 3, 1, 4).reshape(B, rows, 3)
    new = jnp.pad(nkv[:, :, 2:6].reshape(B, tn, 4 * gw), ((0, 0), (0, page - tn), (0, 0)))
    win = cache_win_l.reshape(B, 1, wb, 2 * gw)
    eye_g = jnp.eye(G, dtype=F32)
    w1 = jnp.einsum('tlde,gh->tlgdhe', cmp_w1, eye_g).reshape(2, CMP_LEN, gw, G * CMP_HID).astype(BF16)
    w2 = jnp.einsum('ted,gh->tgehd', cmp_w2, eye_g).reshape(2, G * CMP_HID, gw).astype(BF16)
    pos = jnp.tile(cmp_pos[:, :, None, :], (1, 1, G, 1)).reshape(2, CMP_LEN, 1, gw)
    pos_a, pos_b = pos[:, :CMP_STRIDE], pos[:, CMP_STRIDE:]
    table = rpb.astype(F32).reshape(N_BUCKETS, G, R).transpose(1, 2, 0)
    q_pos = past_len + jnp.arange(tn)

    def bias_rows(k_pos, ok):
        dist = q_pos[:, None] - k_pos[None, :]
        b = jnp.take(table, _rel_bucket(dist), axis=2)
        return jnp.where(ok(dist, k_pos[None, :]), b, NEG).reshape(rows, k_pos.shape[0])

    c_end = jnp.arange(n_sub) * CMP_STRIDE + (CMP_LEN - 1)
    bias_c = bias_rows(c_end, lambda dist, kp: dist >= 0)
    new_pos = past_len + jnp.arange(page)
    bias_new = bias_rows(new_pos, lambda dist, kp: (dist >= 0) & (kp < past_len + tn))
    win_pos = past_len - wb + jnp.arange(wb)
    bias_win = bias_rows(win_pos, lambda dist, kp: (dist >= 0) & (dist < WINDOW) & (kp >= 0))
    n_pg = next(n for n in (4, 2, 1) if n_pages % n == 0)
    bias_last = bias_rows(past_len - n_pg * page + jnp.arange(n_pg * page), lambda dist, kp: dist >= 0)
    bias_far = jnp.broadcast_to(table[:, :, None, N_BUCKETS - 1:], (G, R, tn, 1)).reshape(rows, 1)
    map_t = jnp.pad(_cmp_sel_map(n_sub, ns).T, ((0, ns_pad - ns), (0, 0))).astype(BF16)

    const = lambda a: pl.BlockSpec(a.shape, lambda b, s, pt: (0,) * a.ndim)
    per_b = lambda a: pl.BlockSpec((1,) + a.shape[1:], lambda b, s, pt: (b,) + (0,) * (a.ndim - 1))
    params = pltpu.CompilerParams(dimension_semantics=("parallel", "arbitrary"), vmem_limit_bytes=VMEM_LIMIT_BYTES)
    o_c, sel = pl.pallas_call(
        functools.partial(_nsa_sample_cmp_kernel, n_pg=n_pg, n_new=tn, past_len=past_len),
        grid_spec=pltpu.PrefetchScalarGridSpec(
            num_scalar_prefetch=1,
            grid=(B, n_pages // n_pg),
            in_specs=[per_b(q_blk)] + [
                pl.BlockSpec((1, 1, page, 2 * gw), lambda b, s, pt, i=i: (pt[b, s * n_pg + i], layer, 0, 0))
                for i in range(n_pg)] + [
                const(pos_a), const(pos_b), const(w1), const(w2), const(bias_c), const(map_t),
            ],
            out_specs=[pl.BlockSpec((1, rows, gw), lambda b, s, pt: (b, 0, 0)),
                       pl.BlockSpec((1, rows, ns_pad), lambda b, s, pt: (b, 0, 0))],
            scratch_shapes=[pltpu.VMEM((2 * gw // LANES, past_len, LANES), F32)],
        ),
        out_shape=[jax.ShapeDtypeStruct((B, rows, gw), F32), jax.ShapeDtypeStruct((B, rows, ns_pad), BF16)],
        compiler_params=params,
        name="nsa_sample_cmp",
    )(page_table, q_blk, *([cache_cmp] * n_pg), pos_a, pos_b, w1, w2, bias_c, map_t)
    o = pl.pallas_call(
        functools.partial(_nsa_sample_attn_kernel, n_pg=n_pg, n_pages=n_pages),
        grid_spec=pltpu.PrefetchScalarGridSpec(
            num_scalar_prefetch=1,
            grid=(B, n_pages // n_pg + 1),
            in_specs=[per_b(q_blk), per_b(sel), per_b(o_c), per_b(gates), per_b(new), per_b(win)] + [
                pl.BlockSpec((1, 1, page, 2 * gw),
                             lambda b, s, pt, i=i: (pt[b, n_pages - jnp.maximum(s, 1) * n_pg + i], layer, 0, 0))
                for i in range(n_pg)] + [
                const(bias_new), const(bias_win), const(bias_last), const(bias_far),
            ],
            out_specs=pl.BlockSpec((1, rows, d), lambda b, s, pt: (b, 0, 0)),
            scratch_shapes=[pltpu.VMEM((rows, 1), F32), pltpu.VMEM((rows, 1), F32), pltpu.VMEM((rows, gw), F32),
                            pltpu.VMEM((rows, gw), F32)],
        ),
        out_shape=jax.ShapeDtypeStruct((B, rows, d), BF16),
        compiler_params=params,
        name="nsa_sample_attn",
    )(page_table, q_blk, sel, o_c, gates, new, win, *([cache_sel] * n_pg), bias_new, bias_win, bias_last, bias_far)
    return o.reshape(B, G * R, tn, d).transpose(0, 2, 1, 3).reshape(B, tn, G * R * d)


def _rmsnorm(x, g):
    xf = x.astype(jnp.float32)
    y = xf * lax.rsqrt(jnp.mean(xf * xf, axis=-1, keepdims=True) + NORM_EPS)
    return (y * g.astype(jnp.float32)).astype(x.dtype)


def _rel_bucket(dist):
    n = jnp.maximum(dist, 0)
    nf = jnp.maximum(n, 1).astype(jnp.float32)
    large = RPB_MAX_EXACT + (jnp.log(nf / RPB_MAX_EXACT) / math.log(RPB_MAX_DIST / RPB_MAX_EXACT)
                             * (N_BUCKETS - RPB_MAX_EXACT)).astype(jnp.int32)
    return jnp.where(n < RPB_MAX_EXACT, n, jnp.minimum(large, N_BUCKETS - 1))


def _gla_scan(q, k, v, log_a, s0):
    B, T = q.shape[:2]
    C = min(GLA_CHUNK, T)
    n_chunks = -(-T // C)
    pad = n_chunks * C - T

    def prep(a):
        a = jnp.pad(a, ((0, 0), (0, pad), (0, 0), (0, 0)))
        return a.reshape(B, n_chunks, C, a.shape[2], a.shape[3]).transpose(1, 0, 3, 2, 4)

    xs = (prep(q), prep(k), prep(v), prep(log_a.astype(jnp.float32)))
    tri = jnp.tril(jnp.ones((C, C), dtype=bool))

    def step(S, inp):
        qb, kb, vb, ab = inp
        qf, kf, vf = qb.astype(jnp.float32), kb.astype(jnp.float32), vb.astype(jnp.float32)
        b = jnp.cumsum(ab, axis=2)
        o_inter = jnp.einsum('bhcd,bhde->bhce', qf * jnp.exp(b), S)
        rel = jnp.where(tri[None, None, :, :, None], b[:, :, :, None, :] - b[:, :, None, :, :], -jnp.inf)
        att = jnp.einsum('bhtd,bhsd,bhtsd->bhts', qf, kf, jnp.exp(rel))
        o = o_inter + jnp.einsum('bhts,bhse->bhte', att, vf)
        b_last = b[:, :, -1:, :]
        S = jnp.exp(b_last[:, :, 0, :])[..., None] * S + jnp.einsum('bhsd,bhse->bhde', kf * jnp.exp(b_last - b), vf)
        return S, o

    S, o = lax.scan(step, s0.astype(jnp.float32), xs)
    o = o.transpose(1, 0, 3, 2, 4).reshape(B, n_chunks * C, GLA_HEADS, GLA_DV)[:, :T]
    return o.astype(q.dtype), S.astype(s0.dtype)


def _gla_mixer(q, k, v, a_low, r, s0, w_decay, b_decay, out_norm):
    B, T = q.shape[:2]
    q = q.reshape(B, T, GLA_HEADS, GLA_DK) * (GLA_DK ** -0.5)
    k = k.reshape(B, T, GLA_HEADS, GLA_DK)
    v = v.reshape(B, T, GLA_HEADS, GLA_DV)
    log_a = (jax.nn.log_sigmoid((a_low @ w_decay + b_decay).astype(jnp.float32)) / GLA_TAU).reshape(B, T, GLA_HEADS, GLA_DK)
    o, state = _gla_scan(q, k, v, log_a, s0)
    o = _rmsnorm(o, out_norm).reshape(B, T, GLA_VAL) * jax.nn.silu(r)
    return o, state


def _nsa_compress(rows, pos_emb, w1, w2):
    B, S = rows.shape[:2]
    n_sub = S // CMP_STRIDE
    sub = rows[:, :n_sub * CMP_STRIDE].reshape(B, n_sub, CMP_STRIDE, NSA_KV_HEADS, NSA_D)
    first = jnp.einsum('bnlgd,lde->bnge', sub + pos_emb[None, None, :CMP_STRIDE, None, :], w1[:CMP_STRIDE])
    second = jnp.einsum('bnlgd,lde->bnge', sub + pos_emb[None, None, CMP_STRIDE:, None, :], w1[CMP_STRIDE:])
    hid = jax.nn.silu(first[:, :-1] + second[:, 1:])
    return jnp.einsum('bnge,ed->bngd', hid, w2)


def _sel_blocks(rows):
    B, S = rows.shape[:2]
    ns = -(-S // SEL_BLOCK)
    rows = jnp.pad(rows, ((0, 0), (0, ns * SEL_BLOCK - S), (0, 0), (0, 0)))
    return rows.reshape(B, ns, SEL_BLOCK, NSA_KV_HEADS, NSA_D).transpose(0, 3, 1, 2, 4)


def _cmp_sel_map(nc, ns):
    c_start = jnp.arange(nc) * CMP_STRIDE
    s_start = jnp.arange(ns) * SEL_BLOCK
    hit = (c_start[:, None] < s_start[None, :] + SEL_BLOCK) & (c_start[:, None] + CMP_LEN > s_start[None, :])
    return hit.astype(jnp.float32)


def _nsa_block(q, gates, q_pos, kc, vc, c_end, ksb, vsb, wk, wv, w_pos, rpb):
    B, Q, G, R, d = q.shape
    scale = d ** -0.5
    table = rpb.astype(jnp.float32).reshape(N_BUCKETS, G, R)
    s = jnp.einsum('bqgrd,bngd->bqgrn', q, kc).astype(jnp.float32) * scale
    s = s + table[_rel_bucket(q_pos[:, None] - c_end[None, :])].transpose(0, 2, 3, 1)[None]
    m = (c_end[None, :] <= q_pos[:, None])[None, :, None, None, :]
    p_c = jax.nn.softmax(jnp.where(m, s, NEG), axis=-1) * m
    o_c = jnp.einsum('bqgrn,bngd->bqgrd', p_c.astype(vc.dtype), vc)
    ns = ksb.shape[2]
    imp = jnp.einsum('bqgn,nj->bqgj', p_c.sum(axis=3), _cmp_sel_map(kc.shape[1], ns))
    j = jnp.arange(ns)[None, :]
    cur = (q_pos // SEL_BLOCK)[:, None]
    valid = (j * SEL_BLOCK <= q_pos[:, None])[None, :, None, :]
    forced = ((j == 0) | (j == cur) | (j == cur - 1))[None, :, None, :]
    score = jnp.where(valid, jnp.where(forced, FORCE, imp), NEG)
    _, idx = lax.top_k(score, min(N_SEL, ns))
    idx = idx.transpose(0, 2, 1, 3)
    bi = jnp.arange(B)[:, None, None, None]
    gi = jnp.arange(G)[None, :, None, None]
    kg = ksb[bi, gi, idx]
    vg = vsb[bi, gi, idx]
    k_pos = idx[..., None] * SEL_BLOCK + jnp.arange(SEL_BLOCK)
    s = jnp.einsum('bqgrd,bgqnld->bgqrnl', q, kg).astype(jnp.float32) * scale
    bias = table.transpose(1, 0, 2)[gi[..., None], _rel_bucket(q_pos[None, None, :, None, None] - k_pos)]
    s = s + bias.transpose(0, 1, 2, 5, 3, 4)
    m = (k_pos <= q_pos[None, None, :, None, None])[:, :, :, None]
    shp = s.shape
    p = jax.nn.softmax(jnp.where(m, s, NEG).reshape(shp[0], shp[1], shp[2], shp[3], -1), axis=-1).reshape(shp)
    o_s = jnp.einsum('bgqrnl,bgqnld->bqgrd', p.astype(vg.dtype), vg)
    s = jnp.einsum('bqgrd,bsgd->bqgrs', q, wk).astype(jnp.float32) * scale
    dist = q_pos[:, None] - w_pos[None, :]
    s = s + table[_rel_bucket(dist)].transpose(0, 2, 3, 1)[None]
    m = ((dist >= 0) & (dist < WINDOW) & (w_pos[None, :] >= 0))[None, :, None, None, :]
    p = jax.nn.softmax(jnp.where(m, s, NEG), axis=-1)
    o_w = jnp.einsum('bqgrs,bsgd->bqgrd', p.astype(wv.dtype), wv)
    g = jax.nn.sigmoid(gates.astype(jnp.float32))
    o = g[..., 0:1] * o_c + g[..., 1:2] * o_s + g[..., 2:3] * o_w
    return o.reshape(B, Q, G * R * d).astype(q.dtype)


def _sb_block(q, q_pos, k, v, k_pos):
    z = jnp.einsum('bqhd,bshd->bhqs', q, k).astype(jnp.float32) * (SB_D ** -0.5)
    m = (k_pos[None, :] < q_pos[:, None])[None, None]
    log_fail = jnp.where(m, jax.nn.log_sigmoid(-z), 0.0)
    later = lax.cumsum(log_fail, axis=3, reverse=True) - log_fail
    a = jnp.where(m, jnp.exp(jax.nn.log_sigmoid(z) + later), 0.0)
    return jnp.einsum('bhqs,bshd->bqhd', a.astype(v.dtype), v)


def _split(z):
    cuts = [int(c) for c in np.cumsum(IN_SPLITS)[:-1]]
    return jnp.split(z[..., :N_IN], cuts, axis=-1)


def _mix_prompt(z, gla_w_decay, gla_b_decay, gla_out_norm, cmp_pos, cmp_w1, cmp_w2, rpb):
    B, T = z.shape[:2]
    gq, gk, gv, ga, gr, nq, nkv, ng, sq, skv, mg = _split(z)
    s0 = jnp.zeros((B, GLA_HEADS, GLA_DK, GLA_DV), jnp.float32)
    o_gla, gla_state = gla_mixer(gq, gk, gv, ga, gr, s0, gla_w_decay, gla_b_decay, gla_out_norm)
    nkv = nkv.reshape(B, T, 6, NSA_KV_HEADS, NSA_D)
    o_nsa = nsa_prompt_t(nq, ng, nkv, cmp_pos, cmp_w1, cmp_w2, rpb)
    skv = skv.reshape(B, T, 2, SB_HEADS, SB_D)
    o_sb = sb_prompt(sq, skv)
    keep = min(WINDOW, T)
    return (o_gla, o_nsa, o_sb, mg), nkv[:, :, :4], skv, nkv[:, T - keep:, 4:], gla_state


def _mix_sample(z, cache_nsa, cache_sb, page_table, layer, win_buf, s0, gla_w_decay, gla_b_decay,
                gla_out_norm, cmp_pos, cmp_w1, cmp_w2, rpb):
    B, T = z.shape[:2]
    gq, gk, gv, ga, gr, nq, nkv, ng, sq, skv, mg = _split(z)
    o_gla, gla_state = gla_mixer(gq, gk, gv, ga, gr, s0, gla_w_decay, gla_b_decay, gla_out_norm)
    nkv = nkv.reshape(B, T, 6, NSA_KV_HEADS, NSA_D)
    o_nsa = nsa_sample(nq, ng, nkv, cache_nsa, win_buf, page_table, layer, cmp_pos, cmp_w1, cmp_w2, rpb)
    win = jnp.concatenate([win_buf, nkv[:, :, 4:]], axis=1)
    wb = win_buf.shape[1]
    skv = skv.reshape(B, T, 2, SB_HEADS, SB_D)
    o_sb = sb_sample(sq, skv, cache_sb, page_table, layer)
    keep = min(WINDOW, wb + T)
    return (o_gla, o_nsa, o_sb, mg), nkv[:, :, :4], skv, win[:, wb + T - keep:], gla_state


def kernel(x_prompt, x_sample, cache_nsa, cache_sb, cache_win, state_gla, page_table, ffn1_norm, ffn1_w_gate, ffn1_w_up, ffn1_w_down, mix_norm, w_in, gla_w_decay, gla_b_decay, gla_out_norm, nsa_cmp_pos, nsa_cmp_w1, nsa_cmp_w2, rpb_table, w_branch_gla, w_branch_nsa, w_branch_sb, w_out, ffn2_norm, ffn2_w_gate, ffn2_w_up, ffn2_w_down, final_norm):
    bp, tp, _ = x_prompt.shape
    bs, ts, _ = x_sample.shape
    mp, ms = bp * tp, bs * ts
    dec_batch, n_pages = page_table.shape
    past_len = n_pages * cache_nsa.shape[2]
    x = jnp.concatenate([x_prompt.reshape(mp, D_MODEL), x_sample.reshape(ms, D_MODEL)], axis=0)
    outs = [[] for _ in range(8)]
    for l in range(DEPTH):
        x = ffn_half_step(x, ffn1_norm[l], ffn1_w_gate[l].astype(BF16), ffn1_w_up[l].astype(BF16),
                          ffn1_w_down[l].astype(BF16))
        w_in_l = jnp.pad(w_in[l].astype(BF16), ((0, 0), (0, N_IN_PAD - N_IN)))
        z = in_projection(x, mix_norm[l], w_in_l)
        mix_w = (gla_w_decay[l], gla_b_decay[l], gla_out_norm[l], nsa_cmp_pos[l], nsa_cmp_w1[l], nsa_cmp_w2[l],
                 rpb_table)
        br_p, r_nsa, r_sb, r_win, r_gla = _mix_prompt(z[:mp].reshape(bp, tp, N_IN_PAD), *mix_w)
        for lst, val in zip(outs[0::2], (r_nsa, r_sb, r_win, r_gla)):
            lst.append(val)
        br_s, r_nsa, r_sb, r_win, r_gla = _mix_sample(z[mp:].reshape(bs, ts, N_IN_PAD), cache_nsa, cache_sb,
                                                      page_table, l, cache_win[:, l], state_gla[:, l], *mix_w)
        for lst, val in zip(outs[1::2], (r_nsa, r_sb, r_win, r_gla)):
            lst.append(val)
        o_gla, o_nsa, o_sb, gates = (
            jnp.concatenate([a.reshape(mp, a.shape[-1]), b.reshape(ms, b.shape[-1])], axis=0)
            for a, b in zip(br_p, br_s))
        y = merge_branches(o_gla.astype(BF16), o_nsa.astype(BF16), o_sb.astype(BF16), gates,
                           w_branch_gla[l].astype(BF16), w_branch_nsa[l].astype(BF16), w_branch_sb[l].astype(BF16))
        x = out_projection(x, y, w_out[l].astype(BF16))
        x = ffn_half_step(x, ffn2_norm[l], ffn2_w_gate[l].astype(BF16), ffn2_w_up[l].astype(BF16),
                          ffn2_w_down[l].astype(BF16))
    y = final_rmsnorm(x, final_norm)
    return (y[:mp].reshape(bp, tp, D_MODEL), y[mp:].reshape(bs, ts, D_MODEL)) + tuple(
        jnp.stack(o, axis=1) for o in outs)
```

```python
import functools
import math

import jax
import jax.numpy as jnp
import numpy as np
from jax import lax
from jax.experimental import pallas as pl
from jax.experimental.pallas import tpu as pltpu

D_MODEL = 2048
DEPTH = 2
D_FF = 5632
NORM_EPS = 1e-6
GLA_HEADS = 4
GLA_DK = 128
GLA_DV = 256
GLA_RANK = 16
GLA_TAU = 16.0
GLA_CHUNK = 64
GLA_KEY = GLA_HEADS * GLA_DK
GLA_VAL = GLA_HEADS * GLA_DV
NSA_HEADS = 16
NSA_KV_HEADS = 4
NSA_GROUP = NSA_HEADS // NSA_KV_HEADS
NSA_D = 64
NSA_OUT = NSA_HEADS * NSA_D
NSA_KVW = NSA_KV_HEADS * NSA_D
CMP_STRIDE = 16
CMP_LEN = 2 * CMP_STRIDE
CMP_HID = 64
SEL_BLOCK = 64
N_SEL = 16
WINDOW = 512
NSA_QBLK = 32
SB_HEADS = 16
SB_D = 64
SB_OUT = SB_HEADS * SB_D
SB_QBLK = 128
N_BUCKETS = 32
RPB_MAX_EXACT = 16
RPB_MAX_DIST = 128
NEG = -1e30
FORCE = 1e9
IN_SPLITS = (GLA_KEY, GLA_KEY, GLA_VAL, GLA_RANK, GLA_VAL, NSA_OUT, 6 * NSA_KVW, 3 * NSA_HEADS, SB_OUT, 2 * SB_OUT,
             3 * D_MODEL)
N_IN = sum(IN_SPLITS)

BF16 = jnp.bfloat16
F32 = jnp.float32

VMEM_LIMIT_BYTES = 56 * 1024 * 1024
LANES = 128
ROW_TILE = 640
FF_TILE = 512
PROJ_TILE = 1152
N_IN_PAD = 14976
OUT_TILE = 512
Z_GQ, Z_GK, Z_GV, Z_GR = 0, GLA_KEY, 2 * GLA_KEY, 2 * GLA_KEY + GLA_VAL
Z_NQ = Z_GR + GLA_VAL
Z_NKV = Z_NQ + NSA_OUT
Z_SQ = Z_NKV + 6 * NSA_KVW
Z_SKV = Z_SQ + SB_OUT
Z_MG = Z_SKV + 2 * SB_OUT
Z_NG = Z_MG + 3 * D_MODEL
Z_GA = Z_NG + 3 * NSA_HEADS
assert Z_GA + GLA_RANK == N_IN and Z_NG % LANES == 0 and Z_MG % OUT_TILE == 0


def _rms_bf16(x, g):
    y = x * lax.rsqrt(jnp.mean(x * x, axis=-1, keepdims=True) + NORM_EPS)
    return (y * g).astype(BF16)


def _ffn_kernel(x_ref, g_ref, wg_ref, wu_ref, wd_ref, o_ref, h_ref, acc_ref):
    f = pl.program_id(1)

    @pl.when(f == 0)
    def _():
        h_ref[...] = _rms_bf16(x_ref[...], g_ref[...])
        acc_ref[...] = jnp.zeros_like(acc_ref)

    h = h_ref[...]
    a = jnp.dot(h, wg_ref[...], preferred_element_type=F32)
    b = jnp.dot(h, wu_ref[...], preferred_element_type=F32)
    act = (a * jax.nn.sigmoid(a) * b).astype(BF16)
    acc_ref[...] += jnp.dot(act, wd_ref[...], preferred_element_type=F32)

    @pl.when(f == pl.num_programs(1) - 1)
    def _():
        o_ref[...] = x_ref[...] + 0.5 * acc_ref[...]


def ffn_half_step(x, g, wg, wu, wd):
    m = x.shape[0]
    return pl.pallas_call(
        _ffn_kernel,
        grid=(m // ROW_TILE, D_FF // FF_TILE),
        in_specs=[
            pl.BlockSpec((ROW_TILE, D_MODEL), lambda i, f: (i, 0)),
            pl.BlockSpec((1, D_MODEL), lambda i, f: (0, 0)),
            pl.BlockSpec((D_MODEL, FF_TILE), lambda i, f: (0, f)),
            pl.BlockSpec((D_MODEL, FF_TILE), lambda i, f: (0, f)),
            pl.BlockSpec((FF_TILE, D_MODEL), lambda i, f: (f, 0)),
        ],
        out_specs=pl.BlockSpec((ROW_TILE, D_MODEL), lambda i, f: (i, 0)),
        out_shape=jax.ShapeDtypeStruct((m, D_MODEL), F32),
        scratch_shapes=[pltpu.VMEM((ROW_TILE, D_MODEL), BF16), pltpu.VMEM((ROW_TILE, D_MODEL), F32)],
        compiler_params=pltpu.CompilerParams(dimension_semantics=("parallel", "arbitrary"),
                                             vmem_limit_bytes=VMEM_LIMIT_BYTES),
        name="ffn_half_step",
    )(x, g.reshape(1, D_MODEL), wg, wu, wd)


def _proj_kernel(x_ref, g_ref, w_ref, o_ref, h_ref):
    @pl.when(pl.program_id(1) == 0)
    def _():
        h_ref[...] = _rms_bf16(x_ref[...], g_ref[...])

    o_ref[...] = jnp.dot(h_ref[...], w_ref[...], preferred_element_type=F32)


def in_projection(x, g, w):
    m = x.shape[0]
    return pl.pallas_call(
        _proj_kernel,
        grid=(m // ROW_TILE, N_IN_PAD // PROJ_TILE),
        in_specs=[
            pl.BlockSpec((ROW_TILE, D_MODEL), lambda i, n: (i, 0)),
            pl.BlockSpec((1, D_MODEL), lambda i, n: (0, 0)),
            pl.BlockSpec((D_MODEL, PROJ_TILE), lambda i, n: (0, n)),
        ],
        out_specs=pl.BlockSpec((ROW_TILE, PROJ_TILE), lambda i, n: (i, n)),
        out_shape=jax.ShapeDtypeStruct((m, N_IN_PAD), F32),
        scratch_shapes=[pltpu.VMEM((ROW_TILE, D_MODEL), BF16)],
        compiler_params=pltpu.CompilerParams(dimension_semantics=("parallel", "arbitrary"),
                                             vmem_limit_bytes=VMEM_LIMIT_BYTES),
        name="in_projection",
    )(x, g.reshape(1, D_MODEL), w)


def _merge_kernel(oa_ref, ob_ref, oc_ref, ga_ref, gb_ref, gc_ref, wa_ref, wb_ref, wc_ref, y_ref):
    ya = jnp.dot(oa_ref[...], wa_ref[...], preferred_element_type=F32)
    yb = jnp.dot(ob_ref[...], wb_ref[...], preferred_element_type=F32)
    yc = jnp.dot(oc_ref[...], wc_ref[...], preferred_element_type=F32)
    y = jax.nn.sigmoid(ga_ref[...]) * ya + jax.nn.sigmoid(gb_ref[...]) * yb + jax.nn.sigmoid(gc_ref[...]) * yc
    y_ref[...] = y.astype(BF16)


def merge_branches(o_gla, o_nsa, o_sb, gates, wa, wb, wc):
    m = o_gla.shape[0]
    nb = D_MODEL // OUT_TILE
    o_spec = lambda width: pl.BlockSpec((ROW_TILE, width), lambda i, n: (i, 0))
    g_spec = lambda k: pl.BlockSpec((ROW_TILE, OUT_TILE), lambda i, n, k=k: (i, Z_MG // OUT_TILE + n + k * nb))
    w_spec = lambda width: pl.BlockSpec((width, OUT_TILE), lambda i, n: (0, n))
    return pl.pallas_call(
        _merge_kernel,
        grid=(m // ROW_TILE, nb),
        in_specs=[o_spec(GLA_VAL), o_spec(NSA_OUT), o_spec(SB_OUT), g_spec(0), g_spec(1), g_spec(2),
                  w_spec(GLA_VAL), w_spec(NSA_OUT), w_spec(SB_OUT)],
        out_specs=pl.BlockSpec((ROW_TILE, OUT_TILE), lambda i, n: (i, n)),
        out_shape=jax.ShapeDtypeStruct((m, D_MODEL), BF16),
        compiler_params=pltpu.CompilerParams(dimension_semantics=("parallel", "arbitrary"),
                                             vmem_limit_bytes=VMEM_LIMIT_BYTES),
        name="merge_branches",
    )(o_gla, o_nsa, o_sb, gates, gates, gates, wa, wb, wc)


def _out_kernel(x_ref, y_ref, w_ref, o_ref):
    o_ref[...] = x_ref[...] + jnp.dot(y_ref[...], w_ref[...], preferred_element_type=F32)


def out_projection(x, y, w):
    m = x.shape[0]
    return pl.pallas_call(
        _out_kernel,
        grid=(m // ROW_TILE, D_MODEL // OUT_TILE),
        in_specs=[
            pl.BlockSpec((ROW_TILE, OUT_TILE), lambda i, n: (i, n)),
            pl.BlockSpec((ROW_TILE, D_MODEL), lambda i, n: (i, 0)),
            pl.BlockSpec((D_MODEL, OUT_TILE), lambda i, n: (0, n)),
        ],
        out_specs=pl.BlockSpec((ROW_TILE, OUT_TILE), lambda i, n: (i, n)),
        out_shape=jax.ShapeDtypeStruct((m, D_MODEL), F32),
        compiler_params=pltpu.CompilerParams(dimension_semantics=("parallel", "arbitrary"),
                                             vmem_limit_bytes=VMEM_LIMIT_BYTES),
        name="out_projection",
    )(x, y, w)


def _norm_kernel(x_ref, g_ref, o_ref):
    x = x_ref[...]
    o_ref[...] = x * lax.rsqrt(jnp.mean(x * x, axis=-1, keepdims=True) + NORM_EPS) * g_ref[...]


def final_rmsnorm(x, g):
    m = x.shape[0]
    return pl.pallas_call(
        _norm_kernel,
        grid=(m // ROW_TILE,),
        in_specs=[pl.BlockSpec((ROW_TILE, D_MODEL), lambda i: (i, 0)), pl.BlockSpec((1, D_MODEL), lambda i: (0, 0))],
        out_specs=pl.BlockSpec((ROW_TILE, D_MODEL), lambda i: (i, 0)),
        out_shape=jax.ShapeDtypeStruct((m, D_MODEL), F32),
        compiler_params=pltpu.CompilerParams(dimension_semantics=("parallel",), vmem_limit_bytes=VMEM_LIMIT_BYTES),
        name="final_rmsnorm",
    )(x, g.reshape(1, D_MODEL))


NSA_TQ = 128
NSA_TK = 256
SEL_TOEP = 4
WIN_TOEP = 6


def _toeplitz_bias(rpb, window):
    n = WIN_TOEP if window else SEL_TOEP
    u = jnp.arange(NSA_TQ)[:, None]
    w = jnp.arange(NSA_TK)[None, :]
    d = (jnp.arange(n) * NSA_TQ)[:, None, None] + (u - w)[None]
    b = rpb.astype(F32)[_rel_bucket(d)]
    ok = (d >= 0) & (d < WINDOW) if window else (d >= 0)
    b = jnp.where(ok[..., None], b, NEG)
    return b.transpose(3, 0, 1, 2).reshape(NSA_KV_HEADS, NSA_GROUP, n, NSA_TQ, NSA_TK)


def _split_hi_lo(x):
    hi = x.astype(BF16)
    lo = (x - hi.astype(F32)).astype(BF16)
    return hi, lo


def _nsa_prompt_kernel(q_ref, gate_ref, cmp_ref, kv_ref, posa_ref, posb_ref, w1a_ref, w1b_ref, w2_ref,
                       biasc_ref, stoep_ref, wtoep_ref, mapt_ref, expand_ref, o_ref,
                       kc_ref, vc_ref, m_ref, l_ref, acc_ref, *, n_cmp):
    qi = pl.program_id(2)
    tq, tk, R = NSA_TQ, NSA_TK, NSA_GROUP
    rows = R * tq

    @pl.when(qi == 0)
    def _():
        for t, dst in ((0, kc_ref), (1, vc_ref)):
            x = cmp_ref[0, 0, t]
            first = jnp.dot((x + posa_ref[t]).astype(BF16), w1a_ref[t], preferred_element_type=F32)
            second = jnp.dot((x + posb_ref[t]).astype(BF16), w1b_ref[t], preferred_element_type=F32)
            pre = first + pltpu.roll(second, second.shape[0] - 1, 0)
            hid = (pre * jax.nn.sigmoid(pre)).astype(BF16)
            dst[...] = jnp.dot(hid, w2_ref[t], preferred_element_type=F32).astype(BF16)

    q4 = q_ref[0, 0].reshape(rows, NSA_D)
    q_pos = qi * tq + lax.broadcasted_iota(jnp.int32, (tq, 1), 0)

    ncp = kc_ref.shape[0]
    s = lax.dot_general(q4, kc_ref[...], (((1,), (1,)), ((), ())), preferred_element_type=F32)
    s = s.reshape(R, tq, ncp) + biasc_ref[0]
    n_idx = lax.broadcasted_iota(jnp.int32, (tq, ncp), 1)
    ok = ((n_idx * CMP_STRIDE + (CMP_LEN - 1)) <= q_pos) & (n_idx < n_cmp)
    s = jnp.where(ok[None], s, NEG)
    e = jnp.exp(s - jnp.max(s, axis=-1, keepdims=True))
    p_c = jnp.where(ok[None], e / jnp.sum(e, axis=-1, keepdims=True), 0.0)
    o_c = jnp.dot(p_c.reshape(rows, ncp).astype(BF16), vc_ref[...], preferred_element_type=F32)
    psum = jnp.sum(p_c, axis=0)
    p_hi, p_lo = _split_hi_lo(psum)
    nt = (((1,), (1,)), ((), ()))
    imp = (lax.dot_general(mapt_ref[...], p_hi, nt, preferred_element_type=F32)
           + lax.dot_general(mapt_ref[...], p_lo, nt, preferred_element_type=F32))

    ns = imp.shape[0]
    j_idx = lax.broadcasted_iota(jnp.int32, (ns, tq), 0)
    t_pos = qi * tq + lax.broadcasted_iota(jnp.int32, (ns, tq), 1)
    cur = t_pos // SEL_BLOCK
    valid = j_idx * SEL_BLOCK <= t_pos
    forced = (j_idx == 0) | (j_idx == cur) | (j_idx == cur - 1)
    score = jnp.where(valid, jnp.where(forced, FORCE, imp), NEG)
    rank = jnp.zeros((ns, tq), F32)
    for i in range(ns):
        row = score[i:i + 1, :]
        beats = (row > score) | ((row == score) & (j_idx > i))
        rank = rank + jnp.where(beats, 1.0, 0.0)
    sel_t = jnp.where((rank < float(min(N_SEL, ns))) & valid, 1.0, 0.0)
    sel = sel_t.T.astype(BF16)

    def attend(k_idx, v_idx, toep_ref, n_toep, j_lo, use_sel):
        m_ref[...] = jnp.full(m_ref.shape, NEG, F32)
        l_ref[...] = jnp.zeros(l_ref.shape, F32)
        acc_ref[...] = jnp.zeros(acc_ref.shape, F32)
        j_hi = qi // 2

        def body(step, carry):
            j = j_hi - step
            k = kv_ref[0, 0, k_idx, pl.ds(pl.multiple_of(j * tk, tk), tk), :]
            v = kv_ref[0, 0, v_idx, pl.ds(pl.multiple_of(j * tk, tk), tk), :]
            sc = lax.dot_general(q4, k, (((1,), (1,)), ((), ())), preferred_element_type=F32)
            off = jnp.minimum(qi - 2 * j, n_toep - 1)
            sc = sc.reshape(R, tq, tk) + toep_ref[0, :, off]
            if use_sel:
                hit = jnp.dot(sel, expand_ref[j], preferred_element_type=F32)
                sc = jnp.where((hit > 0.5)[None], sc, NEG)
            sc = sc.reshape(rows, tk)
            m_old = m_ref[...]
            m_new = jnp.maximum(m_old, jnp.max(sc, axis=-1, keepdims=True))
            alpha = jnp.exp(m_old - m_new)
            p = jnp.exp(sc - m_new)
            l_ref[...] = alpha * l_ref[...] + jnp.sum(p, axis=-1, keepdims=True)
            acc_ref[...] = alpha * acc_ref[...] + jnp.dot(p.astype(BF16), v, preferred_element_type=F32)
            m_ref[...] = m_new
            return carry

        lax.fori_loop(0, j_hi - j_lo + 1, body, 0)
        return acc_ref[...] / l_ref[...]

    o_s = attend(0, 1, stoep_ref, SEL_TOEP, 0, True)
    o_w = attend(2, 3, wtoep_ref, WIN_TOEP, jnp.maximum(qi * tq - (WINDOW - 1), 0) // tk, False)

    g = jax.nn.sigmoid(gate_ref[0, 0])
    for r in range(R):
        sl = slice(r * tq, (r + 1) * tq)
        o = (g[:, 3 * r:3 * r + 1] * o_c[sl] + g[:, 3 * r + 1:3 * r + 2] * o_s[sl]
             + g[:, 3 * r + 2:3 * r + 3] * o_w[sl])
        o_ref[0, 0, r] = o.astype(o_ref.dtype)


def nsa_prompt(nq, ng, nkv, cmp_pos, cmp_w1, cmp_w2, rpb):
    B, T = nq.shape[:2]
    G, R, d = NSA_KV_HEADS, NSA_GROUP, NSA_D
    n_sub = T // CMP_STRIDE
    n_cmp = n_sub - 1
    ns = T // SEL_BLOCK
    nq_t = T // NSA_TQ
    nk_t = T // NSA_TK
    q = (nq * (d ** -0.5)).astype(BF16).reshape(B, T, G, R, d).transpose(0, 2, 3, 1, 4)
    gates = ng.reshape(B, T, G, 3 * R).transpose(0, 2, 1, 3)
    cmp_rows = nkv[:, :, 0:2].transpose(0, 3, 2, 1, 4).reshape(B, G, 2, n_sub, CMP_STRIDE * d)
    kv = nkv[:, :, 2:6].astype(BF16).transpose(0, 3, 2, 1, 4)
    pos_a = cmp_pos[:, :CMP_STRIDE].reshape(2, 1, CMP_STRIDE * d)
    pos_b = cmp_pos[:, CMP_STRIDE:].reshape(2, 1, CMP_STRIDE * d)
    w1a = cmp_w1[:, :CMP_STRIDE].reshape(2, CMP_STRIDE * d, CMP_HID).astype(BF16)
    w1b = cmp_w1[:, CMP_STRIDE:].reshape(2, CMP_STRIDE * d, CMP_HID).astype(BF16)
    w2 = cmp_w2.astype(BF16)
    t_pos = jnp.arange(T)[:, None]
    c_end = jnp.arange(n_sub)[None, :] * CMP_STRIDE + (CMP_LEN - 1)
    bias_c = rpb.astype(F32)[_rel_bucket(t_pos - c_end)].transpose(2, 0, 1).reshape(G, R, T, n_sub)
    stoep = _toeplitz_bias(rpb, False)
    wtoep = _toeplitz_bias(rpb, True)
    map_t = _cmp_sel_map(n_sub, ns).T.astype(BF16)
    expand = (jnp.arange(ns)[None, :, None]
              == ((jnp.arange(nk_t)[:, None, None] * NSA_TK + jnp.arange(NSA_TK)[None, None, :]) // SEL_BLOCK)
              ).astype(BF16)
    full = lambda a: pl.BlockSpec(a.shape, lambda b, g, i: (0,) * a.ndim)
    per_g = lambda a: pl.BlockSpec((1,) + a.shape[1:], lambda b, g, i: (g,) + (0,) * (a.ndim - 1))
    o = pl.pallas_call(
        functools.partial(_nsa_prompt_kernel, n_cmp=n_cmp),
        grid=(B, G, nq_t),
        in_specs=[
            pl.BlockSpec((1, 1, R, NSA_TQ, d), lambda b, g, i: (b, g, 0, i, 0)),
            pl.BlockSpec((1, 1, NSA_TQ, 3 * R), lambda b, g, i: (b, g, i, 0)),
            pl.BlockSpec((1, 1, 2, n_sub, CMP_STRIDE * d), lambda b, g, i: (b, g, 0, 0, 0)),
            pl.BlockSpec((1, 1, 4, T, d), lambda b, g, i: (b, g, 0, 0, 0)),
            full(pos_a), full(pos_b), full(w1a), full(w1b), full(w2),
            pl.BlockSpec((1, R, NSA_TQ, n_sub), lambda b, g, i: (g, 0, i, 0)),
            per_g(stoep), per_g(wtoep), full(map_t), full(expand),
        ],
        out_specs=pl.BlockSpec((1, 1, R, NSA_TQ, d), lambda b, g, i: (b, g, 0, i, 0)),
        out_shape=jax.ShapeDtypeStruct((B, G, R, T, d), BF16),
        scratch_shapes=[pltpu.VMEM((n_sub, d), BF16), pltpu.VMEM((n_sub, d), BF16),
                        pltpu.VMEM((R * NSA_TQ, 1), F32), pltpu.VMEM((R * NSA_TQ, 1), F32),
                        pltpu.VMEM((R * NSA_TQ, d), F32)],
        compiler_params=pltpu.CompilerParams(dimension_semantics=("parallel", "parallel", "arbitrary"),
                                             vmem_limit_bytes=VMEM_LIMIT_BYTES),
        name="nsa_prompt",
    )(q, gates, cmp_rows, kv, pos_a, pos_b, w1a, w1b, w2, bias_c, stoep, wtoep, map_t, expand)
    return o.transpose(0, 3, 1, 2, 4).reshape(B, T, G * R * d)


def _bias_lookup(rpb, bucket):
    one_hot = jax.nn.one_hot(bucket, N_BUCKETS, dtype=F32)
    return jnp.einsum('...k,kh->...h', one_hot, rpb.astype(F32), precision=lax.Precision.HIGHEST)


def _toeplitz_bias_t(rpb, window):
    n = WIN_TOEP if window else SEL_TOEP
    w = jnp.arange(NSA_TK)[:, None]
    u = jnp.arange(NSA_TQ)[None, :]
    d = (jnp.arange(n) * NSA_TQ)[:, None, None] + (u - w)[None]
    b = _bias_lookup(rpb, _rel_bucket(d))
    ok = (d >= 0) & (d < WINDOW) if window else (d >= 0)
    b = jnp.where(ok[..., None], b, NEG).reshape(n, NSA_TK, NSA_TQ, NSA_KV_HEADS, NSA_GROUP)
    return b.transpose(3, 0, 1, 4, 2).reshape(NSA_KV_HEADS, n, NSA_TK, NSA_GROUP * NSA_TQ)


def _nsa_prompt_t_kernel(q_ref, gate_ref, cmp_ref, k_ref, vt_ref, posa_ref, posb_ref, w1a_ref, w1b_ref, w2_ref,
                         biasc_ref, stoep_ref, wtoep_ref, mapt_ref, expand_ref, o_ref,
                         kc_ref, vct_ref, acc_ref, *, n_cmp):
    qi = pl.program_id(2)
    tq, tk, R = NSA_TQ, NSA_TK, NSA_GROUP
    cols = R * tq
    nt = (((1,), (1,)), ((), ()))

    @pl.when(qi == 0)
    def _():
        for t in range(2):
            x = cmp_ref[0, 0, t]
            first = jnp.dot((x + posa_ref[t]).astype(BF16), w1a_ref[t], preferred_element_type=F32)
            second = jnp.dot((x + posb_ref[t]).astype(BF16), w1b_ref[t], preferred_element_type=F32)
            pre = first + pltpu.roll(second, second.shape[0] - 1, 0)
            hid = (pre * jax.nn.sigmoid(pre)).astype(BF16)
            c = jnp.dot(hid, w2_ref[t], preferred_element_type=F32)
            if t == 0:
                kc_ref[...] = c.astype(BF16)
            else:
                vct_ref[...] = c.T.astype(BF16)

    q4 = q_ref[0, 0].reshape(cols, NSA_D)
    lane_q = lax.broadcasted_iota(jnp.int32, (1, cols), 1) % tq
    q_pos = qi * tq + lane_q

    ncp = kc_ref.shape[0]
    s = lax.dot_general(kc_ref[...], q4, nt, preferred_element_type=F32) + biasc_ref[0, 0]
    n_idx = lax.broadcasted_iota(jnp.int32, (ncp, cols), 0)
    ok = ((n_idx * CMP_STRIDE + (CMP_LEN - 1)) <= q_pos) & (n_idx < n_cmp)
    s = jnp.where(ok, s, NEG)
    e = jnp.exp(s - jnp.max(s, axis=0, keepdims=True))
    p_c = jnp.where(ok, e / jnp.sum(e, axis=0, keepdims=True), 0.0)
    o_c = jnp.dot(vct_ref[...], p_c.astype(BF16), preferred_element_type=F32)
    psum = p_c[:, 0:tq]
    for r in range(1, R):
        psum = psum + p_c[:, r * tq:(r + 1) * tq]
    p_hi, p_lo = _split_hi_lo(psum)
    imp = (jnp.dot(mapt_ref[...], p_hi, preferred_element_type=F32)
           + jnp.dot(mapt_ref[...], p_lo, preferred_element_type=F32))

    ns = imp.shape[0]
    j_idx = lax.broadcasted_iota(jnp.int32, (ns, tq), 0)
    t_pos = qi * tq + lax.broadcasted_iota(jnp.int32, (ns, tq), 1)
    cur = t_pos // SEL_BLOCK
    valid = j_idx * SEL_BLOCK <= t_pos
    forced = (j_idx == 0) | (j_idx == cur) | (j_idx == cur - 1)
    score = jnp.where(valid, jnp.where(forced, FORCE, imp), NEG)
    rank = jnp.zeros((ns, tq), F32)
    for i in range(ns):
        row = score[i:i + 1, :]
        beats = (row > score) | ((row == score) & (j_idx > i))
        rank = rank + jnp.where(beats, 1.0, 0.0)
    sel_t = jnp.where((rank < float(min(N_SEL, ns))) & valid, 1.0, 0.0).astype(BF16)
    sel4 = jnp.concatenate([sel_t] * R, axis=1)

    def attend(idx, toep_ref, n_toep, j_lo, use_sel):
        acc_ref[...] = jnp.zeros(acc_ref.shape, F32)
        j_hi = qi // 2

        def body(step, carry):
            m_old, l_old = carry
            j = j_hi - step
            k = k_ref[0, 0, idx, pl.ds(pl.multiple_of(j * tk, tk), tk), :]
            vt = vt_ref[0, 0, idx, :, pl.ds(pl.multiple_of(j * tk, tk), tk)]
            off = jnp.minimum(qi - 2 * j, n_toep - 1)
            sc = lax.dot_general(k, q4, nt, preferred_element_type=F32) + toep_ref[0, off]
            if use_sel:
                hit = jnp.dot(expand_ref[j], sel4, preferred_element_type=F32)
                sc = jnp.where(hit > 0.5, sc, NEG)
            m_new = jnp.maximum(m_old, jnp.max(sc, axis=0, keepdims=True))
            alpha = jnp.exp(m_old - m_new)
            p = jnp.exp(sc - m_new)
            l_new = alpha * l_old + jnp.sum(p, axis=0, keepdims=True)
            acc_ref[...] = alpha * acc_ref[...] + jnp.dot(vt, p.astype(BF16), preferred_element_type=F32)
            return m_new, l_new

        init = (jnp.full((1, cols), NEG, F32), jnp.zeros((1, cols), F32))
        _, l_fin = lax.fori_loop(0, j_hi - j_lo + 1, body, init)
        return acc_ref[...] / l_fin

    o_s = attend(0, stoep_ref, SEL_TOEP, 0, True)
    o_w = attend(1, wtoep_ref, WIN_TOEP, jnp.maximum(qi * tq - (WINDOW - 1), 0) // tk, False)

    g = jax.nn.sigmoid(gate_ref[0, 0])
    for r in range(R):
        sl = slice(r * tq, (r + 1) * tq)
        o = (g[3 * r:3 * r + 1] * o_c[:, sl] + g[3 * r + 1:3 * r + 2] * o_s[:, sl]
             + g[3 * r + 2:3 * r + 3] * o_w[:, sl])
        o_ref[0, 0, r] = o.astype(o_ref.dtype)


def nsa_prompt_t(nq, ng, nkv, cmp_pos, cmp_w1, cmp_w2, rpb):
    B, T = nq.shape[:2]
    G, R, d = NSA_KV_HEADS, NSA_GROUP, NSA_D
    n_sub = T // CMP_STRIDE
    n_cmp = n_sub - 1
    ns = T // SEL_BLOCK
    nq_t = T // NSA_TQ
    nk_t = T // NSA_TK
    q = (nq * (d ** -0.5)).astype(BF16).reshape(B, T, G, R, d).transpose(0, 2, 3, 1, 4)
    gates = ng.reshape(B, T, G, 3 * R).transpose(0, 2, 3, 1)
    cmp_rows = nkv[:, :, 0:2].transpose(0, 3, 2, 1, 4).reshape(B, G, 2, n_sub, CMP_STRIDE * d)
    k = jnp.stack([nkv[:, :, 2].astype(BF16).transpose(0, 2, 1, 3),
                   nkv[:, :, 4].astype(BF16).transpose(0, 2, 1, 3)], axis=2)
    vt = jnp.stack([nkv[:, :, 3].astype(BF16).transpose(0, 2, 3, 1),
                    nkv[:, :, 5].astype(BF16).transpose(0, 2, 3, 1)], axis=2)
    pos_a = cmp_pos[:, :CMP_STRIDE].reshape(2, 1, CMP_STRIDE * d)
    pos_b = cmp_pos[:, CMP_STRIDE:].reshape(2, 1, CMP_STRIDE * d)
    w1a = cmp_w1[:, :CMP_STRIDE].reshape(2, CMP_STRIDE * d, CMP_HID).astype(BF16)
    w1b = cmp_w1[:, CMP_STRIDE:].reshape(2, CMP_STRIDE * d, CMP_HID).astype(BF16)
    w2 = cmp_w2.astype(BF16)
    t_pos = jnp.arange(T)[None, :]
    c_end = jnp.arange(n_sub)[:, None] * CMP_STRIDE + (CMP_LEN - 1)
    bias_c = _bias_lookup(rpb, _rel_bucket(t_pos - c_end))
    bias_c = bias_c.reshape(n_sub, nq_t, NSA_TQ, G, R).transpose(3, 1, 0, 4, 2).reshape(G, nq_t, n_sub, R * NSA_TQ)
    stoep = _toeplitz_bias_t(rpb, False)
    wtoep = _toeplitz_bias_t(rpb, True)
    map_t = _cmp_sel_map(n_sub, ns).T.astype(BF16)
    expand = ((jnp.arange(nk_t)[:, None, None] * NSA_TK + jnp.arange(NSA_TK)[None, :, None]) // SEL_BLOCK
              == jnp.arange(ns)[None, None, :]).astype(BF16)
    full = lambda a: pl.BlockSpec(a.shape, lambda b, g, i: (0,) * a.ndim)
    per_g = lambda a: pl.BlockSpec((1,) + a.shape[1:], lambda b, g, i: (g,) + (0,) * (a.ndim - 1))
    per_bg = lambda a: pl.BlockSpec((1, 1) + a.shape[2:], lambda b, g, i: (b, g) + (0,) * (a.ndim - 2))
    o = pl.pallas_call(
        functools.partial(_nsa_prompt_t_kernel, n_cmp=n_cmp),
        grid=(B, G, nq_t),
        in_specs=[
            pl.BlockSpec((1, 1, R, NSA_TQ, d), lambda b, g, i: (b, g, 0, i, 0)),
            pl.BlockSpec((1, 1, 3 * R, NSA_TQ), lambda b, g, i: (b, g, 0, i)),
            per_bg(cmp_rows), per_bg(k), per_bg(vt),
            full(pos_a), full(pos_b), full(w1a), full(w1b), full(w2),
            pl.BlockSpec((1, 1, n_sub, R * NSA_TQ), lambda b, g, i: (g, i, 0, 0)),
            per_g(stoep), per_g(wtoep), full(map_t), full(expand),
        ],
        out_specs=pl.BlockSpec((1, 1, R, d, NSA_TQ), lambda b, g, i: (b, g, 0, 0, i)),
        out_shape=jax.ShapeDtypeStruct((B, G, R, d, T), BF16),
        scratch_shapes=[pltpu.VMEM((n_sub, d), BF16), pltpu.VMEM((d, n_sub), BF16),
                        pltpu.VMEM((d, R * NSA_TQ), F32)],
        compiler_params=pltpu.CompilerParams(dimension_semantics=("parallel", "parallel", "arbitrary"),
                                             vmem_limit_bytes=VMEM_LIMIT_BYTES),
        name="nsa_prompt",
    )(q, gates, cmp_rows, k, vt, pos_a, pos_b, w1a, w1b, w2, bias_c, stoep, wtoep, map_t, expand)
    return o.transpose(0, 4, 1, 2, 3).reshape(B, T, G * R * d)


SB_TQ = 512
SB_TK = 256
SB_HEADS_PER_STEP = 4


def _softplus(z):
    return jnp.maximum(z, 0.0) + jnp.log(1.0 + jnp.exp(-jnp.abs(z)))


def _suffix_matrix(n):
    j = jnp.arange(n)[:, None]
    s = jnp.arange(n)[None, :]
    return jnp.concatenate([(j > s), jnp.ones((n, n), bool)], axis=1).astype(BF16)


def _sb_tile(z, ok, v, suffix, c_ref, acc_ref):
    tk = z.shape[1]
    sp = _softplus(z)
    lf = -sp if ok is None else jnp.where(ok, -sp, 0.0)
    hi, lo = _split_hi_lo(lf)
    tot = (jnp.dot(hi, suffix, preferred_element_type=F32) + jnp.dot(lo, suffix, preferred_element_type=F32))
    later = tot[:, :tk] + c_ref[...]
    a = jnp.exp(z - sp + later)
    if ok is not None:
        a = jnp.where(ok, a, 0.0)
    acc_ref[...] += jnp.dot(a.astype(BF16), v, preferred_element_type=F32)
    c_ref[...] += tot[:, tk:]


def _sb_prompt_kernel(q_ref, k_ref, v_ref, suffix_ref, o_ref, c_ref, acc_ref):
    qi = pl.program_id(2)
    tq, tk = SB_TQ, SB_TK
    per = tq // tk
    heads = q_ref.shape[1]
    suffix = suffix_ref[...]
    c_ref[...] = jnp.zeros(c_ref.shape, F32)
    acc_ref[...] = jnp.zeros(acc_ref.shape, F32)
    nt = (((1,), (1,)), ((), ()))
    q_pos = qi * tq + lax.broadcasted_iota(jnp.int32, (tq, tk), 0)
    k_off = lax.broadcasted_iota(jnp.int32, (tq, tk), 1)

    def tile(j, masked):
        ok = (j * tk + k_off < q_pos) if masked else None
        for h in range(heads):
            k = k_ref[0, h, pl.ds(pl.multiple_of(j * tk, tk), tk), :]
            v = v_ref[0, h, pl.ds(pl.multiple_of(j * tk, tk), tk), :]
            z = lax.dot_general(q_ref[0, h], k, nt, preferred_element_type=F32)
            _sb_tile(z, ok, v, suffix, c_ref.at[h], acc_ref.at[h])

    for t in range(per):
        tile(qi * per + (per - 1 - t), True)

    def body(step, carry):
        tile(qi * per - 1 - step, False)
        return carry

    lax.fori_loop(0, qi * per, body, 0)
    o_ref[0] = acc_ref[...].astype(o_ref.dtype)


def sb_prompt(sq, skv):
    B, T = sq.shape[:2]
    H, d, hp = SB_HEADS, SB_D, SB_HEADS_PER_STEP
    q = (sq * (d ** -0.5)).astype(BF16).reshape(B, T, H, d).transpose(0, 2, 1, 3)
    k = skv[:, :, 0].astype(BF16).transpose(0, 2, 1, 3)
    v = skv[:, :, 1].astype(BF16).transpose(0, 2, 1, 3)
    suffix = _suffix_matrix(SB_TK)
    kv_spec = pl.BlockSpec((1, hp, T, d), lambda b, h, i: (b, h, 0, 0))
    q_spec = pl.BlockSpec((1, hp, SB_TQ, d), lambda b, h, i: (b, h, i, 0))
    o = pl.pallas_call(
        _sb_prompt_kernel,
        grid=(B, H // hp, T // SB_TQ),
        in_specs=[q_spec, kv_spec, kv_spec, pl.BlockSpec(suffix.shape, lambda b, h, i: (0, 0))],
        out_specs=q_spec,
        out_shape=jax.ShapeDtypeStruct((B, H, T, d), BF16),
        scratch_shapes=[pltpu.VMEM((hp, SB_TQ, SB_TK), F32), pltpu.VMEM((hp, SB_TQ, d), F32)],
        compiler_params=pltpu.CompilerParams(dimension_semantics=("parallel", "parallel", "arbitrary"),
                                             vmem_limit_bytes=VMEM_LIMIT_BYTES),
        name="sb_prompt",
    )(q, k, v, suffix)
    return o.transpose(0, 2, 1, 3).reshape(B, T, H * d)


GLA_SUB = 16
GLA_EXP_CAP = 60.0


def _gla_kernel(q_ref, k_ref, v_ref, a_ref, r_ref, wd_ref, bd_ref, gn_ref, s0_ref, o_ref, st_ref, s_ref, *, sub):
    c = pl.program_id(2)
    C = q_ref.shape[0]

    @pl.when(c == 0)
    def _():
        s_ref[...] = s0_ref[0, 0]

    q = q_ref[...] * (GLA_DK ** -0.5)
    k = k_ref[...]
    v = v_ref[...].astype(BF16)
    pre = jnp.dot(a_ref[...].astype(BF16), wd_ref[...], preferred_element_type=F32) + bd_ref[...]
    log_a = -_softplus(-pre) * (1.0 / GLA_TAU)
    row = lax.broadcasted_iota(jnp.int32, (C, C), 0)
    col = lax.broadcasted_iota(jnp.int32, (C, C), 1)
    lower = jnp.where(col <= row, 1.0, 0.0).astype(BF16)
    a_hi, a_lo = _split_hi_lo(log_a)
    b = jnp.dot(lower, a_hi, preferred_element_type=F32) + jnp.dot(lower, a_lo, preferred_element_type=F32)

    s_old = s_ref[...]
    o = jnp.dot((q * jnp.exp(b)).astype(BF16), s_old.astype(BF16), preferred_element_type=F32)

    nt = (((1,), (1,)), ((), ()))
    parts = []
    for i in range(C // sub):
        lo, hi = i * sub, (i + 1) * sub
        ref_row = b[lo:lo + 1, :]
        qs = (q[lo:hi] * jnp.exp(b[lo:hi] - ref_row)).astype(BF16)
        ks = (k[:hi] * jnp.exp(jnp.minimum(ref_row - b[:hi], GLA_EXP_CAP))).astype(BF16)
        att = lax.dot_general(qs, ks, nt, preferred_element_type=F32)
        t_idx = lo + lax.broadcasted_iota(jnp.int32, (sub, hi), 0)
        s_idx = lax.broadcasted_iota(jnp.int32, (sub, hi), 1)
        att = jnp.where(s_idx <= t_idx, att, 0.0)
        parts.append(jnp.dot(att.astype(BF16), v[:hi], preferred_element_type=F32))
    o = o + (parts[0] if len(parts) == 1 else jnp.concatenate(parts, axis=0))

    b_t = b.T
    last = b_t[:, C - 1:C]
    kd_t = (k.T * jnp.exp(last - b_t)).astype(BF16)
    s_new = jnp.exp(last) * s_old + jnp.dot(kd_t, v, preferred_element_type=F32)
    s_ref[...] = s_new

    y = o * lax.rsqrt(jnp.mean(o * o, axis=-1, keepdims=True) + NORM_EPS) * gn_ref[...]
    r = r_ref[...]
    o_ref[0] = (y * (r * jax.nn.sigmoid(r))).astype(o_ref.dtype)

    @pl.when(c == pl.num_programs(2) - 1)
    def _():
        st_ref[0, 0] = s_new


def gla_mixer(z, row0, B, T, s0, w_decay, b_decay, out_norm):
    H, dk, dv = GLA_HEADS, GLA_DK, GLA_DV
    C = min(GLA_CHUNK, T)
    sub = min(GLA_SUB, C)
    assert row0 % C == 0 and T % C == 0
    n_c = T // C
    tok = lambda w, col0: pl.BlockSpec((C, w), lambda b, h, c: (row0 // C + b * n_c + c, col0 // w + h))
    wd = jnp.zeros((LANES, H * dk), BF16).at[Z_GA - Z_NG:Z_GA - Z_NG + GLA_RANK].set(w_decay.astype(BF16))
    o, st = pl.pallas_call(
        functools.partial(_gla_kernel, sub=sub),
        grid=(B, H, n_c),
        in_specs=[
            tok(dk, Z_GQ), tok(dk, Z_GK), tok(dv, Z_GV),
            pl.BlockSpec((C, LANES), lambda b, h, c: (row0 // C + b * n_c + c, Z_NG // LANES)),
            tok(dv, Z_GR),
            pl.BlockSpec((LANES, dk), lambda b, h, c: (0, h)),
            pl.BlockSpec((1, dk), lambda b, h, c: (0, h)),
            pl.BlockSpec((1, dv), lambda b, h, c: (0, 0)),
            pl.BlockSpec((1, 1, dk, dv), lambda b, h, c: (b, h, 0, 0)),
        ],
        out_specs=[pl.BlockSpec((1, C, dv), lambda b, h, c: (b, c, h)),
                   pl.BlockSpec((1, 1, dk, dv), lambda b, h, c: (b, h, 0, 0))],
        out_shape=[jax.ShapeDtypeStruct((B, T, H * dv), BF16), jax.ShapeDtypeStruct((B, H, dk, dv), F32)],
        scratch_shapes=[pltpu.VMEM((dk, dv), F32)],
        compiler_params=pltpu.CompilerParams(dimension_semantics=("parallel", "parallel", "arbitrary"),
                                             vmem_limit_bytes=VMEM_LIMIT_BYTES),
        name="gla_mixer",
    )(z, z, z, z, z, wd, b_decay.reshape(1, H * dk), out_norm.reshape(1, dv), s0)
    return o, st


def _sb_sample_kernel(pt_ref, q_ref, new_ref, page_ref, suffix_ref, o_ref, c_ref, acc_ref, *, n_new):
    s = pl.program_id(1)
    H, d = SB_HEADS, SB_D
    rows = q_ref.shape[1]
    nt = (((1,), (1,)), ((), ()))

    def tile(blk, ok):
        k = blk[:, :H * d].astype(BF16)
        v = blk[:, H * d:].astype(BF16)
        z = lax.dot_general(q_ref[0], k, nt, preferred_element_type=F32)
        _sb_tile(z, ok, v, suffix_ref[...], c_ref, acc_ref)

    @pl.when(s == 0)
    def _():
        c_ref[...] = jnp.zeros(c_ref.shape, F32)
        acc_ref[...] = jnp.zeros(acc_ref.shape, F32)
        t_idx = lax.broadcasted_iota(jnp.int32, c_ref.shape, 0) % n_new
        k_idx = lax.broadcasted_iota(jnp.int32, c_ref.shape, 1)
        tile(new_ref[0], k_idx < t_idx)

    @pl.when(s > 0)
    def _():
        tile(page_ref[0, 0], None)

    @pl.when(s == pl.num_programs(1) - 1)
    def _():
        acc = acc_ref[...]
        r_idx = lax.broadcasted_iota(jnp.int32, acc.shape, 0) // n_new
        l_idx = lax.broadcasted_iota(jnp.int32, acc.shape, 1) // d
        own = jnp.where(r_idx == l_idx, acc, 0.0)
        o_ref[0] = jnp.sum(own.reshape(H, n_new, H * d), axis=0).astype(o_ref.dtype)


def sb_sample(sq, skv, cache_sb, page_table, layer):
    B, tn = sq.shape[:2]
    H, d = SB_HEADS, SB_D
    n_pages = page_table.shape[1]
    page = cache_sb.shape[2]
    cache = cache_sb.reshape(cache_sb.shape[0], cache_sb.shape[1], page, 2 * H * d)
    q = (sq * (d ** -0.5)).astype(BF16).reshape(B, tn, H, d).transpose(0, 2, 1, 3)
    q_blk = (q[:, :, :, None, :] * jnp.eye(H, dtype=BF16)[None, :, None, :, None]).reshape(B, H * tn, H * d)
    new = jnp.pad(skv.reshape(B, tn, 2 * H * d), ((0, 0), (0, page - tn), (0, 0)))
    suffix = _suffix_matrix(page)
    grid_spec = pltpu.PrefetchScalarGridSpec(
        num_scalar_prefetch=1,
        grid=(B, n_pages + 1),
        in_specs=[
            pl.BlockSpec((1, H * tn, H * d), lambda b, s, pt: (b, 0, 0)),
            pl.BlockSpec((1, page, 2 * H * d), lambda b, s, pt: (b, 0, 0)),
            pl.BlockSpec((1, 1, page, 2 * H * d),
                         lambda b, s, pt: (pt[b, n_pages - jnp.maximum(s, 1)], layer, 0, 0)),
            pl.BlockSpec(suffix.shape, lambda b, s, pt: (0, 0)),
        ],
        out_specs=pl.BlockSpec((1, tn, H * d), lambda b, s, pt: (b, 0, 0)),
        scratch_shapes=[pltpu.VMEM((H * tn, page), F32), pltpu.VMEM((H * tn, H * d), F32)],
    )
    return pl.pallas_call(
        functools.partial(_sb_sample_kernel, n_new=tn),
        grid_spec=grid_spec,
        out_shape=jax.ShapeDtypeStruct((B, tn, H * d), BF16),
        compiler_params=pltpu.CompilerParams(dimension_semantics=("parallel", "arbitrary"),
                                             vmem_limit_bytes=VMEM_LIMIT_BYTES),
        name="sb_sample",
    )(page_table, q_blk, new, cache, suffix)


def _nsa_sample_cmp_kernel(pt_ref, q_ref, *refs, n_pg, n_new, past_len):
    page_refs = refs[:n_pg]
    posa_ref, posb_ref, w1_ref, w2_ref, biasc_ref, mapt_ref, oc_ref, sel_ref, x_ref = refs[n_pg:]
    s = pl.program_id(1)
    page = page_refs[0].shape[2]
    G, R = NSA_KV_HEADS, NSA_GROUP
    gw = G * NSA_D
    per = gw // LANES
    for i, page_ref in enumerate(page_refs):
        row0 = pl.multiple_of((s * n_pg + i) * page, page)
        for c in range(x_ref.shape[0]):
            x_ref[c, pl.ds(row0, page), :] = page_ref[0, 0, :, c * LANES:(c + 1) * LANES]

    @pl.when(s == pl.num_programs(1) - 1)
    def _():
        n_sub = x_ref.shape[1] // CMP_STRIDE
        n_cmp = n_sub - 1
        kv = []
        for t in range(2):
            first = jnp.zeros((n_sub, gw), F32)
            second = jnp.zeros((n_sub, gw), F32)
            for l in range(CMP_STRIDE):
                x = jnp.concatenate([x_ref[t * per + c, pl.ds(l, n_sub, stride=CMP_STRIDE), :] for c in range(per)],
                                    axis=1)
                first += jnp.dot((x + posa_ref[t, l]).astype(BF16), w1_ref[t, l], preferred_element_type=F32)
                second += jnp.dot((x + posb_ref[t, l]).astype(BF16), w1_ref[t, CMP_STRIDE + l],
                                  preferred_element_type=F32)
            pre = first + pltpu.roll(second, n_sub - 1, 0)
            hid = (pre * jax.nn.sigmoid(pre)).astype(BF16)
            kv.append(jnp.dot(hid, w2_ref[t], preferred_element_type=F32).astype(BF16))
        kc, vc = kv
        nt = (((1,), (1,)), ((), ()))
        sc = lax.dot_general(q_ref[0], kc, nt, preferred_element_type=F32) + biasc_ref[...]
        ok = lax.broadcasted_iota(jnp.int32, sc.shape, 1) < n_cmp
        sc = jnp.where(ok, sc, NEG)
        e = jnp.exp(sc - jnp.max(sc, axis=-1, keepdims=True))
        p_c = jnp.where(ok, e / jnp.sum(e, axis=-1, keepdims=True), 0.0)
        oc_ref[0] = jnp.dot(p_c.astype(BF16), vc, preferred_element_type=F32)
        psum = jnp.sum(p_c.reshape(G, R, n_new, n_sub), axis=1).reshape(G * n_new, n_sub)
        p_hi, p_lo = _split_hi_lo(psum)
        imp = (lax.dot_general(mapt_ref[...], p_hi, nt, preferred_element_type=F32)
               + lax.dot_general(mapt_ref[...], p_lo, nt, preferred_element_type=F32))
        ns_pad, cols = imp.shape
        ns = past_len // SEL_BLOCK + 1
        j_idx = lax.broadcasted_iota(jnp.int32, imp.shape, 0)
        t_pos = past_len + lax.broadcasted_iota(jnp.int32, imp.shape, 1) % n_new
        cur = t_pos // SEL_BLOCK
        valid = (j_idx * SEL_BLOCK <= t_pos) & (j_idx < ns)
        forced = (j_idx == 0) | (j_idx == cur) | (j_idx == cur - 1)
        score = jnp.where(valid, jnp.where(forced, FORCE, imp), NEG)

        def rank_step(i, rank):
            row = sel_scratch_row(score, i)
            beats = (row > score) | ((row == score) & (j_idx > i))
            return rank + jnp.where(beats, 1.0, 0.0)

        def sel_scratch_row(a, i):
            return jnp.sum(jnp.where(j_idx == i, a, 0.0), axis=0, keepdims=True)

        rank = lax.fori_loop(0, ns, rank_step, jnp.zeros(imp.shape, F32))
        sel_t = jnp.where((rank < float(min(N_SEL, ns))) & valid, 1.0, 0.0)
        sel = sel_t.T.reshape(G, 1, n_new, ns_pad)
        sel_ref[0] = jnp.broadcast_to(sel, (G, R, n_new, ns_pad)).reshape(G * R * n_new, ns_pad).astype(BF16)


def _two_part_softmax_av(s_a, v_a, s_b, v_b):
    m = jnp.maximum(jnp.max(s_a, axis=-1, keepdims=True), jnp.max(s_b, axis=-1, keepdims=True))
    p_a = jnp.exp(s_a - m)
    p_b = jnp.exp(s_b - m)
    den = jnp.sum(p_a, axis=-1, keepdims=True) + jnp.sum(p_b, axis=-1, keepdims=True)
    num = (jnp.dot(p_a.astype(BF16), v_a, preferred_element_type=F32)
           + jnp.dot(p_b.astype(BF16), v_b, preferred_element_type=F32))
    return num / den


def _nsa_sample_attn_kernel(pt_ref, q_ref, sel_ref, oc_ref, gate_ref, new_ref, win_ref, *refs, n_pg, n_pages):
    page_refs = refs[:n_pg]
    bnew_ref, bwin_ref, blast_ref, bfar_ref, o_ref, m_ref, l_ref, acc_ref, ow_ref = refs[n_pg:]
    s = pl.program_id(1)
    G, d = NSA_KV_HEADS, NSA_D
    gw = G * d
    page = page_refs[0].shape[2]
    rows = q_ref.shape[1]
    q = q_ref[0]
    nt = (((1,), (1,)), ((), ()))

    def online(sc, v):
        m_old = m_ref[...]
        m_new = jnp.maximum(m_old, jnp.max(sc, axis=-1, keepdims=True))
        alpha = jnp.exp(m_old - m_new)
        p = jnp.exp(sc - m_new)
        l_ref[...] = alpha * l_ref[...] + jnp.sum(p, axis=-1, keepdims=True)
        acc_ref[...] = alpha * acc_ref[...] + jnp.dot(p.astype(BF16), v, preferred_element_type=F32)
        m_ref[...] = m_new

    @pl.when(s == 0)
    def _():
        m_ref[...] = jnp.full(m_ref.shape, NEG, F32)
        l_ref[...] = jnp.zeros(l_ref.shape, F32)
        acc_ref[...] = jnp.zeros(acc_ref.shape, F32)
        new = new_ref[0].astype(BF16)
        online(lax.dot_general(q, new[:, :gw], nt, preferred_element_type=F32) + bnew_ref[...], new[:, gw:2 * gw])
        win = win_ref[0, 0].astype(BF16)
        s_w = lax.dot_general(q, win[:, :gw], nt, preferred_element_type=F32) + bwin_ref[...]
        s_n = lax.dot_general(q, new[:, 2 * gw:3 * gw], nt, preferred_element_type=F32) + bnew_ref[...]
        ow_ref[...] = _two_part_softmax_av(s_w, win[:, gw:], s_n, new[:, 3 * gw:])

    @pl.when(s > 0)
    def _():
        pg = n_pages - s * n_pg
        blk = jnp.concatenate([r[0, 0].astype(BF16) for r in page_refs], axis=0)
        keys = blk.shape[0]
        sc = lax.dot_general(q, blk[:, :gw], nt, preferred_element_type=F32)
        sc = sc + jnp.where(s == 1, blast_ref[...], bfar_ref[...])
        ns_pad = sel_ref.shape[2]
        blk_idx = lax.broadcasted_iota(jnp.int32, (ns_pad, keys), 0)
        key_blk = pg * (page // SEL_BLOCK) + lax.broadcasted_iota(jnp.int32, (ns_pad, keys), 1) // SEL_BLOCK
        expand = jnp.where(blk_idx == key_blk, 1.0, 0.0).astype(BF16)
        hit = jnp.dot(sel_ref[0], expand, preferred_element_type=F32)
        online(jnp.where(hit > 0.5, sc, NEG), blk[:, gw:])

    @pl.when(s == pl.num_programs(1) - 1)
    def _():
        g = jax.nn.sigmoid(gate_ref[0])
        o = g[:, 0:1] * oc_ref[0] + g[:, 1:2] * (acc_ref[...] / l_ref[...]) + g[:, 2:3] * ow_ref[...]
        r_grp = lax.broadcasted_iota(jnp.int32, o.shape, 0) // (rows // G)
        l_grp = lax.broadcasted_iota(jnp.int32, o.shape, 1) // d
        own = jnp.where(r_grp == l_grp, o, 0.0)
        out = own[:, 0:d]
        for gi in range(1, G):
            out = out + own[:, gi * d:(gi + 1) * d]
        o_ref[0] = out.astype(o_ref.dtype)


def nsa_sample(nq, ng, nkv, cache_nsa, cache_win_l, page_table, layer, cmp_pos, cmp_w1, cmp_w2, rpb):
    B, tn = nq.shape[:2]
    G, R, d = NSA_KV_HEADS, NSA_GROUP, NSA_D
    gw = G * d
    n_pages = page_table.shape[1]
    page = cache_nsa.shape[2]
    past_len = n_pages * page
    assert tn < CMP_STRIDE and page % SEL_BLOCK == 0 and past_len % SEL_BLOCK == 0
    wb = cache_win_l.shape[1]
    n_sub = past_len // CMP_STRIDE
    ns = past_len // SEL_BLOCK + 1
    ns_pad = -(-ns // LANES) * LANES
    rows = G * R * tn
    cache = cache_nsa.reshape(cache_nsa.shape[:2] + (page, 4 * gw))
    q = (nq * (d ** -0.5)).astype(BF16).reshape(B, tn, G, R, d).transpose(0, 2, 3, 1, 4)
    q_blk = (q[:, :, :, :, None, :] * jnp.eye(G, dtype=BF16)[None, :, None, None, :, None]).reshape(B, rows, gw)
    gates = ng.reshape(B, tn, G, R, 3).transpose(0, 2, 3, 1, 4).reshape(B, rows, 3)
    new = jnp.pad(nkv[:, :, 2:6].reshape(B, tn, 4 * gw), ((0, 0), (0, page - tn), (0, 0)))
    win = cache_win_l.reshape(B, 1, wb, 2 * gw)
    eye_g = jnp.eye(G, dtype=F32)
    w1 = jnp.einsum('tlde,gh->tlgdhe', cmp_w1, eye_g).reshape(2, CMP_LEN, gw, G * CMP_HID).astype(BF16)
    w2 = jnp.einsum('ted,gh->tgehd', cmp_w2, eye_g).reshape(2, G * CMP_HID, gw).astype(BF16)
    pos = jnp.tile(cmp_pos[:, :, None, :], (1, 1, G, 1)).reshape(2, CMP_LEN, 1, gw)
    pos_a, pos_b = pos[:, :CMP_STRIDE], pos[:, CMP_STRIDE:]
    table = rpb.astype(F32).reshape(N_BUCKETS, G, R).transpose(1, 2, 0)
    q_pos = past_len + jnp.arange(tn)

    def bias_rows(k_pos, ok):
        dist = q_pos[:, None] - k_pos[None, :]
        b = jnp.take(table, _rel_bucket(dist), axis=2)
        return jnp.where(ok(dist, k_pos[None, :]), b, NEG).reshape(rows, k_pos.shape[0])

    c_end = jnp.arange(n_sub) * CMP_STRIDE + (CMP_LEN - 1)
    bias_c = bias_rows(c_end, lambda dist, kp: dist >= 0)
    new_pos = past_len + jnp.arange(page)
    bias_new = bias_rows(new_pos, lambda dist, kp: (dist >= 0) & (kp < past_len + tn))
    win_pos = past_len - wb + jnp.arange(wb)
    bias_win = bias_rows(win_pos, lambda dist, kp: (dist >= 0) & (dist < WINDOW) & (kp >= 0))
    n_pg = next(n for n in (4, 2, 1) if n_pages % n == 0)
    bias_last = bias_rows(past_len - n_pg * page + jnp.arange(n_pg * page), lambda dist, kp: dist >= 0)
    bias_far = jnp.broadcast_to(table[:, :, None, N_BUCKETS - 1:], (G, R, tn, 1)).reshape(rows, 1)
    map_t = jnp.pad(_cmp_sel_map(n_sub, ns).T, ((0, ns_pad - ns), (0, 0))).astype(BF16)

    const = lambda a: pl.BlockSpec(a.shape, lambda b, s, pt: (0,) * a.ndim)
    per_b = lambda a: pl.BlockSpec((1,) + a.shape[1:], lambda b, s, pt: (b,) + (0,) * (a.ndim - 1))
    params = pltpu.CompilerParams(dimension_semantics=("parallel", "arbitrary"), vmem_limit_bytes=VMEM_LIMIT_BYTES)
    o_c, sel = pl.pallas_call(
        functools.partial(_nsa_sample_cmp_kernel, n_pg=n_pg, n_new=tn, past_len=past_len),
        grid_spec=pltpu.PrefetchScalarGridSpec(
            num_scalar_prefetch=1,
            grid=(B, n_pages // n_pg),
            in_specs=[per_b(q_blk)] + [
                pl.BlockSpec((1, 1, page, 2 * gw), lambda b, s, pt, i=i: (pt[b, s * n_pg + i], layer, 0, 0))
                for i in range(n_pg)] + [
                const(pos_a), const(pos_b), const(w1), const(w2), const(bias_c), const(map_t),
            ],
            out_specs=[pl.BlockSpec((1, rows, gw), lambda b, s, pt: (b, 0, 0)),
                       pl.BlockSpec((1, rows, ns_pad), lambda b, s, pt: (b, 0, 0))],
            scratch_shapes=[pltpu.VMEM((2 * gw // LANES, past_len, LANES), F32)],
        ),
        out_shape=[jax.ShapeDtypeStruct((B, rows, gw), F32), jax.ShapeDtypeStruct((B, rows, ns_pad), BF16)],
        compiler_params=params,
        name="nsa_sample_cmp",
    )(page_table, q_blk, *([cache] * n_pg), pos_a, pos_b, w1, w2, bias_c, map_t)
    o = pl.pallas_call(
        functools.partial(_nsa_sample_attn_kernel, n_pg=n_pg, n_pages=n_pages),
        grid_spec=pltpu.PrefetchScalarGridSpec(
            num_scalar_prefetch=1,
            grid=(B, n_pages // n_pg + 1),
            in_specs=[per_b(q_blk), per_b(sel), per_b(o_c), per_b(gates), per_b(new), per_b(win)] + [
                pl.BlockSpec((1, 1, page, 2 * gw),
                             lambda b, s, pt, i=i: (pt[b, n_pages - jnp.maximum(s, 1) * n_pg + i], layer, 0, 1))
                for i in range(n_pg)] + [
                const(bias_new), const(bias_win), const(bias_last), const(bias_far),
            ],
            out_specs=pl.BlockSpec((1, rows, d), lambda b, s, pt: (b, 0, 0)),
            scratch_shapes=[pltpu.VMEM((rows, 1), F32), pltpu.VMEM((rows, 1), F32), pltpu.VMEM((rows, gw), F32),
                            pltpu.VMEM((rows, gw), F32)],
        ),
        out_shape=jax.ShapeDtypeStruct((B, rows, d), BF16),
        compiler_params=params,
        name="nsa_sample_attn",
    )(page_table, q_blk, sel, o_c, gates, new, win, *([cache] * n_pg), bias_new, bias_win, bias_last, bias_far)
    return o.reshape(B, G * R, tn, d).transpose(0, 2, 1, 3).reshape(B, tn, G * R * d)


def _rmsnorm(x, g):
    xf = x.astype(jnp.float32)
    y = xf * lax.rsqrt(jnp.mean(xf * xf, axis=-1, keepdims=True) + NORM_EPS)
    return (y * g.astype(jnp.float32)).astype(x.dtype)


def _rel_bucket(dist):
    n = jnp.maximum(dist, 0)
    nf = jnp.maximum(n, 1).astype(jnp.float32)
    large = RPB_MAX_EXACT + (jnp.log(nf / RPB_MAX_EXACT) / math.log(RPB_MAX_DIST / RPB_MAX_EXACT)
                             * (N_BUCKETS - RPB_MAX_EXACT)).astype(jnp.int32)
    return jnp.where(n < RPB_MAX_EXACT, n, jnp.minimum(large, N_BUCKETS - 1))


def _gla_scan(q, k, v, log_a, s0):
    B, T = q.shape[:2]
    C = min(GLA_CHUNK, T)
    n_chunks = -(-T // C)
    pad = n_chunks * C - T

    def prep(a):
        a = jnp.pad(a, ((0, 0), (0, pad), (0, 0), (0, 0)))
        return a.reshape(B, n_chunks, C, a.shape[2], a.shape[3]).transpose(1, 0, 3, 2, 4)

    xs = (prep(q), prep(k), prep(v), prep(log_a.astype(jnp.float32)))
    tri = jnp.tril(jnp.ones((C, C), dtype=bool))

    def step(S, inp):
        qb, kb, vb, ab = inp
        qf, kf, vf = qb.astype(jnp.float32), kb.astype(jnp.float32), vb.astype(jnp.float32)
        b = jnp.cumsum(ab, axis=2)
        o_inter = jnp.einsum('bhcd,bhde->bhce', qf * jnp.exp(b), S)
        rel = jnp.where(tri[None, None, :, :, None], b[:, :, :, None, :] - b[:, :, None, :, :], -jnp.inf)
        att = jnp.einsum('bhtd,bhsd,bhtsd->bhts', qf, kf, jnp.exp(rel))
        o = o_inter + jnp.einsum('bhts,bhse->bhte', att, vf)
        b_last = b[:, :, -1:, :]
        S = jnp.exp(b_last[:, :, 0, :])[..., None] * S + jnp.einsum('bhsd,bhse->bhde', kf * jnp.exp(b_last - b), vf)
        return S, o

    S, o = lax.scan(step, s0.astype(jnp.float32), xs)
    o = o.transpose(1, 0, 3, 2, 4).reshape(B, n_chunks * C, GLA_HEADS, GLA_DV)[:, :T]
    return o.astype(q.dtype), S.astype(s0.dtype)


def _gla_mixer(q, k, v, a_low, r, s0, w_decay, b_decay, out_norm):
    B, T = q.shape[:2]
    q = q.reshape(B, T, GLA_HEADS, GLA_DK) * (GLA_DK ** -0.5)
    k = k.reshape(B, T, GLA_HEADS, GLA_DK)
    v = v.reshape(B, T, GLA_HEADS, GLA_DV)
    log_a = (jax.nn.log_sigmoid((a_low @ w_decay + b_decay).astype(jnp.float32)) / GLA_TAU).reshape(B, T, GLA_HEADS, GLA_DK)
    o, state = _gla_scan(q, k, v, log_a, s0)
    o = _rmsnorm(o, out_norm).reshape(B, T, GLA_VAL) * jax.nn.silu(r)
    return o, state


def _nsa_compress(rows, pos_emb, w1, w2):
    B, S = rows.shape[:2]
    n_sub = S // CMP_STRIDE
    sub = rows[:, :n_sub * CMP_STRIDE].reshape(B, n_sub, CMP_STRIDE, NSA_KV_HEADS, NSA_D)
    first = jnp.einsum('bnlgd,lde->bnge', sub + pos_emb[None, None, :CMP_STRIDE, None, :], w1[:CMP_STRIDE])
    second = jnp.einsum('bnlgd,lde->bnge', sub + pos_emb[None, None, CMP_STRIDE:, None, :], w1[CMP_STRIDE:])
    hid = jax.nn.silu(first[:, :-1] + second[:, 1:])
    return jnp.einsum('bnge,ed->bngd', hid, w2)


def _sel_blocks(rows):
    B, S = rows.shape[:2]
    ns = -(-S // SEL_BLOCK)
    rows = jnp.pad(rows, ((0, 0), (0, ns * SEL_BLOCK - S), (0, 0), (0, 0)))
    return rows.reshape(B, ns, SEL_BLOCK, NSA_KV_HEADS, NSA_D).transpose(0, 3, 1, 2, 4)


def _cmp_sel_map(nc, ns):
    c_start = jnp.arange(nc) * CMP_STRIDE
    s_start = jnp.arange(ns) * SEL_BLOCK
    hit = (c_start[:, None] < s_start[None, :] + SEL_BLOCK) & (c_start[:, None] + CMP_LEN > s_start[None, :])
    return hit.astype(jnp.float32)


def _nsa_block(q, gates, q_pos, kc, vc, c_end, ksb, vsb, wk, wv, w_pos, rpb):
    B, Q, G, R, d = q.shape
    scale = d ** -0.5
    table = rpb.astype(jnp.float32).reshape(N_BUCKETS, G, R)
    s = jnp.einsum('bqgrd,bngd->bqgrn', q, kc).astype(jnp.float32) * scale
    s = s + table[_rel_bucket(q_pos[:, None] - c_end[None, :])].transpose(0, 2, 3, 1)[None]
    m = (c_end[None, :] <= q_pos[:, None])[None, :, None, None, :]
    p_c = jax.nn.softmax(jnp.where(m, s, NEG), axis=-1) * m
    o_c = jnp.einsum('bqgrn,bngd->bqgrd', p_c.astype(vc.dtype), vc)
    ns = ksb.shape[2]
    imp = jnp.einsum('bqgn,nj->bqgj', p_c.sum(axis=3), _cmp_sel_map(kc.shape[1], ns))
    j = jnp.arange(ns)[None, :]
    cur = (q_pos // SEL_BLOCK)[:, None]
    valid = (j * SEL_BLOCK <= q_pos[:, None])[None, :, None, :]
    forced = ((j == 0) | (j == cur) | (j == cur - 1))[None, :, None, :]
    score = jnp.where(valid, jnp.where(forced, FORCE, imp), NEG)
    _, idx = lax.top_k(score, min(N_SEL, ns))
    idx = idx.transpose(0, 2, 1, 3)
    bi = jnp.arange(B)[:, None, None, None]
    gi = jnp.arange(G)[None, :, None, None]
    kg = ksb[bi, gi, idx]
    vg = vsb[bi, gi, idx]
    k_pos = idx[..., None] * SEL_BLOCK + jnp.arange(SEL_BLOCK)
    s = jnp.einsum('bqgrd,bgqnld->bgqrnl', q, kg).astype(jnp.float32) * scale
    bias = table.transpose(1, 0, 2)[gi[..., None], _rel_bucket(q_pos[None, None, :, None, None] - k_pos)]
    s = s + bias.transpose(0, 1, 2, 5, 3, 4)
    m = (k_pos <= q_pos[None, None, :, None, None])[:, :, :, None]
    shp = s.shape
    p = jax.nn.softmax(jnp.where(m, s, NEG).reshape(shp[0], shp[1], shp[2], shp[3], -1), axis=-1).reshape(shp)
    o_s = jnp.einsum('bgqrnl,bgqnld->bqgrd', p.astype(vg.dtype), vg)
    s = jnp.einsum('bqgrd,bsgd->bqgrs', q, wk).astype(jnp.float32) * scale
    dist = q_pos[:, None] - w_pos[None, :]
    s = s + table[_rel_bucket(dist)].transpose(0, 2, 3, 1)[None]
    m = ((dist >= 0) & (dist < WINDOW) & (w_pos[None, :] >= 0))[None, :, None, None, :]
    p = jax.nn.softmax(jnp.where(m, s, NEG), axis=-1)
    o_w = jnp.einsum('bqgrs,bsgd->bqgrd', p.astype(wv.dtype), wv)
    g = jax.nn.sigmoid(gates.astype(jnp.float32))
    o = g[..., 0:1] * o_c + g[..., 1:2] * o_s + g[..., 2:3] * o_w
    return o.reshape(B, Q, G * R * d).astype(q.dtype)


def _sb_block(q, q_pos, k, v, k_pos):
    z = jnp.einsum('bqhd,bshd->bhqs', q, k).astype(jnp.float32) * (SB_D ** -0.5)
    m = (k_pos[None, :] < q_pos[:, None])[None, None]
    log_fail = jnp.where(m, jax.nn.log_sigmoid(-z), 0.0)
    later = lax.cumsum(log_fail, axis=3, reverse=True) - log_fail
    a = jnp.where(m, jnp.exp(jax.nn.log_sigmoid(z) + later), 0.0)
    return jnp.einsum('bhqs,bshd->bqhd', a.astype(v.dtype), v)


def _permute_w_in(w):
    c = [0] + [int(v) for v in np.cumsum(IN_SPLITS)]
    order = (0, 1, 2, 4, 5, 6, 8, 9, 10, 7, 3)
    parts = [w[:, c[i]:c[i + 1]] for i in order]
    return jnp.pad(jnp.concatenate(parts, axis=1), ((0, 0), (0, N_IN_PAD - N_IN)))


def _attn_parts(z, B, T):
    cut = lambda lo, w: z[:, lo:lo + w].reshape(B, T, w)
    nkv = cut(Z_NKV, 6 * NSA_KVW).reshape(B, T, 6, NSA_KV_HEADS, NSA_D)
    skv = cut(Z_SKV, 2 * SB_OUT).reshape(B, T, 2, SB_HEADS, SB_D)
    return cut(Z_NQ, NSA_OUT), nkv, cut(Z_NG, 3 * NSA_HEADS), cut(Z_SQ, SB_OUT), skv


def _mix_prompt(z, B, T, gla_w_decay, gla_b_decay, gla_out_norm, cmp_pos, cmp_w1, cmp_w2, rpb):
    nq, nkv, ng, sq, skv = _attn_parts(z[:B * T], B, T)
    s0 = jnp.zeros((B, GLA_HEADS, GLA_DK, GLA_DV), jnp.float32)
    o_gla, gla_state = gla_mixer(z, 0, B, T, s0, gla_w_decay, gla_b_decay, gla_out_norm)
    o_nsa = nsa_prompt_t(nq, ng, nkv, cmp_pos, cmp_w1, cmp_w2, rpb)
    o_sb = sb_prompt(sq, skv)
    keep = min(WINDOW, T)
    return (o_gla, o_nsa, o_sb), nkv[:, :, :4], skv, nkv[:, T - keep:, 4:], gla_state


def _mix_sample(z, row0, B, T, cache_nsa, cache_sb, page_table, layer, win_buf, s0, gla_w_decay, gla_b_decay,
                gla_out_norm, cmp_pos, cmp_w1, cmp_w2, rpb):
    nq, nkv, ng, sq, skv = _attn_parts(z[row0:row0 + B * T], B, T)
    o_gla, gla_state = gla_mixer(z, row0, B, T, s0, gla_w_decay, gla_b_decay, gla_out_norm)
    o_nsa = nsa_sample(nq, ng, nkv, cache_nsa, win_buf, page_table, layer, cmp_pos, cmp_w1, cmp_w2, rpb)
    win = jnp.concatenate([win_buf, nkv[:, :, 4:]], axis=1)
    wb = win_buf.shape[1]
    o_sb = sb_sample(sq, skv, cache_sb, page_table, layer)
    keep = min(WINDOW, wb + T)
    return (o_gla, o_nsa, o_sb), nkv[:, :, :4], skv, win[:, wb + T - keep:], gla_state


def kernel(x_prompt, x_sample, cache_nsa, cache_sb, cache_win, state_gla, page_table, ffn1_norm, ffn1_w_gate, ffn1_w_up, ffn1_w_down, mix_norm, w_in, gla_w_decay, gla_b_decay, gla_out_norm, nsa_cmp_pos, nsa_cmp_w1, nsa_cmp_w2, rpb_table, w_branch_gla, w_branch_nsa, w_branch_sb, w_out, ffn2_norm, ffn2_w_gate, ffn2_w_up, ffn2_w_down, final_norm):
    bp, tp, _ = x_prompt.shape
    bs, ts, _ = x_sample.shape
    mp, ms = bp * tp, bs * ts
    dec_batch, n_pages = page_table.shape
    past_len = n_pages * cache_nsa.shape[2]
    x = jnp.concatenate([x_prompt.reshape(mp, D_MODEL), x_sample.reshape(ms, D_MODEL)], axis=0)
    outs = [[] for _ in range(8)]
    for l in range(DEPTH):
        x = ffn_half_step(x, ffn1_norm[l], ffn1_w_gate[l].astype(BF16), ffn1_w_up[l].astype(BF16),
                          ffn1_w_down[l].astype(BF16))
        z = in_projection(x, mix_norm[l], _permute_w_in(w_in[l].astype(BF16)))
        mix_w = (gla_w_decay[l], gla_b_decay[l], gla_out_norm[l], nsa_cmp_pos[l], nsa_cmp_w1[l], nsa_cmp_w2[l],
                 rpb_table)
        br_p, r_nsa, r_sb, r_win, r_gla = _mix_prompt(z, bp, tp, *mix_w)
        for lst, val in zip(outs[0::2], (r_nsa, r_sb, r_win, r_gla)):
            lst.append(val)
        br_s, r_nsa, r_sb, r_win, r_gla = _mix_sample(z, mp, bs, ts, cache_nsa, cache_sb, page_table, l,
                                                      cache_win[:, l], state_gla[:, l], *mix_w)
        for lst, val in zip(outs[1::2], (r_nsa, r_sb, r_win, r_gla)):
            lst.append(val)
        o_gla, o_nsa, o_sb = (
            jnp.concatenate([a.reshape(mp, a.shape[-1]), b.reshape(ms, b.shape[-1])], axis=0)
            for a, b in zip(br_p, br_s))
        y = merge_branches(o_gla, o_nsa, o_sb, z,
                           w_branch_gla[l].astype(BF16), w_branch_nsa[l].astype(BF16), w_branch_sb[l].astype(BF16))
        x = out_projection(x, y, w_out[l].astype(BF16))
        x = ffn_half_step(x, ffn2_norm[l], ffn2_w_gate[l].astype(BF16), ffn2_w_up[l].astype(BF16),
                          ffn2_w_down[l].astype(BF16))
    y = final_rmsnorm(x, final_norm)
    return (y[:mp].reshape(bp, tp, D_MODEL), y[mp:].reshape(bs, ts, D_MODEL)) + tuple(
        jnp.stack(o, axis=1) for o in outs)
```

```python
import functools
import math

import jax
import jax.numpy as jnp
import numpy as np
from jax import lax
from jax.experimental import pallas as pl
from jax.experimental.pallas import tpu as pltpu

D_MODEL = 2048
DEPTH = 2
D_FF = 5632
NORM_EPS = 1e-6
GLA_HEADS = 4
GLA_DK = 128
GLA_DV = 256
GLA_RANK = 16
GLA_TAU = 16.0
GLA_CHUNK = 64
GLA_KEY = GLA_HEADS * GLA_DK
GLA_VAL = GLA_HEADS * GLA_DV
NSA_HEADS = 16
NSA_KV_HEADS = 4
NSA_GROUP = NSA_HEADS // NSA_KV_HEADS
NSA_D = 64
NSA_OUT = NSA_HEADS * NSA_D
NSA_KVW = NSA_KV_HEADS * NSA_D
CMP_STRIDE = 16
CMP_LEN = 2 * CMP_STRIDE
CMP_HID = 64
SEL_BLOCK = 64
N_SEL = 16
WINDOW = 512
NSA_QBLK = 32
SB_HEADS = 16
SB_D = 64
SB_OUT = SB_HEADS * SB_D
SB_QBLK = 128
N_BUCKETS = 32
RPB_MAX_EXACT = 16
RPB_MAX_DIST = 128
NEG = -1e30
FORCE = 1e9
IN_SPLITS = (GLA_KEY, GLA_KEY, GLA_VAL, GLA_RANK, GLA_VAL, NSA_OUT, 6 * NSA_KVW, 3 * NSA_HEADS, SB_OUT, 2 * SB_OUT,
             3 * D_MODEL)
N_IN = sum(IN_SPLITS)

BF16 = jnp.bfloat16
F32 = jnp.float32

VMEM_LIMIT_BYTES = 56 * 1024 * 1024
LANES = 128
ROW_TILE = 640
FF_TILE = 512
PROJ_TILE = 1152
N_IN_PAD = 14976
OUT_TILE = 512
Z_GQ, Z_GK, Z_GV, Z_GR = 0, GLA_KEY, 2 * GLA_KEY, 2 * GLA_KEY + GLA_VAL
Z_NQ = Z_GR + GLA_VAL
Z_NKV = Z_NQ + NSA_OUT
Z_SQ = Z_NKV + 6 * NSA_KVW
Z_SKV = Z_SQ + SB_OUT
Z_MG = Z_SKV + 2 * SB_OUT
Z_NG = Z_MG + 3 * D_MODEL
Z_GA = Z_NG + 3 * NSA_HEADS
assert Z_GA + GLA_RANK == N_IN and Z_NG % LANES == 0 and Z_MG % OUT_TILE == 0


def _rms_bf16(x, g):
    y = x * lax.rsqrt(jnp.mean(x * x, axis=-1, keepdims=True) + NORM_EPS)
    return (y * g).astype(BF16)


def _ffn_kernel(x_ref, g_ref, wg_ref, wu_ref, wd_ref, o_ref, h_ref, acc_ref):
    f = pl.program_id(1)

    @pl.when(f == 0)
    def _():
        h_ref[...] = _rms_bf16(x_ref[...], g_ref[...])
        acc_ref[...] = jnp.zeros_like(acc_ref)

    h = h_ref[...]
    a = jnp.dot(h, wg_ref[...], preferred_element_type=F32)
    b = jnp.dot(h, wu_ref[...], preferred_element_type=F32)
    act = (a * jax.nn.sigmoid(a) * b).astype(BF16)
    acc_ref[...] += jnp.dot(act, wd_ref[...], preferred_element_type=F32)

    @pl.when(f == pl.num_programs(1) - 1)
    def _():
        o_ref[...] = x_ref[...] + 0.5 * acc_ref[...]


def ffn_half_step(x, g, wg, wu, wd):
    m = x.shape[0]
    return pl.pallas_call(
        _ffn_kernel,
        grid=(m // ROW_TILE, D_FF // FF_TILE),
        in_specs=[
            pl.BlockSpec((ROW_TILE, D_MODEL), lambda i, f: (i, 0)),
            pl.BlockSpec((1, D_MODEL), lambda i, f: (0, 0)),
            pl.BlockSpec((D_MODEL, FF_TILE), lambda i, f: (0, f)),
            pl.BlockSpec((D_MODEL, FF_TILE), lambda i, f: (0, f)),
            pl.BlockSpec((FF_TILE, D_MODEL), lambda i, f: (f, 0)),
        ],
        out_specs=pl.BlockSpec((ROW_TILE, D_MODEL), lambda i, f: (i, 0)),
        out_shape=jax.ShapeDtypeStruct((m, D_MODEL), F32),
        scratch_shapes=[pltpu.VMEM((ROW_TILE, D_MODEL), BF16), pltpu.VMEM((ROW_TILE, D_MODEL), F32)],
        compiler_params=pltpu.CompilerParams(dimension_semantics=("parallel", "arbitrary"),
                                             vmem_limit_bytes=VMEM_LIMIT_BYTES),
        name="ffn_half_step",
    )(x, g.reshape(1, D_MODEL), wg, wu, wd)


def _proj_kernel(x_ref, g_ref, w_ref, o_ref, h_ref):
    @pl.when(pl.program_id(1) == 0)
    def _():
        h_ref[...] = _rms_bf16(x_ref[...], g_ref[...])

    o_ref[...] = jnp.dot(h_ref[...], w_ref[...], preferred_element_type=F32)


def in_projection(x, g, w):
    m = x.shape[0]
    return pl.pallas_call(
        _proj_kernel,
        grid=(m // ROW_TILE, N_IN_PAD // PROJ_TILE),
        in_specs=[
            pl.BlockSpec((ROW_TILE, D_MODEL), lambda i, n: (i, 0)),
            pl.BlockSpec((1, D_MODEL), lambda i, n: (0, 0)),
            pl.BlockSpec((D_MODEL, PROJ_TILE), lambda i, n: (0, n)),
        ],
        out_specs=pl.BlockSpec((ROW_TILE, PROJ_TILE), lambda i, n: (i, n)),
        out_shape=jax.ShapeDtypeStruct((m, N_IN_PAD), F32),
        scratch_shapes=[pltpu.VMEM((ROW_TILE, D_MODEL), BF16)],
        compiler_params=pltpu.CompilerParams(dimension_semantics=("parallel", "arbitrary"),
                                             vmem_limit_bytes=VMEM_LIMIT_BYTES),
        name="in_projection",
    )(x, g.reshape(1, D_MODEL), w)


def _merge_kernel(oa_ref, ob_ref, oc_ref, ga_ref, gb_ref, gc_ref, wa_ref, wb_ref, wc_ref, y_ref):
    ya = jnp.dot(oa_ref[...], wa_ref[...], preferred_element_type=F32)
    yb = jnp.dot(ob_ref[...], wb_ref[...], preferred_element_type=F32)
    yc = jnp.dot(oc_ref[...], wc_ref[...], preferred_element_type=F32)
    y = jax.nn.sigmoid(ga_ref[...]) * ya + jax.nn.sigmoid(gb_ref[...]) * yb + jax.nn.sigmoid(gc_ref[...]) * yc
    y_ref[...] = y.astype(BF16)


def merge_branches(o_gla, o_nsa, o_sb, gates, wa, wb, wc):
    m = o_gla.shape[0]
    nb = D_MODEL // OUT_TILE
    o_spec = lambda width: pl.BlockSpec((ROW_TILE, width), lambda i, n: (i, 0))
    g_spec = lambda k: pl.BlockSpec((ROW_TILE, OUT_TILE), lambda i, n, k=k: (i, Z_MG // OUT_TILE + n + k * nb))
    w_spec = lambda width: pl.BlockSpec((width, OUT_TILE), lambda i, n: (0, n))
    return pl.pallas_call(
        _merge_kernel,
        grid=(m // ROW_TILE, nb),
        in_specs=[o_spec(GLA_VAL), o_spec(NSA_OUT), o_spec(SB_OUT), g_spec(0), g_spec(1), g_spec(2),
                  w_spec(GLA_VAL), w_spec(NSA_OUT), w_spec(SB_OUT)],
        out_specs=pl.BlockSpec((ROW_TILE, OUT_TILE), lambda i, n: (i, n)),
        out_shape=jax.ShapeDtypeStruct((m, D_MODEL), BF16),
        compiler_params=pltpu.CompilerParams(dimension_semantics=("parallel", "arbitrary"),
                                             vmem_limit_bytes=VMEM_LIMIT_BYTES),
        name="merge_branches",
    )(o_gla, o_nsa, o_sb, gates, gates, gates, wa, wb, wc)


def _out_kernel(x_ref, y_ref, w_ref, o_ref):
    o_ref[...] = x_ref[...] + jnp.dot(y_ref[...], w_ref[...], preferred_element_type=F32)


def out_projection(x, y, w):
    m = x.shape[0]
    return pl.pallas_call(
        _out_kernel,
        grid=(m // ROW_TILE, D_MODEL // OUT_TILE),
        in_specs=[
            pl.BlockSpec((ROW_TILE, OUT_TILE), lambda i, n: (i, n)),
            pl.BlockSpec((ROW_TILE, D_MODEL), lambda i, n: (i, 0)),
            pl.BlockSpec((D_MODEL, OUT_TILE), lambda i, n: (0, n)),
        ],
        out_specs=pl.BlockSpec((ROW_TILE, OUT_TILE), lambda i, n: (i, n)),
        out_shape=jax.ShapeDtypeStruct((m, D_MODEL), F32),
        compiler_params=pltpu.CompilerParams(dimension_semantics=("parallel", "arbitrary"),
                                             vmem_limit_bytes=VMEM_LIMIT_BYTES),
        name="out_projection",
    )(x, y, w)


def _norm_kernel(x_ref, g_ref, o_ref):
    x = x_ref[...]
    o_ref[...] = x * lax.rsqrt(jnp.mean(x * x, axis=-1, keepdims=True) + NORM_EPS) * g_ref[...]


def final_rmsnorm(x, g):
    m = x.shape[0]
    return pl.pallas_call(
        _norm_kernel,
        grid=(m // ROW_TILE,),
        in_specs=[pl.BlockSpec((ROW_TILE, D_MODEL), lambda i: (i, 0)), pl.BlockSpec((1, D_MODEL), lambda i: (0, 0))],
        out_specs=pl.BlockSpec((ROW_TILE, D_MODEL), lambda i: (i, 0)),
        out_shape=jax.ShapeDtypeStruct((m, D_MODEL), F32),
        compiler_params=pltpu.CompilerParams(dimension_semantics=("parallel",), vmem_limit_bytes=VMEM_LIMIT_BYTES),
        name="final_rmsnorm",
    )(x, g.reshape(1, D_MODEL))


NSA_TQ = 128
NSA_TK = 256
SEL_TOEP = 4
WIN_TOEP = 6


def _rel_bucket(dist):
    n = jnp.maximum(dist, 0)
    nf = jnp.maximum(n, 1).astype(jnp.float32)
    large = RPB_MAX_EXACT + (jnp.log(nf / RPB_MAX_EXACT) / math.log(RPB_MAX_DIST / RPB_MAX_EXACT)
                             * (N_BUCKETS - RPB_MAX_EXACT)).astype(jnp.int32)
    return jnp.where(n < RPB_MAX_EXACT, n, jnp.minimum(large, N_BUCKETS - 1))


def _cmp_sel_map(nc, ns):
    c_start = jnp.arange(nc) * CMP_STRIDE
    s_start = jnp.arange(ns) * SEL_BLOCK
    hit = (c_start[:, None] < s_start[None, :] + SEL_BLOCK) & (c_start[:, None] + CMP_LEN > s_start[None, :])
    return hit.astype(jnp.float32)


def _split_hi_lo(x):
    hi = x.astype(BF16)
    lo = (x - hi.astype(F32)).astype(BF16)
    return hi, lo


def _bias_lookup(rpb, bucket):
    one_hot = jax.nn.one_hot(bucket, N_BUCKETS, dtype=F32)
    return jnp.einsum('...k,kh->...h', one_hot, rpb.astype(F32), precision=lax.Precision.HIGHEST)


def _toeplitz_bias(rpb, window):
    n = WIN_TOEP if window else SEL_TOEP
    w = jnp.arange(NSA_TK)[:, None]
    u = jnp.arange(NSA_TQ)[None, :]
    d = (jnp.arange(n) * NSA_TQ)[:, None, None] + (u - w)[None]
    b = _bias_lookup(rpb, _rel_bucket(d))
    ok = (d >= 0) & (d < WINDOW) if window else (d >= 0)
    b = jnp.where(ok[..., None], b, NEG).reshape(n, NSA_TK, NSA_TQ, NSA_KV_HEADS, NSA_GROUP)
    return b.transpose(3, 0, 1, 4, 2).reshape(NSA_KV_HEADS, n, NSA_TK, NSA_GROUP * NSA_TQ)


def _nsa_prompt_kernel(q_ref, gate_ref, cmp_ref, k_ref, vt_ref, posa_ref, posb_ref, w1a_ref, w1b_ref, w2_ref,
                         biasc_ref, stoep_ref, wtoep_ref, mapt_ref, expand_ref, o_ref,
                         kc_ref, vct_ref, acc_ref, *, n_cmp):
    qi = pl.program_id(2)
    tq, tk, R = NSA_TQ, NSA_TK, NSA_GROUP
    cols = R * tq
    nt = (((1,), (1,)), ((), ()))

    @pl.when(qi == 0)
    def _():
        for t in range(2):
            x = cmp_ref[0, 0, t]
            first = jnp.dot((x + posa_ref[t]).astype(BF16), w1a_ref[t], preferred_element_type=F32)
            second = jnp.dot((x + posb_ref[t]).astype(BF16), w1b_ref[t], preferred_element_type=F32)
            pre = first + pltpu.roll(second, second.shape[0] - 1, 0)
            hid = (pre * jax.nn.sigmoid(pre)).astype(BF16)
            c = jnp.dot(hid, w2_ref[t], preferred_element_type=F32)
            if t == 0:
                kc_ref[...] = c.astype(BF16)
            else:
                vct_ref[...] = c.T.astype(BF16)

    q4 = q_ref[0, 0].reshape(cols, NSA_D)
    lane_q = lax.broadcasted_iota(jnp.int32, (1, cols), 1) % tq
    q_pos = qi * tq + lane_q

    ncp = kc_ref.shape[0]
    s = lax.dot_general(kc_ref[...], q4, nt, preferred_element_type=F32) + biasc_ref[0, 0]
    n_idx = lax.broadcasted_iota(jnp.int32, (ncp, cols), 0)
    ok = ((n_idx * CMP_STRIDE + (CMP_LEN - 1)) <= q_pos) & (n_idx < n_cmp)
    s = jnp.where(ok, s, NEG)
    e = jnp.exp(s - jnp.max(s, axis=0, keepdims=True))
    p_c = jnp.where(ok, e / jnp.sum(e, axis=0, keepdims=True), 0.0)
    o_c = jnp.dot(vct_ref[...], p_c.astype(BF16), preferred_element_type=F32)
    psum = p_c[:, 0:tq]
    for r in range(1, R):
        psum = psum + p_c[:, r * tq:(r + 1) * tq]
    p_hi, p_lo = _split_hi_lo(psum)
    imp = (jnp.dot(mapt_ref[...], p_hi, preferred_element_type=F32)
           + jnp.dot(mapt_ref[...], p_lo, preferred_element_type=F32))

    ns = imp.shape[0]
    j_idx = lax.broadcasted_iota(jnp.int32, (ns, tq), 0)
    t_pos = qi * tq + lax.broadcasted_iota(jnp.int32, (ns, tq), 1)
    cur = t_pos // SEL_BLOCK
    valid = j_idx * SEL_BLOCK <= t_pos
    forced = (j_idx == 0) | (j_idx == cur) | (j_idx == cur - 1)
    score = jnp.where(valid, jnp.where(forced, FORCE, imp), NEG)
    rank = jnp.zeros((ns, tq), F32)
    for i in range(ns):
        row = score[i:i + 1, :]
        beats = (row > score) | ((row == score) & (j_idx > i))
        rank = rank + jnp.where(beats, 1.0, 0.0)
    sel_t = jnp.where((rank < float(min(N_SEL, ns))) & valid, 1.0, 0.0).astype(BF16)
    sel4 = jnp.concatenate([sel_t] * R, axis=1)

    def attend(idx, toep_ref, n_toep, j_lo, use_sel):
        acc_ref[...] = jnp.zeros(acc_ref.shape, F32)
        j_hi = qi // 2

        def body(step, carry):
            m_old, l_old = carry
            j = j_hi - step
            k = k_ref[0, 0, idx, pl.ds(pl.multiple_of(j * tk, tk), tk), :]
            vt = vt_ref[0, 0, idx, :, pl.ds(pl.multiple_of(j * tk, tk), tk)]
            off = jnp.minimum(qi - 2 * j, n_toep - 1)
            sc = lax.dot_general(k, q4, nt, preferred_element_type=F32) + toep_ref[0, off]
            if use_sel:
                hit = jnp.dot(expand_ref[j], sel4, preferred_element_type=F32)
                sc = jnp.where(hit > 0.5, sc, NEG)
            m_new = jnp.maximum(m_old, jnp.max(sc, axis=0, keepdims=True))
            alpha = jnp.exp(m_old - m_new)
            p = jnp.exp(sc - m_new)
            l_new = alpha * l_old + jnp.sum(p, axis=0, keepdims=True)
            acc_ref[...] = alpha * acc_ref[...] + jnp.dot(vt, p.astype(BF16), preferred_element_type=F32)
            return m_new, l_new

        init = (jnp.full((1, cols), NEG, F32), jnp.zeros((1, cols), F32))
        _, l_fin = lax.fori_loop(0, j_hi - j_lo + 1, body, init)
        return acc_ref[...] / l_fin

    o_s = attend(0, stoep_ref, SEL_TOEP, 0, True)
    o_w = attend(1, wtoep_ref, WIN_TOEP, jnp.maximum(qi * tq - (WINDOW - 1), 0) // tk, False)

    g = jax.nn.sigmoid(gate_ref[0, 0])
    for r in range(R):
        sl = slice(r * tq, (r + 1) * tq)
        o = (g[3 * r:3 * r + 1] * o_c[:, sl] + g[3 * r + 1:3 * r + 2] * o_s[:, sl]
             + g[3 * r + 2:3 * r + 3] * o_w[:, sl])
        o_ref[0, 0, r] = o.astype(o_ref.dtype)


def nsa_prompt(nq, ng, nkv, cmp_pos, cmp_w1, cmp_w2, rpb):
    B, T = nq.shape[:2]
    G, R, d = NSA_KV_HEADS, NSA_GROUP, NSA_D
    n_sub = T // CMP_STRIDE
    n_cmp = n_sub - 1
    ns = T // SEL_BLOCK
    nq_t = T // NSA_TQ
    nk_t = T // NSA_TK
    q = (nq * (d ** -0.5)).astype(BF16).reshape(B, T, G, R, d).transpose(0, 2, 3, 1, 4)
    gates = ng.reshape(B, T, G, 3 * R).transpose(0, 2, 3, 1)
    cmp_rows = nkv[:, :, 0:2].transpose(0, 3, 2, 1, 4).reshape(B, G, 2, n_sub, CMP_STRIDE * d)
    k = jnp.stack([nkv[:, :, 2].astype(BF16).transpose(0, 2, 1, 3),
                   nkv[:, :, 4].astype(BF16).transpose(0, 2, 1, 3)], axis=2)
    vt = jnp.stack([nkv[:, :, 3].astype(BF16).transpose(0, 2, 3, 1),
                    nkv[:, :, 5].astype(BF16).transpose(0, 2, 3, 1)], axis=2)
    pos_a = cmp_pos[:, :CMP_STRIDE].reshape(2, 1, CMP_STRIDE * d)
    pos_b = cmp_pos[:, CMP_STRIDE:].reshape(2, 1, CMP_STRIDE * d)
    w1a = cmp_w1[:, :CMP_STRIDE].reshape(2, CMP_STRIDE * d, CMP_HID).astype(BF16)
    w1b = cmp_w1[:, CMP_STRIDE:].reshape(2, CMP_STRIDE * d, CMP_HID).astype(BF16)
    w2 = cmp_w2.astype(BF16)
    t_pos = jnp.arange(T)[None, :]
    c_end = jnp.arange(n_sub)[:, None] * CMP_STRIDE + (CMP_LEN - 1)
    bias_c = _bias_lookup(rpb, _rel_bucket(t_pos - c_end))
    bias_c = bias_c.reshape(n_sub, nq_t, NSA_TQ, G, R).transpose(3, 1, 0, 4, 2).reshape(G, nq_t, n_sub, R * NSA_TQ)
    stoep = _toeplitz_bias(rpb, False)
    wtoep = _toeplitz_bias(rpb, True)
    map_t = _cmp_sel_map(n_sub, ns).T.astype(BF16)
    expand = ((jnp.arange(nk_t)[:, None, None] * NSA_TK + jnp.arange(NSA_TK)[None, :, None]) // SEL_BLOCK
              == jnp.arange(ns)[None, None, :]).astype(BF16)
    full = lambda a: pl.BlockSpec(a.shape, lambda b, g, i: (0,) * a.ndim)
    per_g = lambda a: pl.BlockSpec((1,) + a.shape[1:], lambda b, g, i: (g,) + (0,) * (a.ndim - 1))
    per_bg = lambda a: pl.BlockSpec((1, 1) + a.shape[2:], lambda b, g, i: (b, g) + (0,) * (a.ndim - 2))
    o = pl.pallas_call(
        functools.partial(_nsa_prompt_kernel, n_cmp=n_cmp),
        grid=(B, G, nq_t),
        in_specs=[
            pl.BlockSpec((1, 1, R, NSA_TQ, d), lambda b, g, i: (b, g, 0, i, 0)),
            pl.BlockSpec((1, 1, 3 * R, NSA_TQ), lambda b, g, i: (b, g, 0, i)),
            per_bg(cmp_rows), per_bg(k), per_bg(vt),
            full(pos_a), full(pos_b), full(w1a), full(w1b), full(w2),
            pl.BlockSpec((1, 1, n_sub, R * NSA_TQ), lambda b, g, i: (g, i, 0, 0)),
            per_g(stoep), per_g(wtoep), full(map_t), full(expand),
        ],
        out_specs=pl.BlockSpec((1, 1, R, d, NSA_TQ), lambda b, g, i: (b, g, 0, 0, i)),
        out_shape=jax.ShapeDtypeStruct((B, G, R, d, T), BF16),
        scratch_shapes=[pltpu.VMEM((n_sub, d), BF16), pltpu.VMEM((d, n_sub), BF16),
                        pltpu.VMEM((d, R * NSA_TQ), F32)],
        compiler_params=pltpu.CompilerParams(dimension_semantics=("parallel", "parallel", "arbitrary"),
                                             vmem_limit_bytes=VMEM_LIMIT_BYTES),
        name="nsa_prompt",
    )(q, gates, cmp_rows, k, vt, pos_a, pos_b, w1a, w1b, w2, bias_c, stoep, wtoep, map_t, expand)
    return o.transpose(0, 4, 1, 2, 3).reshape(B, T, G * R * d)


SB_TQ = 512
SB_TK = 256
SB_HEADS_PER_STEP = 4


def _softplus(z):
    return jnp.maximum(z, 0.0) + jnp.log(1.0 + jnp.exp(-jnp.abs(z)))


def _suffix_matrix(n):
    j = jnp.arange(n)[:, None]
    s = jnp.arange(n)[None, :]
    return jnp.concatenate([(j > s), jnp.ones((n, n), bool)], axis=1).astype(BF16)


def _sb_tile(z, ok, v, suffix, c_ref, acc_ref):
    tk = z.shape[1]
    sp = _softplus(z)
    lf = -sp if ok is None else jnp.where(ok, -sp, 0.0)
    hi, lo = _split_hi_lo(lf)
    tot = (jnp.dot(hi, suffix, preferred_element_type=F32) + jnp.dot(lo, suffix, preferred_element_type=F32))
    later = tot[:, :tk] + c_ref[...]
    a = jnp.exp(z - sp + later)
    if ok is not None:
        a = jnp.where(ok, a, 0.0)
    acc_ref[...] += jnp.dot(a.astype(BF16), v, preferred_element_type=F32)
    c_ref[...] += tot[:, tk:]


def _sb_prompt_kernel(q_ref, k_ref, v_ref, suffix_ref, o_ref, c_ref, acc_ref):
    qi = pl.program_id(2)
    tq, tk = SB_TQ, SB_TK
    per = tq // tk
    heads = q_ref.shape[1]
    suffix = suffix_ref[...]
    c_ref[...] = jnp.zeros(c_ref.shape, F32)
    acc_ref[...] = jnp.zeros(acc_ref.shape, F32)
    nt = (((1,), (1,)), ((), ()))
    q_pos = qi * tq + lax.broadcasted_iota(jnp.int32, (tq, tk), 0)
    k_off = lax.broadcasted_iota(jnp.int32, (tq, tk), 1)

    def tile(j, masked):
        ok = (j * tk + k_off < q_pos) if masked else None
        for h in range(heads):
            k = k_ref[0, h, pl.ds(pl.multiple_of(j * tk, tk), tk), :]
            v = v_ref[0, h, pl.ds(pl.multiple_of(j * tk, tk), tk), :]
            z = lax.dot_general(q_ref[0, h], k, nt, preferred_element_type=F32)
            _sb_tile(z, ok, v, suffix, c_ref.at[h], acc_ref.at[h])

    for t in range(per):
        tile(qi * per + (per - 1 - t), True)

    def body(step, carry):
        tile(qi * per - 1 - step, False)
        return carry

    lax.fori_loop(0, qi * per, body, 0)
    o_ref[0] = acc_ref[...].astype(o_ref.dtype)


def sb_prompt(sq, skv):
    B, T = sq.shape[:2]
    H, d, hp = SB_HEADS, SB_D, SB_HEADS_PER_STEP
    q = (sq * (d ** -0.5)).astype(BF16).reshape(B, T, H, d).transpose(0, 2, 1, 3)
    k = skv[:, :, 0].astype(BF16).transpose(0, 2, 1, 3)
    v = skv[:, :, 1].astype(BF16).transpose(0, 2, 1, 3)
    suffix = _suffix_matrix(SB_TK)
    kv_spec = pl.BlockSpec((1, hp, T, d), lambda b, h, i: (b, h, 0, 0))
    q_spec = pl.BlockSpec((1, hp, SB_TQ, d), lambda b, h, i: (b, h, i, 0))
    o = pl.pallas_call(
        _sb_prompt_kernel,
        grid=(B, H // hp, T // SB_TQ),
        in_specs=[q_spec, kv_spec, kv_spec, pl.BlockSpec(suffix.shape, lambda b, h, i: (0, 0))],
        out_specs=q_spec,
        out_shape=jax.ShapeDtypeStruct((B, H, T, d), BF16),
        scratch_shapes=[pltpu.VMEM((hp, SB_TQ, SB_TK), F32), pltpu.VMEM((hp, SB_TQ, d), F32)],
        compiler_params=pltpu.CompilerParams(dimension_semantics=("parallel", "parallel", "arbitrary"),
                                             vmem_limit_bytes=VMEM_LIMIT_BYTES),
        name="sb_prompt",
    )(q, k, v, suffix)
    return o.transpose(0, 2, 1, 3).reshape(B, T, H * d)


GLA_SUB = 16
GLA_EXP_CAP = 60.0


def _gla_head(q, k, v, a_low, r, wd, bd, gn, s_old, sub):
    C = q.shape[0]
    q = q * (GLA_DK ** -0.5)
    v = v.astype(BF16)
    pre = jnp.dot(a_low, wd, preferred_element_type=F32) + bd
    log_a = -_softplus(-pre) * (1.0 / GLA_TAU)
    row = lax.broadcasted_iota(jnp.int32, (C, C), 0)
    col = lax.broadcasted_iota(jnp.int32, (C, C), 1)
    lower = jnp.where(col <= row, 1.0, 0.0).astype(BF16)
    a_hi, a_lo = _split_hi_lo(log_a)
    b = jnp.dot(lower, a_hi, preferred_element_type=F32) + jnp.dot(lower, a_lo, preferred_element_type=F32)

    o = jnp.dot((q * jnp.exp(b)).astype(BF16), s_old.astype(BF16), preferred_element_type=F32)

    nt = (((1,), (1,)), ((), ()))
    parts = []
    for i in range(C // sub):
        lo, hi = i * sub, (i + 1) * sub
        ref_row = b[lo:lo + 1, :]
        qs = (q[lo:hi] * jnp.exp(b[lo:hi] - ref_row)).astype(BF16)
        ks = (k[:hi] * jnp.exp(jnp.minimum(ref_row - b[:hi], GLA_EXP_CAP))).astype(BF16)
        att = lax.dot_general(qs, ks, nt, preferred_element_type=F32)
        t_idx = lo + lax.broadcasted_iota(jnp.int32, (sub, hi), 0)
        s_idx = lax.broadcasted_iota(jnp.int32, (sub, hi), 1)
        att = jnp.where(s_idx <= t_idx, att, 0.0)
        parts.append(jnp.dot(att.astype(BF16), v[:hi], preferred_element_type=F32))
    o = o + (parts[0] if len(parts) == 1 else jnp.concatenate(parts, axis=0))

    b_t = b.T
    last = b_t[:, C - 1:C]
    kd_t = (k.T * jnp.exp(last - b_t)).astype(BF16)
    s_new = jnp.exp(last) * s_old + jnp.dot(kd_t, v, preferred_element_type=F32)

    y = o * lax.rsqrt(jnp.mean(o * o, axis=-1, keepdims=True) + NORM_EPS) * gn
    return y * (r * jax.nn.sigmoid(r)), s_new


def _gla_kernel(q_ref, k_ref, v_ref, a_ref, r_ref, wd_ref, bd_ref, gn_ref, s0_ref, o_ref, st_ref, s_ref, *, sub):
    c = pl.program_id(1)
    dk, dv = GLA_DK, GLA_DV

    @pl.when(c == 0)
    def _():
        s_ref[...] = s0_ref[0]

    a_low = a_ref[...].astype(BF16)
    for h in range(GLA_HEADS):
        kc = slice(h * dk, (h + 1) * dk)
        vc = slice(h * dv, (h + 1) * dv)
        o, s_new = _gla_head(q_ref[:, kc], k_ref[:, kc], v_ref[:, vc], a_low, r_ref[:, vc], wd_ref[:, kc],
                             bd_ref[:, kc], gn_ref[...], s_ref[h], sub)
        s_ref[h] = s_new
        o_ref[0, :, vc] = o.astype(o_ref.dtype)

    @pl.when(c == pl.num_programs(1) - 1)
    def _():
        st_ref[0] = s_ref[...]


def gla_mixer(z, row0, B, T, s0, w_decay, b_decay, out_norm):
    H, dk, dv = GLA_HEADS, GLA_DK, GLA_DV
    C = min(GLA_CHUNK, T)
    sub = min(GLA_SUB, C)
    assert row0 % C == 0 and T % C == 0
    n_c = T // C
    tok = lambda w, col0: pl.BlockSpec((C, w), lambda b, c: (row0 // C + b * n_c + c, col0 // w))
    full = lambda shape: pl.BlockSpec(shape, lambda b, c: (0,) * len(shape))
    state = pl.BlockSpec((1, H, dk, dv), lambda b, c: (b, 0, 0, 0))
    wd = jnp.zeros((LANES, H * dk), BF16).at[Z_GA - Z_NG:Z_GA - Z_NG + GLA_RANK].set(w_decay.astype(BF16))
    o, st = pl.pallas_call(
        functools.partial(_gla_kernel, sub=sub),
        grid=(B, n_c),
        in_specs=[tok(H * dk, Z_GQ), tok(H * dk, Z_GK), tok(H * dv, Z_GV), tok(LANES, Z_NG), tok(H * dv, Z_GR),
                  full((LANES, H * dk)), full((1, H * dk)), full((1, dv)), state],
        out_specs=[pl.BlockSpec((1, C, H * dv), lambda b, c: (b, c, 0)), state],
        out_shape=[jax.ShapeDtypeStruct((B, T, H * dv), BF16), jax.ShapeDtypeStruct((B, H, dk, dv), F32)],
        scratch_shapes=[pltpu.VMEM((H, dk, dv), F32)],
        compiler_params=pltpu.CompilerParams(dimension_semantics=("parallel", "arbitrary"),
                                             vmem_limit_bytes=VMEM_LIMIT_BYTES),
        name="gla_mixer",
    )(z, z, z, z, z, wd, b_decay.reshape(1, H * dk), out_norm.reshape(1, dv), s0)
    return o, st


def _sb_sample_kernel(pt_ref, q_ref, new_ref, *refs, n_pg, n_new):
    page_refs = refs[:n_pg]
    suffix_ref, o_ref, c_ref, acc_ref = refs[n_pg:]
    s = pl.program_id(1)
    H, d = SB_HEADS, SB_D
    rows = q_ref.shape[1]
    nt = (((1,), (1,)), ((), ()))

    def tile(blk, ok):
        k = blk[:, :H * d].astype(BF16)
        v = blk[:, H * d:].astype(BF16)
        z = lax.dot_general(q_ref[0], k, nt, preferred_element_type=F32)
        _sb_tile(z, ok, v, suffix_ref[...], c_ref, acc_ref)

    @pl.when(s == 0)
    def _():
        c_ref[...] = jnp.zeros(c_ref.shape, F32)
        acc_ref[...] = jnp.zeros(acc_ref.shape, F32)
        t_idx = lax.broadcasted_iota(jnp.int32, c_ref.shape, 0) % n_new
        k_idx = lax.broadcasted_iota(jnp.int32, c_ref.shape, 1)
        tile(new_ref[0], k_idx < t_idx)

    @pl.when(s > 0)
    def _():
        tile(jnp.concatenate([r[0, 0] for r in page_refs], axis=0), None)

    @pl.when(s == pl.num_programs(1) - 1)
    def _():
        acc = acc_ref[...]
        r_idx = lax.broadcasted_iota(jnp.int32, acc.shape, 0) // n_new
        l_idx = lax.broadcasted_iota(jnp.int32, acc.shape, 1) // d
        own = jnp.where(r_idx == l_idx, acc, 0.0)
        o_ref[0] = jnp.sum(own.reshape(H, n_new, H * d), axis=0).astype(o_ref.dtype)


def sb_sample(sq, skv, cache_sb, page_table, layer):
    B, tn = sq.shape[:2]
    H, d = SB_HEADS, SB_D
    n_pages = page_table.shape[1]
    page = cache_sb.shape[2]
    cache = cache_sb.reshape(cache_sb.shape[0], cache_sb.shape[1], page, 2 * H * d)
    q = (sq * (d ** -0.5)).astype(BF16).reshape(B, tn, H, d).transpose(0, 2, 1, 3)
    q_blk = (q[:, :, :, None, :] * jnp.eye(H, dtype=BF16)[None, :, None, :, None]).reshape(B, H * tn, H * d)
    n_pg = 2 if n_pages % 2 == 0 else 1
    keys = n_pg * page
    new = jnp.pad(skv.reshape(B, tn, 2 * H * d), ((0, 0), (0, keys - tn), (0, 0)))
    suffix = _suffix_matrix(keys)
    grid_spec = pltpu.PrefetchScalarGridSpec(
        num_scalar_prefetch=1,
        grid=(B, n_pages // n_pg + 1),
        in_specs=[
            pl.BlockSpec((1, H * tn, H * d), lambda b, s, pt: (b, 0, 0)),
            pl.BlockSpec((1, keys, 2 * H * d), lambda b, s, pt: (b, 0, 0))] + [
            pl.BlockSpec((1, 1, page, 2 * H * d),
                         lambda b, s, pt, i=i: (pt[b, n_pages - jnp.maximum(s, 1) * n_pg + i], layer, 0, 0))
            for i in range(n_pg)] + [
            pl.BlockSpec(suffix.shape, lambda b, s, pt: (0, 0)),
        ],
        out_specs=pl.BlockSpec((1, tn, H * d), lambda b, s, pt: (b, 0, 0)),
        scratch_shapes=[pltpu.VMEM((H * tn, keys), F32), pltpu.VMEM((H * tn, H * d), F32)],
    )
    return pl.pallas_call(
        functools.partial(_sb_sample_kernel, n_pg=n_pg, n_new=tn),
        grid_spec=grid_spec,
        out_shape=jax.ShapeDtypeStruct((B, tn, H * d), BF16),
        compiler_params=pltpu.CompilerParams(dimension_semantics=("parallel", "arbitrary"),
                                             vmem_limit_bytes=VMEM_LIMIT_BYTES),
        name="sb_sample",
    )(page_table, q_blk, new, *([cache] * n_pg), suffix)


def _nsa_sample_cmp_kernel(pt_ref, q_ref, *refs, n_pg, n_new, past_len):
    page_refs = refs[:n_pg]
    posa_ref, posb_ref, w1_ref, w2_ref, biasc_ref, mapt_ref, oc_ref, sel_ref, x_ref = refs[n_pg:]
    s = pl.program_id(1)
    page = page_refs[0].shape[2]
    G, R = NSA_KV_HEADS, NSA_GROUP
    gw = G * NSA_D
    per = gw // LANES
    for i, page_ref in enumerate(page_refs):
        row0 = pl.multiple_of((s * n_pg + i) * page, page)
        for c in range(x_ref.shape[0]):
            x_ref[c, pl.ds(row0, page), :] = page_ref[0, 0, :, c * LANES:(c + 1) * LANES]

    @pl.when(s == pl.num_programs(1) - 1)
    def _():
        n_sub = x_ref.shape[1] // CMP_STRIDE
        n_cmp = n_sub - 1
        kv = []
        for t in range(2):
            first = jnp.zeros((n_sub, gw), F32)
            second = jnp.zeros((n_sub, gw), F32)
            for l in range(CMP_STRIDE):
                x = jnp.concatenate([x_ref[t * per + c, pl.ds(l, n_sub, stride=CMP_STRIDE), :] for c in range(per)],
                                    axis=1)
                first += jnp.dot((x + posa_ref[t, l]).astype(BF16), w1_ref[t, l], preferred_element_type=F32)
                second += jnp.dot((x + posb_ref[t, l]).astype(BF16), w1_ref[t, CMP_STRIDE + l],
                                  preferred_element_type=F32)
            pre = first + pltpu.roll(second, n_sub - 1, 0)
            hid = (pre * jax.nn.sigmoid(pre)).astype(BF16)
            kv.append(jnp.dot(hid, w2_ref[t], preferred_element_type=F32).astype(BF16))
        kc, vc = kv
        nt = (((1,), (1,)), ((), ()))
        sc = lax.dot_general(q_ref[0], kc, nt, preferred_element_type=F32) + biasc_ref[...]
        ok = lax.broadcasted_iota(jnp.int32, sc.shape, 1) < n_cmp
        sc = jnp.where(ok, sc, NEG)
        e = jnp.exp(sc - jnp.max(sc, axis=-1, keepdims=True))
        p_c = jnp.where(ok, e / jnp.sum(e, axis=-1, keepdims=True), 0.0)
        oc_ref[0] = jnp.dot(p_c.astype(BF16), vc, preferred_element_type=F32)
        psum = jnp.sum(p_c.reshape(G, R, n_new, n_sub), axis=1).reshape(G * n_new, n_sub)
        p_hi, p_lo = _split_hi_lo(psum)
        imp = (lax.dot_general(mapt_ref[...], p_hi, nt, preferred_element_type=F32)
               + lax.dot_general(mapt_ref[...], p_lo, nt, preferred_element_type=F32))
        ns_pad, cols = imp.shape
        ns = past_len // SEL_BLOCK + 1
        j_idx = lax.broadcasted_iota(jnp.int32, imp.shape, 0)
        t_pos = past_len + lax.broadcasted_iota(jnp.int32, imp.shape, 1) % n_new
        cur = t_pos // SEL_BLOCK
        valid = (j_idx * SEL_BLOCK <= t_pos) & (j_idx < ns)
        forced = (j_idx == 0) | (j_idx == cur) | (j_idx == cur - 1)
        score = jnp.where(valid, jnp.where(forced, FORCE, imp), NEG)

        def rank_step(i, rank):
            row = jnp.sum(jnp.where(j_idx == i, score, 0.0), axis=0, keepdims=True)
            beats = (row > score) | ((row == score) & (j_idx > i))
            return rank + jnp.where(beats, 1.0, 0.0)

        rank = lax.fori_loop(0, ns, rank_step, jnp.zeros(imp.shape, F32))
        sel_t = jnp.where((rank < float(min(N_SEL, ns))) & valid, 1.0, 0.0)
        sel = sel_t.T.reshape(G, 1, n_new, ns_pad)
        sel_ref[0] = jnp.broadcast_to(sel, (G, R, n_new, ns_pad)).reshape(G * R * n_new, ns_pad).astype(BF16)


def _two_part_softmax_av(s_a, v_a, s_b, v_b):
    m = jnp.maximum(jnp.max(s_a, axis=-1, keepdims=True), jnp.max(s_b, axis=-1, keepdims=True))
    p_a = jnp.exp(s_a - m)
    p_b = jnp.exp(s_b - m)
    den = jnp.sum(p_a, axis=-1, keepdims=True) + jnp.sum(p_b, axis=-1, keepdims=True)
    num = (jnp.dot(p_a.astype(BF16), v_a, preferred_element_type=F32)
           + jnp.dot(p_b.astype(BF16), v_b, preferred_element_type=F32))
    return num / den


def _nsa_sample_attn_kernel(pt_ref, q_ref, sel_ref, oc_ref, gate_ref, new_ref, win_ref, *refs, n_pg, n_pages):
    page_refs = refs[:n_pg]
    bnew_ref, bwin_ref, blast_ref, bfar_ref, o_ref, m_ref, l_ref, acc_ref, ow_ref = refs[n_pg:]
    s = pl.program_id(1)
    G, d = NSA_KV_HEADS, NSA_D
    gw = G * d
    page = page_refs[0].shape[2]
    rows = q_ref.shape[1]
    q = q_ref[0]
    nt = (((1,), (1,)), ((), ()))

    def online(sc, v):
        m_old = m_ref[...]
        m_new = jnp.maximum(m_old, jnp.max(sc, axis=-1, keepdims=True))
        alpha = jnp.exp(m_old - m_new)
        p = jnp.exp(sc - m_new)
        l_ref[...] = alpha * l_ref[...] + jnp.sum(p, axis=-1, keepdims=True)
        acc_ref[...] = alpha * acc_ref[...] + jnp.dot(p.astype(BF16), v, preferred_element_type=F32)
        m_ref[...] = m_new

    @pl.when(s == 0)
    def _():
        m_ref[...] = jnp.full(m_ref.shape, NEG, F32)
        l_ref[...] = jnp.zeros(l_ref.shape, F32)
        acc_ref[...] = jnp.zeros(acc_ref.shape, F32)
        new = new_ref[0].astype(BF16)
        online(lax.dot_general(q, new[:, :gw], nt, preferred_element_type=F32) + bnew_ref[...], new[:, gw:2 * gw])
        win = win_ref[0, 0].astype(BF16)
        s_w = lax.dot_general(q, win[:, :gw], nt, preferred_element_type=F32) + bwin_ref[...]
        s_n = lax.dot_general(q, new[:, 2 * gw:3 * gw], nt, preferred_element_type=F32) + bnew_ref[...]
        ow_ref[...] = _two_part_softmax_av(s_w, win[:, gw:], s_n, new[:, 3 * gw:])

    @pl.when(s > 0)
    def _():
        pg = n_pages - s * n_pg
        blk = jnp.concatenate([r[0, 0].astype(BF16) for r in page_refs], axis=0)
        keys = blk.shape[0]
        sc = lax.dot_general(q, blk[:, :gw], nt, preferred_element_type=F32)
        sc = sc + jnp.where(s == 1, blast_ref[...], bfar_ref[...])
        ns_pad = sel_ref.shape[2]
        blk_idx = lax.broadcasted_iota(jnp.int32, (ns_pad, keys), 0)
        key_blk = pg * (page // SEL_BLOCK) + lax.broadcasted_iota(jnp.int32, (ns_pad, keys), 1) // SEL_BLOCK
        expand = jnp.where(blk_idx == key_blk, 1.0, 0.0).astype(BF16)
        hit = jnp.dot(sel_ref[0], expand, preferred_element_type=F32)
        online(jnp.where(hit > 0.5, sc, NEG), blk[:, gw:])

    @pl.when(s == pl.num_programs(1) - 1)
    def _():
        g = jax.nn.sigmoid(gate_ref[0])
        o = g[:, 0:1] * oc_ref[0] + g[:, 1:2] * (acc_ref[...] / l_ref[...]) + g[:, 2:3] * ow_ref[...]
        r_grp = lax.broadcasted_iota(jnp.int32, o.shape, 0) // (rows // G)
        l_grp = lax.broadcasted_iota(jnp.int32, o.shape, 1) // d
        own = jnp.where(r_grp == l_grp, o, 0.0)
        out = own[:, 0:d]
        for gi in range(1, G):
            out = out + own[:, gi * d:(gi + 1) * d]
        o_ref[0] = out.astype(o_ref.dtype)


def nsa_sample(nq, ng, nkv, cache_nsa, cache_win_l, page_table, layer, cmp_pos, cmp_w1, cmp_w2, rpb):
    B, tn = nq.shape[:2]
    G, R, d = NSA_KV_HEADS, NSA_GROUP, NSA_D
    gw = G * d
    n_pages = page_table.shape[1]
    page = cache_nsa.shape[2]
    past_len = n_pages * page
    assert tn < CMP_STRIDE and page % SEL_BLOCK == 0 and past_len % SEL_BLOCK == 0
    wb = cache_win_l.shape[1]
    n_sub = past_len // CMP_STRIDE
    ns = past_len // SEL_BLOCK + 1
    ns_pad = -(-ns // LANES) * LANES
    rows = G * R * tn
    cache = cache_nsa.reshape(cache_nsa.shape[:2] + (page, 4 * gw))
    q = (nq * (d ** -0.5)).astype(BF16).reshape(B, tn, G, R, d).transpose(0, 2, 3, 1, 4)
    q_blk = (q[:, :, :, :, None, :] * jnp.eye(G, dtype=BF16)[None, :, None, None, :, None]).reshape(B, rows, gw)
    gates = ng.reshape(B, tn, G, R, 3).transpose(0, 2, 3, 1, 4).reshape(B, rows, 3)
    new = jnp.pad(nkv[:, :, 2:6].reshape(B, tn, 4 * gw), ((0, 0), (0, page - tn), (0, 0)))
    win = cache_win_l.reshape(B, 1, wb, 2 * gw)
    eye_g = jnp.eye(G, dtype=F32)
    w1 = jnp.einsum('tlde,gh->tlgdhe', cmp_w1, eye_g).reshape(2, CMP_LEN, gw, G * CMP_HID).astype(BF16)
    w2 = jnp.einsum('ted,gh->tgehd', cmp_w2, eye_g).reshape(2, G * CMP_HID, gw).astype(BF16)
    pos = jnp.tile(cmp_pos[:, :, None, :], (1, 1, G, 1)).reshape(2, CMP_LEN, 1, gw)
    pos_a, pos_b = pos[:, :CMP_STRIDE], pos[:, CMP_STRIDE:]
    table = rpb.astype(F32).reshape(N_BUCKETS, G, R).transpose(1, 2, 0)
    q_pos = past_len + jnp.arange(tn)

    def bias_rows(k_pos, ok):
        dist = q_pos[:, None] - k_pos[None, :]
        b = jnp.take(table, _rel_bucket(dist), axis=2)
        return jnp.where(ok(dist, k_pos[None, :]), b, NEG).reshape(rows, k_pos.shape[0])

    c_end = jnp.arange(n_sub) * CMP_STRIDE + (CMP_LEN - 1)
    bias_c = bias_rows(c_end, lambda dist, kp: dist >= 0)
    new_pos = past_len + jnp.arange(page)
    bias_new = bias_rows(new_pos, lambda dist, kp: (dist >= 0) & (kp < past_len + tn))
    win_pos = past_len - wb + jnp.arange(wb)
    bias_win = bias_rows(win_pos, lambda dist, kp: (dist >= 0) & (dist < WINDOW) & (kp >= 0))
    n_pg = next(n for n in (4, 2, 1) if n_pages % n == 0)
    bias_last = bias_rows(past_len - n_pg * page + jnp.arange(n_pg * page), lambda dist, kp: dist >= 0)
    bias_far = jnp.broadcast_to(table[:, :, None, N_BUCKETS - 1:], (G, R, tn, 1)).reshape(rows, 1)
    map_t = jnp.pad(_cmp_sel_map(n_sub, ns).T, ((0, ns_pad - ns), (0, 0))).astype(BF16)

    const = lambda a: pl.BlockSpec(a.shape, lambda b, s, pt: (0,) * a.ndim)
    per_b = lambda a: pl.BlockSpec((1,) + a.shape[1:], lambda b, s, pt: (b,) + (0,) * (a.ndim - 1))
    params = pltpu.CompilerParams(dimension_semantics=("parallel", "arbitrary"), vmem_limit_bytes=VMEM_LIMIT_BYTES)
    o_c, sel = pl.pallas_call(
        functools.partial(_nsa_sample_cmp_kernel, n_pg=n_pg, n_new=tn, past_len=past_len),
        grid_spec=pltpu.PrefetchScalarGridSpec(
            num_scalar_prefetch=1,
            grid=(B, n_pages // n_pg),
            in_specs=[per_b(q_blk)] + [
                pl.BlockSpec((1, 1, page, 2 * gw), lambda b, s, pt, i=i: (pt[b, s * n_pg + i], layer, 0, 0))
                for i in range(n_pg)] + [
                const(pos_a), const(pos_b), const(w1), const(w2), const(bias_c), const(map_t),
            ],
            out_specs=[pl.BlockSpec((1, rows, gw), lambda b, s, pt: (b, 0, 0)),
                       pl.BlockSpec((1, rows, ns_pad), lambda b, s, pt: (b, 0, 0))],
            scratch_shapes=[pltpu.VMEM((2 * gw // LANES, past_len, LANES), F32)],
        ),
        out_shape=[jax.ShapeDtypeStruct((B, rows, gw), F32), jax.ShapeDtypeStruct((B, rows, ns_pad), BF16)],
        compiler_params=params,
        name="nsa_sample_cmp",
    )(page_table, q_blk, *([cache] * n_pg), pos_a, pos_b, w1, w2, bias_c, map_t)
    o = pl.pallas_call(
        functools.partial(_nsa_sample_attn_kernel, n_pg=n_pg, n_pages=n_pages),
        grid_spec=pltpu.PrefetchScalarGridSpec(
            num_scalar_prefetch=1,
            grid=(B, n_pages // n_pg + 1),
            in_specs=[per_b(q_blk), per_b(sel), per_b(o_c), per_b(gates), per_b(new), per_b(win)] + [
                pl.BlockSpec((1, 1, page, 2 * gw),
                             lambda b, s, pt, i=i: (pt[b, n_pages - jnp.maximum(s, 1) * n_pg + i], layer, 0, 1))
                for i in range(n_pg)] + [
                const(bias_new), const(bias_win), const(bias_last), const(bias_far),
            ],
            out_specs=pl.BlockSpec((1, rows, d), lambda b, s, pt: (b, 0, 0)),
            scratch_shapes=[pltpu.VMEM((rows, 1), F32), pltpu.VMEM((rows, 1), F32), pltpu.VMEM((rows, gw), F32),
                            pltpu.VMEM((rows, gw), F32)],
        ),
        out_shape=jax.ShapeDtypeStruct((B, rows, d), BF16),
        compiler_params=params,
        name="nsa_sample_attn",
    )(page_table, q_blk, sel, o_c, gates, new, win, *([cache] * n_pg), bias_new, bias_win, bias_last, bias_far)
    return o.reshape(B, G * R, tn, d).transpose(0, 2, 1, 3).reshape(B, tn, G * R * d)


def _permute_w_in(w):
    c = [0] + [int(v) for v in np.cumsum(IN_SPLITS)]
    order = (0, 1, 2, 4, 5, 6, 8, 9, 10, 7, 3)
    parts = [w[:, c[i]:c[i + 1]] for i in order]
    return jnp.pad(jnp.concatenate(parts, axis=1), ((0, 0), (0, N_IN_PAD - N_IN)))


def _attn_parts(z, B, T):
    cut = lambda lo, w: z[:, lo:lo + w].reshape(B, T, w)
    nkv = cut(Z_NKV, 6 * NSA_KVW).reshape(B, T, 6, NSA_KV_HEADS, NSA_D)
    skv = cut(Z_SKV, 2 * SB_OUT).reshape(B, T, 2, SB_HEADS, SB_D)
    return cut(Z_NQ, NSA_OUT), nkv, cut(Z_NG, 3 * NSA_HEADS), cut(Z_SQ, SB_OUT), skv


def _mix_prompt(z, B, T, gla_w_decay, gla_b_decay, gla_out_norm, cmp_pos, cmp_w1, cmp_w2, rpb):
    nq, nkv, ng, sq, skv = _attn_parts(z[:B * T], B, T)
    s0 = jnp.zeros((B, GLA_HEADS, GLA_DK, GLA_DV), jnp.float32)
    o_gla, gla_state = gla_mixer(z, 0, B, T, s0, gla_w_decay, gla_b_decay, gla_out_norm)
    o_nsa = nsa_prompt(nq, ng, nkv, cmp_pos, cmp_w1, cmp_w2, rpb)
    o_sb = sb_prompt(sq, skv)
    keep = min(WINDOW, T)
    return (o_gla, o_nsa, o_sb), nkv[:, :, :4], skv, nkv[:, T - keep:, 4:], gla_state


def _mix_sample(z, row0, B, T, cache_nsa, cache_sb, page_table, layer, win_buf, s0, gla_w_decay, gla_b_decay,
                gla_out_norm, cmp_pos, cmp_w1, cmp_w2, rpb):
    nq, nkv, ng, sq, skv = _attn_parts(z[row0:row0 + B * T], B, T)
    o_gla, gla_state = gla_mixer(z, row0, B, T, s0, gla_w_decay, gla_b_decay, gla_out_norm)
    o_nsa = nsa_sample(nq, ng, nkv, cache_nsa, win_buf, page_table, layer, cmp_pos, cmp_w1, cmp_w2, rpb)
    win = jnp.concatenate([win_buf, nkv[:, :, 4:]], axis=1)
    wb = win_buf.shape[1]
    o_sb = sb_sample(sq, skv, cache_sb, page_table, layer)
    keep = min(WINDOW, wb + T)
    return (o_gla, o_nsa, o_sb), nkv[:, :, :4], skv, win[:, wb + T - keep:], gla_state


def kernel(x_prompt, x_sample, cache_nsa, cache_sb, cache_win, state_gla, page_table, ffn1_norm, ffn1_w_gate, ffn1_w_up, ffn1_w_down, mix_norm, w_in, gla_w_decay, gla_b_decay, gla_out_norm, nsa_cmp_pos, nsa_cmp_w1, nsa_cmp_w2, rpb_table, w_branch_gla, w_branch_nsa, w_branch_sb, w_out, ffn2_norm, ffn2_w_gate, ffn2_w_up, ffn2_w_down, final_norm):
    bp, tp, _ = x_prompt.shape
    bs, ts, _ = x_sample.shape
    mp, ms = bp * tp, bs * ts
    dec_batch, n_pages = page_table.shape
    past_len = n_pages * cache_nsa.shape[2]
    x = jnp.concatenate([x_prompt.reshape(mp, D_MODEL), x_sample.reshape(ms, D_MODEL)], axis=0)
    outs = [[] for _ in range(8)]
    for l in range(DEPTH):
        x = ffn_half_step(x, ffn1_norm[l], ffn1_w_gate[l].astype(BF16), ffn1_w_up[l].astype(BF16),
                          ffn1_w_down[l].astype(BF16))
        z = in_projection(x, mix_norm[l], _permute_w_in(w_in[l].astype(BF16)))
        mix_w = (gla_w_decay[l], gla_b_decay[l], gla_out_norm[l], nsa_cmp_pos[l], nsa_cmp_w1[l], nsa_cmp_w2[l],
                 rpb_table)
        br_p, r_nsa, r_sb, r_win, r_gla = _mix_prompt(z, bp, tp, *mix_w)
        for lst, val in zip(outs[0::2], (r_nsa, r_sb, r_win, r_gla)):
            lst.append(val)
        br_s, r_nsa, r_sb, r_win, r_gla = _mix_sample(z, mp, bs, ts, cache_nsa, cache_sb, page_table, l,
                                                      cache_win[:, l], state_gla[:, l], *mix_w)
        for lst, val in zip(outs[1::2], (r_nsa, r_sb, r_win, r_gla)):
            lst.append(val)
        o_gla, o_nsa, o_sb = (
            jnp.concatenate([a.reshape(mp, a.shape[-1]), b.reshape(ms, b.shape[-1])], axis=0)
            for a, b in zip(br_p, br_s))
        y = merge_branches(o_gla, o_nsa, o_sb, z,
                           w_branch_gla[l].astype(BF16), w_branch_nsa[l].astype(BF16), w_branch_sb[l].astype(BF16))
        x = out_projection(x, y, w_out[l].astype(BF16))
        x = ffn_half_step(x, ffn2_norm[l], ffn2_w_gate[l].astype(BF16), ffn2_w_up[l].astype(BF16),
                          ffn2_w_down[l].astype(BF16))
    y = final_rmsnorm(x, final_norm)
    return (y[:mp].reshape(bp, tp, D_MODEL), y[mp:].reshape(bs, ts, D_MODEL)) + tuple(
        jnp.stack(o, axis=1) for o in outs)
```

```python
import functools
import math

import jax
import jax.numpy as jnp
import numpy as np
from jax import lax
from jax.experimental import pallas as pl
from jax.experimental.pallas import tpu as pltpu

D_MODEL = 2048
DEPTH = 2
D_FF = 5632
NORM_EPS = 1e-6
GLA_HEADS = 4
GLA_DK = 128
GLA_DV = 256
GLA_RANK = 16
GLA_TAU = 16.0
GLA_CHUNK = 64
GLA_KEY = GLA_HEADS * GLA_DK
GLA_VAL = GLA_HEADS * GLA_DV
NSA_HEADS = 16
NSA_KV_HEADS = 4
NSA_GROUP = NSA_HEADS // NSA_KV_HEADS
NSA_D = 64
NSA_OUT = NSA_HEADS * NSA_D
NSA_KVW = NSA_KV_HEADS * NSA_D
CMP_STRIDE = 16
CMP_LEN = 2 * CMP_STRIDE
CMP_HID = 64
SEL_BLOCK = 64
N_SEL = 16
WINDOW = 512
NSA_QBLK = 32
SB_HEADS = 16
SB_D = 64
SB_OUT = SB_HEADS * SB_D
SB_QBLK = 128
N_BUCKETS = 32
RPB_MAX_EXACT = 16
RPB_MAX_DIST = 128
NEG = -1e30
FORCE = 1e9
IN_SPLITS = (GLA_KEY, GLA_KEY, GLA_VAL, GLA_RANK, GLA_VAL, NSA_OUT, 6 * NSA_KVW, 3 * NSA_HEADS, SB_OUT, 2 * SB_OUT,
             3 * D_MODEL)
N_IN = sum(IN_SPLITS)

BF16 = jnp.bfloat16
F32 = jnp.float32

VMEM_LIMIT_BYTES = 56 * 1024 * 1024
LANES = 128
ROW_TILE = 640
FF_TILE = 512
PROJ_TILE = 1152
N_IN_PAD = 14976
OUT_TILE = 512
Z_GQ, Z_GK, Z_GV, Z_GR = 0, GLA_KEY, 2 * GLA_KEY, 2 * GLA_KEY + GLA_VAL
Z_NQ = Z_GR + GLA_VAL
Z_NKV = Z_NQ + NSA_OUT
Z_SQ = Z_NKV + 6 * NSA_KVW
Z_SKV = Z_SQ + SB_OUT
Z_MG = Z_SKV + 2 * SB_OUT
Z_NG = Z_MG + 3 * D_MODEL
Z_GA = Z_NG + 3 * NSA_HEADS
assert Z_GA + GLA_RANK == N_IN and Z_NG % LANES == 0 and Z_MG % OUT_TILE == 0


def _rms_bf16(x, g):
    y = x * lax.rsqrt(jnp.mean(x * x, axis=-1, keepdims=True) + NORM_EPS)
    return (y * g).astype(BF16)


def _ffn_kernel(x_ref, g_ref, wg_ref, wu_ref, wd_ref, o_ref, h_ref, acc_ref):
    f = pl.program_id(1)

    @pl.when(f == 0)
    def _():
        h_ref[...] = _rms_bf16(x_ref[...], g_ref[...])
        acc_ref[...] = jnp.zeros_like(acc_ref)

    h = h_ref[...]
    a = jnp.dot(h, wg_ref[...], preferred_element_type=F32)
    b = jnp.dot(h, wu_ref[...], preferred_element_type=F32)
    act = (a * jax.nn.sigmoid(a) * b).astype(BF16)
    acc_ref[...] += jnp.dot(act, wd_ref[...], preferred_element_type=F32)

    @pl.when(f == pl.num_programs(1) - 1)
    def _():
        o_ref[...] = x_ref[...] + 0.5 * acc_ref[...]


def ffn_half_step(x, g, wg, wu, wd):
    m = x.shape[0]
    return pl.pallas_call(
        _ffn_kernel,
        grid=(m // ROW_TILE, D_FF // FF_TILE),
        in_specs=[
            pl.BlockSpec((ROW_TILE, D_MODEL), lambda i, f: (i, 0)),
            pl.BlockSpec((1, D_MODEL), lambda i, f: (0, 0)),
            pl.BlockSpec((D_MODEL, FF_TILE), lambda i, f: (0, f)),
            pl.BlockSpec((D_MODEL, FF_TILE), lambda i, f: (0, f)),
            pl.BlockSpec((FF_TILE, D_MODEL), lambda i, f: (f, 0)),
        ],
        out_specs=pl.BlockSpec((ROW_TILE, D_MODEL), lambda i, f: (i, 0)),
        out_shape=jax.ShapeDtypeStruct((m, D_MODEL), F32),
        scratch_shapes=[pltpu.VMEM((ROW_TILE, D_MODEL), BF16), pltpu.VMEM((ROW_TILE, D_MODEL), F32)],
        compiler_params=pltpu.CompilerParams(dimension_semantics=("parallel", "arbitrary"),
                                             vmem_limit_bytes=VMEM_LIMIT_BYTES),
        name="ffn_half_step",
    )(x, g.reshape(1, D_MODEL), wg, wu, wd)


def _proj_kernel(x_ref, g_ref, w_ref, o_ref, h_ref):
    @pl.when(pl.program_id(1) == 0)
    def _():
        h_ref[...] = _rms_bf16(x_ref[...], g_ref[...])

    o_ref[...] = jnp.dot(h_ref[...], w_ref[...], preferred_element_type=F32)


def in_projection(x, g, w):
    m = x.shape[0]
    return pl.pallas_call(
        _proj_kernel,
        grid=(m // ROW_TILE, N_IN_PAD // PROJ_TILE),
        in_specs=[
            pl.BlockSpec((ROW_TILE, D_MODEL), lambda i, n: (i, 0)),
            pl.BlockSpec((1, D_MODEL), lambda i, n: (0, 0)),
            pl.BlockSpec((D_MODEL, PROJ_TILE), lambda i, n: (0, n)),
        ],
        out_specs=pl.BlockSpec((ROW_TILE, PROJ_TILE), lambda i, n: (i, n)),
        out_shape=jax.ShapeDtypeStruct((m, N_IN_PAD), F32),
        scratch_shapes=[pltpu.VMEM((ROW_TILE, D_MODEL), BF16)],
        compiler_params=pltpu.CompilerParams(dimension_semantics=("parallel", "arbitrary"),
                                             vmem_limit_bytes=VMEM_LIMIT_BYTES),
        name="in_projection",
    )(x, g.reshape(1, D_MODEL), w)


def _merge_kernel(oa_ref, ob_ref, oc_ref, ga_ref, gb_ref, gc_ref, wa_ref, wb_ref, wc_ref, y_ref):
    ya = jnp.dot(oa_ref[...], wa_ref[...], preferred_element_type=F32)
    yb = jnp.dot(ob_ref[...], wb_ref[...], preferred_element_type=F32)
    yc = jnp.dot(oc_ref[...], wc_ref[...], preferred_element_type=F32)
    y = jax.nn.sigmoid(ga_ref[...]) * ya + jax.nn.sigmoid(gb_ref[...]) * yb + jax.nn.sigmoid(gc_ref[...]) * yc
    y_ref[...] = y.astype(BF16)


def merge_branches(o_gla, o_nsa, o_sb, gates, wa, wb, wc):
    m = o_gla.shape[0]
    nb = D_MODEL // OUT_TILE
    o_spec = lambda width: pl.BlockSpec((ROW_TILE, width), lambda i, n: (i, 0))
    g_spec = lambda k: pl.BlockSpec((ROW_TILE, OUT_TILE), lambda i, n, k=k: (i, Z_MG // OUT_TILE + n + k * nb))
    w_spec = lambda width: pl.BlockSpec((width, OUT_TILE), lambda i, n: (0, n))
    return pl.pallas_call(
        _merge_kernel,
        grid=(m // ROW_TILE, nb),
        in_specs=[o_spec(GLA_VAL), o_spec(NSA_OUT), o_spec(SB_OUT), g_spec(0), g_spec(1), g_spec(2),
                  w_spec(GLA_VAL), w_spec(NSA_OUT), w_spec(SB_OUT)],
        out_specs=pl.BlockSpec((ROW_TILE, OUT_TILE), lambda i, n: (i, n)),
        out_shape=jax.ShapeDtypeStruct((m, D_MODEL), BF16),
        compiler_params=pltpu.CompilerParams(dimension_semantics=("parallel", "arbitrary"),
                                             vmem_limit_bytes=VMEM_LIMIT_BYTES),
        name="merge_branches",
    )(o_gla, o_nsa, o_sb, gates, gates, gates, wa, wb, wc)


def _out_kernel(x_ref, y_ref, w_ref, o_ref):
    o_ref[...] = x_ref[...] + jnp.dot(y_ref[...], w_ref[...], preferred_element_type=F32)


def out_projection(x, y, w):
    m = x.shape[0]
    return pl.pallas_call(
        _out_kernel,
        grid=(m // ROW_TILE, D_MODEL // OUT_TILE),
        in_specs=[
            pl.BlockSpec((ROW_TILE, OUT_TILE), lambda i, n: (i, n)),
            pl.BlockSpec((ROW_TILE, D_MODEL), lambda i, n: (i, 0)),
            pl.BlockSpec((D_MODEL, OUT_TILE), lambda i, n: (0, n)),
        ],
        out_specs=pl.BlockSpec((ROW_TILE, OUT_TILE), lambda i, n: (i, n)),
        out_shape=jax.ShapeDtypeStruct((m, D_MODEL), F32),
        compiler_params=pltpu.CompilerParams(dimension_semantics=("parallel", "arbitrary"),
                                             vmem_limit_bytes=VMEM_LIMIT_BYTES),
        name="out_projection",
    )(x, y, w)


def _norm_kernel(x_ref, g_ref, o_ref):
    x = x_ref[...]
    o_ref[...] = x * lax.rsqrt(jnp.mean(x * x, axis=-1, keepdims=True) + NORM_EPS) * g_ref[...]


def final_rmsnorm(x, g):
    m = x.shape[0]
    return pl.pallas_call(
        _norm_kernel,
        grid=(m // ROW_TILE,),
        in_specs=[pl.BlockSpec((ROW_TILE, D_MODEL), lambda i: (i, 0)), pl.BlockSpec((1, D_MODEL), lambda i: (0, 0))],
        out_specs=pl.BlockSpec((ROW_TILE, D_MODEL), lambda i: (i, 0)),
        out_shape=jax.ShapeDtypeStruct((m, D_MODEL), F32),
        compiler_params=pltpu.CompilerParams(dimension_semantics=("parallel",), vmem_limit_bytes=VMEM_LIMIT_BYTES),
        name="final_rmsnorm",
    )(x, g.reshape(1, D_MODEL))


NSA_TQ = 128
NSA_TK = 256
SEL_TOEP = 4
WIN_TOEP = 6


def _rel_bucket(dist):
    n = jnp.maximum(dist, 0)
    nf = jnp.maximum(n, 1).astype(jnp.float32)
    large = RPB_MAX_EXACT + (jnp.log(nf / RPB_MAX_EXACT) / math.log(RPB_MAX_DIST / RPB_MAX_EXACT)
                             * (N_BUCKETS - RPB_MAX_EXACT)).astype(jnp.int32)
    return jnp.where(n < RPB_MAX_EXACT, n, jnp.minimum(large, N_BUCKETS - 1))


def _cmp_sel_map(nc, ns):
    c_start = jnp.arange(nc) * CMP_STRIDE
    s_start = jnp.arange(ns) * SEL_BLOCK
    hit = (c_start[:, None] < s_start[None, :] + SEL_BLOCK) & (c_start[:, None] + CMP_LEN > s_start[None, :])
    return hit.astype(jnp.float32)


def _split_hi_lo(x):
    hi = x.astype(BF16)
    lo = (x - hi.astype(F32)).astype(BF16)
    return hi, lo


def _bias_lookup(rpb, bucket):
    one_hot = jax.nn.one_hot(bucket, N_BUCKETS, dtype=F32)
    return jnp.einsum('...k,kh->...h', one_hot, rpb.astype(F32), precision=lax.Precision.HIGHEST)


def _toeplitz_bias(rpb, window):
    n = WIN_TOEP if window else SEL_TOEP
    w = jnp.arange(NSA_TK)[:, None]
    u = jnp.arange(NSA_TQ)[None, :]
    d = (jnp.arange(n) * NSA_TQ)[:, None, None] + (u - w)[None]
    b = _bias_lookup(rpb, _rel_bucket(d))
    ok = (d >= 0) & (d < WINDOW) if window else (d >= 0)
    b = jnp.where(ok[..., None], b, NEG).reshape(n, NSA_TK, NSA_TQ, NSA_KV_HEADS, NSA_GROUP)
    return b.transpose(3, 0, 1, 4, 2).reshape(NSA_KV_HEADS, n, NSA_TK, NSA_GROUP * NSA_TQ)


def _nsa_prompt_kernel(q_ref, gate_ref, cmp_ref, k_ref, vt_ref, posa_ref, posb_ref, w1a_ref, w1b_ref, w2_ref,
                         biasc_ref, stoep_ref, wtoep_ref, mapt_ref, expand_ref, o_ref,
                         kc_ref, vct_ref, acc_ref, *, n_cmp):
    qi = pl.program_id(2)
    tq, tk, R = NSA_TQ, NSA_TK, NSA_GROUP
    cols = R * tq
    nt = (((1,), (1,)), ((), ()))

    @pl.when(qi == 0)
    def _():
        for t in range(2):
            x = cmp_ref[0, 0, t]
            first = jnp.dot((x + posa_ref[t]).astype(BF16), w1a_ref[t], preferred_element_type=F32)
            second = jnp.dot((x + posb_ref[t]).astype(BF16), w1b_ref[t], preferred_element_type=F32)
            pre = first + pltpu.roll(second, second.shape[0] - 1, 0)
            hid = (pre * jax.nn.sigmoid(pre)).astype(BF16)
            c = jnp.dot(hid, w2_ref[t], preferred_element_type=F32)
            if t == 0:
                kc_ref[...] = c.astype(BF16)
            else:
                vct_ref[...] = c.T.astype(BF16)

    q4 = q_ref[0, 0].reshape(cols, NSA_D)
    lane_q = lax.broadcasted_iota(jnp.int32, (1, cols), 1) % tq
    q_pos = qi * tq + lane_q

    ncp = kc_ref.shape[0]
    s = lax.dot_general(kc_ref[...], q4, nt, preferred_element_type=F32) + biasc_ref[0, 0]
    n_idx = lax.broadcasted_iota(jnp.int32, (ncp, cols), 0)
    ok = ((n_idx * CMP_STRIDE + (CMP_LEN - 1)) <= q_pos) & (n_idx < n_cmp)
    s = jnp.where(ok, s, NEG)
    e = jnp.exp(s - jnp.max(s, axis=0, keepdims=True))
    p_c = jnp.where(ok, e / jnp.sum(e, axis=0, keepdims=True), 0.0)
    o_c = jnp.dot(vct_ref[...], p_c.astype(BF16), preferred_element_type=F32)
    psum = p_c[:, 0:tq]
    for r in range(1, R):
        psum = psum + p_c[:, r * tq:(r + 1) * tq]
    p_hi, p_lo = _split_hi_lo(psum)
    imp = (jnp.dot(mapt_ref[...], p_hi, preferred_element_type=F32)
           + jnp.dot(mapt_ref[...], p_lo, preferred_element_type=F32))

    ns = imp.shape[0]
    j_idx = lax.broadcasted_iota(jnp.int32, (ns, tq), 0)
    t_pos = qi * tq + lax.broadcasted_iota(jnp.int32, (ns, tq), 1)
    cur = t_pos // SEL_BLOCK
    valid = j_idx * SEL_BLOCK <= t_pos
    forced = (j_idx == 0) | (j_idx == cur) | (j_idx == cur - 1)
    score = jnp.where(valid, jnp.where(forced, FORCE, imp), NEG)
    rank = jnp.zeros((ns, tq), F32)
    for i in range(ns):
        row = score[i:i + 1, :]
        beats = (row > score) | ((row == score) & (j_idx > i))
        rank = rank + jnp.where(beats, 1.0, 0.0)
    sel_t = jnp.where((rank < float(min(N_SEL, ns))) & valid, 1.0, 0.0).astype(BF16)
    sel4 = jnp.concatenate([sel_t] * R, axis=1)

    def attend(idx, toep_ref, n_toep, j_lo, use_sel):
        acc_ref[...] = jnp.zeros(acc_ref.shape, F32)
        j_hi = qi // 2

        def body(step, carry):
            m_old, l_old = carry
            j = j_hi - step
            k = k_ref[0, 0, idx, pl.ds(pl.multiple_of(j * tk, tk), tk), :]
            vt = vt_ref[0, 0, idx, :, pl.ds(pl.multiple_of(j * tk, tk), tk)]
            off = jnp.minimum(qi - 2 * j, n_toep - 1)
            sc = lax.dot_general(k, q4, nt, preferred_element_type=F32) + toep_ref[0, off]
            if use_sel:
                hit = jnp.dot(expand_ref[j], sel4, preferred_element_type=F32)
                sc = jnp.where(hit > 0.5, sc, NEG)
            m_new = jnp.maximum(m_old, jnp.max(sc, axis=0, keepdims=True))
            alpha = jnp.exp(m_old - m_new)
            p = jnp.exp(sc - m_new)
            l_new = alpha * l_old + jnp.sum(p, axis=0, keepdims=True)
            acc_ref[...] = alpha * acc_ref[...] + jnp.dot(vt, p.astype(BF16), preferred_element_type=F32)
            return m_new, l_new

        init = (jnp.full((1, cols), NEG, F32), jnp.zeros((1, cols), F32))
        _, l_fin = lax.fori_loop(0, j_hi - j_lo + 1, body, init)
        return acc_ref[...] / l_fin

    o_s = attend(0, stoep_ref, SEL_TOEP, 0, True)
    o_w = attend(1, wtoep_ref, WIN_TOEP, jnp.maximum(qi * tq - (WINDOW - 1), 0) // tk, False)

    g = jax.nn.sigmoid(gate_ref[0, 0])
    for r in range(R):
        sl = slice(r * tq, (r + 1) * tq)
        o = (g[3 * r:3 * r + 1] * o_c[:, sl] + g[3 * r + 1:3 * r + 2] * o_s[:, sl]
             + g[3 * r + 2:3 * r + 3] * o_w[:, sl])
        o_ref[0, 0, r] = o.astype(o_ref.dtype)


def nsa_prompt(nq, ng, nkv, cmp_pos, cmp_w1, cmp_w2, rpb):
    B, T = nq.shape[:2]
    G, R, d = NSA_KV_HEADS, NSA_GROUP, NSA_D
    n_sub = T // CMP_STRIDE
    n_cmp = n_sub - 1
    ns = T // SEL_BLOCK
    nq_t = T // NSA_TQ
    nk_t = T // NSA_TK
    q = (nq * (d ** -0.5)).astype(BF16).reshape(B, T, G, R, d).transpose(0, 2, 3, 1, 4)
    gates = ng.reshape(B, T, G, 3 * R).transpose(0, 2, 3, 1)
    cmp_rows = nkv[:, :, 0:2].transpose(0, 3, 2, 1, 4).reshape(B, G, 2, n_sub, CMP_STRIDE * d)
    k = jnp.stack([nkv[:, :, 2].astype(BF16).transpose(0, 2, 1, 3),
                   nkv[:, :, 4].astype(BF16).transpose(0, 2, 1, 3)], axis=2)
    vt = jnp.stack([nkv[:, :, 3].astype(BF16).transpose(0, 2, 3, 1),
                    nkv[:, :, 5].astype(BF16).transpose(0, 2, 3, 1)], axis=2)
    pos_a = cmp_pos[:, :CMP_STRIDE].reshape(2, 1, CMP_STRIDE * d)
    pos_b = cmp_pos[:, CMP_STRIDE:].reshape(2, 1, CMP_STRIDE * d)
    w1a = cmp_w1[:, :CMP_STRIDE].reshape(2, CMP_STRIDE * d, CMP_HID).astype(BF16)
    w1b = cmp_w1[:, CMP_STRIDE:].reshape(2, CMP_STRIDE * d, CMP_HID).astype(BF16)
    w2 = cmp_w2.astype(BF16)
    t_pos = jnp.arange(T)[None, :]
    c_end = jnp.arange(n_sub)[:, None] * CMP_STRIDE + (CMP_LEN - 1)
    bias_c = _bias_lookup(rpb, _rel_bucket(t_pos - c_end))
    bias_c = bias_c.reshape(n_sub, nq_t, NSA_TQ, G, R).transpose(3, 1, 0, 4, 2).reshape(G, nq_t, n_sub, R * NSA_TQ)
    stoep = _toeplitz_bias(rpb, False)
    wtoep = _toeplitz_bias(rpb, True)
    map_t = _cmp_sel_map(n_sub, ns).T.astype(BF16)
    expand = ((jnp.arange(nk_t)[:, None, None] * NSA_TK + jnp.arange(NSA_TK)[None, :, None]) // SEL_BLOCK
              == jnp.arange(ns)[None, None, :]).astype(BF16)
    full = lambda a: pl.BlockSpec(a.shape, lambda b, g, i: (0,) * a.ndim)
    per_g = lambda a: pl.BlockSpec((1,) + a.shape[1:], lambda b, g, i: (g,) + (0,) * (a.ndim - 1))
    per_bg = lambda a: pl.BlockSpec((1, 1) + a.shape[2:], lambda b, g, i: (b, g) + (0,) * (a.ndim - 2))
    o = pl.pallas_call(
        functools.partial(_nsa_prompt_kernel, n_cmp=n_cmp),
        grid=(B, G, nq_t),
        in_specs=[
            pl.BlockSpec((1, 1, R, NSA_TQ, d), lambda b, g, i: (b, g, 0, i, 0)),
            pl.BlockSpec((1, 1, 3 * R, NSA_TQ), lambda b, g, i: (b, g, 0, i)),
            per_bg(cmp_rows), per_bg(k), per_bg(vt),
            full(pos_a), full(pos_b), full(w1a), full(w1b), full(w2),
            pl.BlockSpec((1, 1, n_sub, R * NSA_TQ), lambda b, g, i: (g, i, 0, 0)),
            per_g(stoep), per_g(wtoep), full(map_t), full(expand),
        ],
        out_specs=pl.BlockSpec((1, 1, R, d, NSA_TQ), lambda b, g, i: (b, g, 0, 0, i)),
        out_shape=jax.ShapeDtypeStruct((B, G, R, d, T), BF16),
        scratch_shapes=[pltpu.VMEM((n_sub, d), BF16), pltpu.VMEM((d, n_sub), BF16),
                        pltpu.VMEM((d, R * NSA_TQ), F32)],
        compiler_params=pltpu.CompilerParams(dimension_semantics=("parallel", "parallel", "arbitrary"),
                                             vmem_limit_bytes=VMEM_LIMIT_BYTES),
        name="nsa_prompt",
    )(q, gates, cmp_rows, k, vt, pos_a, pos_b, w1a, w1b, w2, bias_c, stoep, wtoep, map_t, expand)
    return o.transpose(0, 4, 1, 2, 3).reshape(B, T, G * R * d)


SB_TQ = 512
SB_TK = 256
SB_HEADS_PER_STEP = 4


def _softplus(z):
    return jnp.maximum(z, 0.0) + jnp.log(1.0 + jnp.exp(-jnp.abs(z)))


def _suffix_matrix(n):
    j = jnp.arange(n)[:, None]
    s = jnp.arange(n)[None, :]
    return jnp.concatenate([(j > s), jnp.ones((n, n), bool)], axis=1).astype(BF16)


def _sb_tile(z, ok, v, suffix, c_ref, acc_ref):
    tk = z.shape[1]
    sp = _softplus(z)
    lf = -sp if ok is None else jnp.where(ok, -sp, 0.0)
    hi, lo = _split_hi_lo(lf)
    tot = (jnp.dot(hi, suffix, preferred_element_type=F32) + jnp.dot(lo, suffix, preferred_element_type=F32))
    later = tot[:, :tk] + c_ref[...]
    a = jnp.exp(z - sp + later)
    if ok is not None:
        a = jnp.where(ok, a, 0.0)
    acc_ref[...] += jnp.dot(a.astype(BF16), v, preferred_element_type=F32)
    c_ref[...] += tot[:, tk:]


def _sb_prompt_kernel(q_ref, k_ref, v_ref, suffix_ref, o_ref, c_ref, acc_ref):
    qi = pl.program_id(2)
    tq, tk = SB_TQ, SB_TK
    per = tq // tk
    heads = q_ref.shape[1]
    suffix = suffix_ref[...]
    c_ref[...] = jnp.zeros(c_ref.shape, F32)
    acc_ref[...] = jnp.zeros(acc_ref.shape, F32)
    nt = (((1,), (1,)), ((), ()))
    q_pos = qi * tq + lax.broadcasted_iota(jnp.int32, (tq, tk), 0)
    k_off = lax.broadcasted_iota(jnp.int32, (tq, tk), 1)

    def tile(j, masked):
        ok = (j * tk + k_off < q_pos) if masked else None
        for h in range(heads):
            k = k_ref[0, h, pl.ds(pl.multiple_of(j * tk, tk), tk), :]
            v = v_ref[0, h, pl.ds(pl.multiple_of(j * tk, tk), tk), :]
            z = lax.dot_general(q_ref[0, h], k, nt, preferred_element_type=F32)
            _sb_tile(z, ok, v, suffix, c_ref.at[h], acc_ref.at[h])

    for t in range(per):
        tile(qi * per + (per - 1 - t), True)

    def body(step, carry):
        tile(qi * per - 1 - step, False)
        return carry

    lax.fori_loop(0, qi * per, body, 0)
    o_ref[0] = acc_ref[...].astype(o_ref.dtype)


def sb_prompt(sq, skv):
    B, T = sq.shape[:2]
    H, d, hp = SB_HEADS, SB_D, SB_HEADS_PER_STEP
    q = (sq * (d ** -0.5)).astype(BF16).reshape(B, T, H, d).transpose(0, 2, 1, 3)
    k = skv[:, :, 0].astype(BF16).transpose(0, 2, 1, 3)
    v = skv[:, :, 1].astype(BF16).transpose(0, 2, 1, 3)
    suffix = _suffix_matrix(SB_TK)
    kv_spec = pl.BlockSpec((1, hp, T, d), lambda b, h, i: (b, h, 0, 0))
    q_spec = pl.BlockSpec((1, hp, SB_TQ, d), lambda b, h, i: (b, h, i, 0))
    o = pl.pallas_call(
        _sb_prompt_kernel,
        grid=(B, H // hp, T // SB_TQ),
        in_specs=[q_spec, kv_spec, kv_spec, pl.BlockSpec(suffix.shape, lambda b, h, i: (0, 0))],
        out_specs=q_spec,
        out_shape=jax.ShapeDtypeStruct((B, H, T, d), BF16),
        scratch_shapes=[pltpu.VMEM((hp, SB_TQ, SB_TK), F32), pltpu.VMEM((hp, SB_TQ, d), F32)],
        compiler_params=pltpu.CompilerParams(dimension_semantics=("parallel", "parallel", "arbitrary"),
                                             vmem_limit_bytes=VMEM_LIMIT_BYTES),
        name="sb_prompt",
    )(q, k, v, suffix)
    return o.transpose(0, 2, 1, 3).reshape(B, T, H * d)


GLA_SUB = 16
GLA_EXP_CAP = 60.0


def _gla_head(q, k, v, a_low, r, wd, bd, gn, s_old, sub):
    C = q.shape[0]
    q = q * (GLA_DK ** -0.5)
    v = v.astype(BF16)
    pre = jnp.dot(a_low, wd, preferred_element_type=F32) + bd
    log_a = -_softplus(-pre) * (1.0 / GLA_TAU)
    row = lax.broadcasted_iota(jnp.int32, (C, C), 0)
    col = lax.broadcasted_iota(jnp.int32, (C, C), 1)
    lower = jnp.where(col <= row, 1.0, 0.0).astype(BF16)
    a_hi, a_lo = _split_hi_lo(log_a)
    b = jnp.dot(lower, a_hi, preferred_element_type=F32) + jnp.dot(lower, a_lo, preferred_element_type=F32)

    o = jnp.dot((q * jnp.exp(b)).astype(BF16), s_old.astype(BF16), preferred_element_type=F32)

    nt = (((1,), (1,)), ((), ()))
    parts = []
    for i in range(C // sub):
        lo, hi = i * sub, (i + 1) * sub
        ref_row = b[lo:lo + 1, :]
        qs = (q[lo:hi] * jnp.exp(b[lo:hi] - ref_row)).astype(BF16)
        ks = (k[:hi] * jnp.exp(jnp.minimum(ref_row - b[:hi], GLA_EXP_CAP))).astype(BF16)
        att = lax.dot_general(qs, ks, nt, preferred_element_type=F32)
        t_idx = lo + lax.broadcasted_iota(jnp.int32, (sub, hi), 0)
        s_idx = lax.broadcasted_iota(jnp.int32, (sub, hi), 1)
        att = jnp.where(s_idx <= t_idx, att, 0.0)
        parts.append(jnp.dot(att.astype(BF16), v[:hi], preferred_element_type=F32))
    o = o + (parts[0] if len(parts) == 1 else jnp.concatenate(parts, axis=0))

    b_t = b.T
    last = b_t[:, C - 1:C]
    kd_t = (k.T * jnp.exp(last - b_t)).astype(BF16)
    s_new = jnp.exp(last) * s_old + jnp.dot(kd_t, v, preferred_element_type=F32)

    y = o * lax.rsqrt(jnp.mean(o * o, axis=-1, keepdims=True) + NORM_EPS) * gn
    return y * (r * jax.nn.sigmoid(r)), s_new


def _gla_kernel(q_ref, k_ref, v_ref, a_ref, r_ref, wd_ref, bd_ref, gn_ref, s0_ref, o_ref, st_ref, s_ref, *, sub):
    c = pl.program_id(1)
    dk, dv = GLA_DK, GLA_DV

    @pl.when(c == 0)
    def _():
        s_ref[...] = s0_ref[0]

    a_low = a_ref[...].astype(BF16)
    for h in range(GLA_HEADS):
        kc = slice(h * dk, (h + 1) * dk)
        vc = slice(h * dv, (h + 1) * dv)
        o, s_new = _gla_head(q_ref[:, kc], k_ref[:, kc], v_ref[:, vc], a_low, r_ref[:, vc], wd_ref[:, kc],
                             bd_ref[:, kc], gn_ref[...], s_ref[h], sub)
        s_ref[h] = s_new
        o_ref[0, :, vc] = o.astype(o_ref.dtype)

    @pl.when(c == pl.num_programs(1) - 1)
    def _():
        st_ref[0] = s_ref[...]


def gla_mixer(z, row0, B, T, s0, w_decay, b_decay, out_norm):
    H, dk, dv = GLA_HEADS, GLA_DK, GLA_DV
    C = min(GLA_CHUNK, T)
    sub = min(GLA_SUB, C)
    assert row0 % C == 0 and T % C == 0
    n_c = T // C
    tok = lambda w, col0: pl.BlockSpec((C, w), lambda b, c: (row0 // C + b * n_c + c, col0 // w))
    full = lambda shape: pl.BlockSpec(shape, lambda b, c: (0,) * len(shape))
    state = pl.BlockSpec((1, H, dk, dv), lambda b, c: (b, 0, 0, 0))
    wd = jnp.zeros((LANES, H * dk), BF16).at[Z_GA - Z_NG:Z_GA - Z_NG + GLA_RANK].set(w_decay.astype(BF16))
    o, st = pl.pallas_call(
        functools.partial(_gla_kernel, sub=sub),
        grid=(B, n_c),
        in_specs=[tok(H * dk, Z_GQ), tok(H * dk, Z_GK), tok(H * dv, Z_GV), tok(LANES, Z_NG), tok(H * dv, Z_GR),
                  full((LANES, H * dk)), full((1, H * dk)), full((1, dv)), state],
        out_specs=[pl.BlockSpec((1, C, H * dv), lambda b, c: (b, c, 0)), state],
        out_shape=[jax.ShapeDtypeStruct((B, T, H * dv), BF16), jax.ShapeDtypeStruct((B, H, dk, dv), F32)],
        scratch_shapes=[pltpu.VMEM((H, dk, dv), F32)],
        compiler_params=pltpu.CompilerParams(dimension_semantics=("parallel", "arbitrary"),
                                             vmem_limit_bytes=VMEM_LIMIT_BYTES),
        name="gla_mixer",
    )(z, z, z, z, z, wd, b_decay.reshape(1, H * dk), out_norm.reshape(1, dv), s0)
    return o, st


def _sb_sample_kernel(pt_ref, q_ref, new_ref, *refs, n_pg, n_new):
    page_refs = refs[:n_pg]
    suffix_ref, o_ref, c_ref, acc_ref = refs[n_pg:]
    s = pl.program_id(1)
    H, d = SB_HEADS, SB_D
    rows = q_ref.shape[1]
    nt = (((1,), (1,)), ((), ()))

    def tile(blk, ok):
        k = blk[:, :H * d].astype(BF16)
        v = blk[:, H * d:].astype(BF16)
        z = lax.dot_general(q_ref[0], k, nt, preferred_element_type=F32)
        _sb_tile(z, ok, v, suffix_ref[...], c_ref, acc_ref)

    @pl.when(s == 0)
    def _():
        c_ref[...] = jnp.zeros(c_ref.shape, F32)
        acc_ref[...] = jnp.zeros(acc_ref.shape, F32)
        t_idx = lax.broadcasted_iota(jnp.int32, c_ref.shape, 0) % n_new
        k_idx = lax.broadcasted_iota(jnp.int32, c_ref.shape, 1)
        tile(new_ref[0], k_idx < t_idx)

    @pl.when(s > 0)
    def _():
        tile(jnp.concatenate([r[0, 0] for r in page_refs], axis=0), None)

    @pl.when(s == pl.num_programs(1) - 1)
    def _():
        acc = acc_ref[...]
        r_idx = lax.broadcasted_iota(jnp.int32, acc.shape, 0) // n_new
        l_idx = lax.broadcasted_iota(jnp.int32, acc.shape, 1) // d
        own = jnp.where(r_idx == l_idx, acc, 0.0)
        o_ref[0] = jnp.sum(own.reshape(H, n_new, H * d), axis=0).astype(o_ref.dtype)


def sb_sample(sq, skv, cache_sb, page_table, layer):
    B, tn = sq.shape[:2]
    H, d = SB_HEADS, SB_D
    n_pages = page_table.shape[1]
    page = cache_sb.shape[2]
    cache = cache_sb.reshape(cache_sb.shape[0], cache_sb.shape[1], page, 2 * H * d)
    q = (sq * (d ** -0.5)).astype(BF16).reshape(B, tn, H, d).transpose(0, 2, 1, 3)
    q_blk = (q[:, :, :, None, :] * jnp.eye(H, dtype=BF16)[None, :, None, :, None]).reshape(B, H * tn, H * d)
    n_pg = next(n for n in (4, 2, 1) if n_pages % n == 0)
    keys = n_pg * page
    new = jnp.pad(skv.reshape(B, tn, 2 * H * d), ((0, 0), (0, keys - tn), (0, 0)))
    suffix = _suffix_matrix(keys)
    grid_spec = pltpu.PrefetchScalarGridSpec(
        num_scalar_prefetch=1,
        grid=(B, n_pages // n_pg + 1),
        in_specs=[
            pl.BlockSpec((1, H * tn, H * d), lambda b, s, pt: (b, 0, 0)),
            pl.BlockSpec((1, keys, 2 * H * d), lambda b, s, pt: (b, 0, 0))] + [
            pl.BlockSpec((1, 1, page, 2 * H * d),
                         lambda b, s, pt, i=i: (pt[b, n_pages - jnp.maximum(s, 1) * n_pg + i], layer, 0, 0))
            for i in range(n_pg)] + [
            pl.BlockSpec(suffix.shape, lambda b, s, pt: (0, 0)),
        ],
        out_specs=pl.BlockSpec((1, tn, H * d), lambda b, s, pt: (b, 0, 0)),
        scratch_shapes=[pltpu.VMEM((H * tn, keys), F32), pltpu.VMEM((H * tn, H * d), F32)],
    )
    return pl.pallas_call(
        functools.partial(_sb_sample_kernel, n_pg=n_pg, n_new=tn),
        grid_spec=grid_spec,
        out_shape=jax.ShapeDtypeStruct((B, tn, H * d), BF16),
        compiler_params=pltpu.CompilerParams(dimension_semantics=("parallel", "arbitrary"),
                                             vmem_limit_bytes=VMEM_LIMIT_BYTES),
        name="sb_sample",
    )(page_table, q_blk, new, *([cache] * n_pg), suffix)


def _nsa_sample_cmp_kernel(pt_ref, q_ref, *refs, n_pg, n_new, past_len):
    page_refs = refs[:n_pg]
    posa_ref, posb_ref, w1_ref, w2_ref, biasc_ref, mapt_ref, oc_ref, sel_ref, x_ref = refs[n_pg:]
    s = pl.program_id(1)
    page = page_refs[0].shape[2]
    G, R = NSA_KV_HEADS, NSA_GROUP
    gw = G * NSA_D
    per = gw // LANES
    for i, page_ref in enumerate(page_refs):
        row0 = pl.multiple_of((s * n_pg + i) * page, page)
        for c in range(x_ref.shape[0]):
            x_ref[c, pl.ds(row0, page), :] = page_ref[0, 0, :, c * LANES:(c + 1) * LANES]

    @pl.when(s == pl.num_programs(1) - 1)
    def _():
        n_sub = x_ref.shape[1] // CMP_STRIDE
        n_cmp = n_sub - 1
        kv = []
        for t in range(2):
            first = jnp.zeros((n_sub, gw), F32)
            second = jnp.zeros((n_sub, gw), F32)
            for l in range(CMP_STRIDE):
                x = jnp.concatenate([x_ref[t * per + c, pl.ds(l, n_sub, stride=CMP_STRIDE), :] for c in range(per)],
                                    axis=1)
                first += jnp.dot((x + posa_ref[t, l]).astype(BF16), w1_ref[t, l], preferred_element_type=F32)
                second += jnp.dot((x + posb_ref[t, l]).astype(BF16), w1_ref[t, CMP_STRIDE + l],
                                  preferred_element_type=F32)
            pre = first + pltpu.roll(second, n_sub - 1, 0)
            hid = (pre * jax.nn.sigmoid(pre)).astype(BF16)
            kv.append(jnp.dot(hid, w2_ref[t], preferred_element_type=F32).astype(BF16))
        kc, vc = kv
        nt = (((1,), (1,)), ((), ()))
        sc = lax.dot_general(q_ref[0], kc, nt, preferred_element_type=F32) + biasc_ref[...]
        ok = lax.broadcasted_iota(jnp.int32, sc.shape, 1) < n_cmp
        sc = jnp.where(ok, sc, NEG)
        e = jnp.exp(sc - jnp.max(sc, axis=-1, keepdims=True))
        p_c = jnp.where(ok, e / jnp.sum(e, axis=-1, keepdims=True), 0.0)
        oc_ref[0] = jnp.dot(p_c.astype(BF16), vc, preferred_element_type=F32)
        psum = jnp.sum(p_c.reshape(G, R, n_new, n_sub), axis=1).reshape(G * n_new, n_sub)
        p_hi, p_lo = _split_hi_lo(psum)
        imp = (lax.dot_general(mapt_ref[...], p_hi, nt, preferred_element_type=F32)
               + lax.dot_general(mapt_ref[...], p_lo, nt, preferred_element_type=F32))
        ns_pad, cols = imp.shape
        ns = past_len // SEL_BLOCK + 1
        j_idx = lax.broadcasted_iota(jnp.int32, imp.shape, 0)
        t_pos = past_len + lax.broadcasted_iota(jnp.int32, imp.shape, 1) % n_new
        cur = t_pos // SEL_BLOCK
        valid = (j_idx * SEL_BLOCK <= t_pos) & (j_idx < ns)
        forced = (j_idx == 0) | (j_idx == cur) | (j_idx == cur - 1)
        score = jnp.where(valid, jnp.where(forced, FORCE, imp), NEG)

        def rank_step(i, rank):
            row = jnp.sum(jnp.where(j_idx == i, score, 0.0), axis=0, keepdims=True)
            beats = (row > score) | ((row == score) & (j_idx > i))
            return rank + jnp.where(beats, 1.0, 0.0)

        rank = lax.fori_loop(0, ns, rank_step, jnp.zeros(imp.shape, F32))
        sel_t = jnp.where((rank < float(min(N_SEL, ns))) & valid, 1.0, 0.0)
        sel = sel_t.T.reshape(G, 1, n_new, ns_pad)
        sel_ref[0] = jnp.broadcast_to(sel, (G, R, n_new, ns_pad)).reshape(G * R * n_new, ns_pad).astype(BF16)


def _two_part_softmax_av(s_a, v_a, s_b, v_b):
    m = jnp.maximum(jnp.max(s_a, axis=-1, keepdims=True), jnp.max(s_b, axis=-1, keepdims=True))
    p_a = jnp.exp(s_a - m)
    p_b = jnp.exp(s_b - m)
    den = jnp.sum(p_a, axis=-1, keepdims=True) + jnp.sum(p_b, axis=-1, keepdims=True)
    num = (jnp.dot(p_a.astype(BF16), v_a, preferred_element_type=F32)
           + jnp.dot(p_b.astype(BF16), v_b, preferred_element_type=F32))
    return num / den


def _nsa_sample_attn_kernel(pt_ref, q_ref, sel_ref, oc_ref, gate_ref, new_ref, win_ref, *refs, n_pg, n_pages):
    page_refs = refs[:n_pg]
    bnew_ref, bwin_ref, blast_ref, bfar_ref, o_ref, m_ref, l_ref, acc_ref, ow_ref = refs[n_pg:]
    s = pl.program_id(1)
    G, d = NSA_KV_HEADS, NSA_D
    gw = G * d
    page = page_refs[0].shape[2]
    rows = q_ref.shape[1]
    q = q_ref[0]
    nt = (((1,), (1,)), ((), ()))

    def online(sc, v):
        m_old = m_ref[...]
        m_new = jnp.maximum(m_old, jnp.max(sc, axis=-1, keepdims=True))
        alpha = jnp.exp(m_old - m_new)
        p = jnp.exp(sc - m_new)
        l_ref[...] = alpha * l_ref[...] + jnp.sum(p, axis=-1, keepdims=True)
        acc_ref[...] = alpha * acc_ref[...] + jnp.dot(p.astype(BF16), v, preferred_element_type=F32)
        m_ref[...] = m_new

    @pl.when(s == 0)
    def _():
        m_ref[...] = jnp.full(m_ref.shape, NEG, F32)
        l_ref[...] = jnp.zeros(l_ref.shape, F32)
        acc_ref[...] = jnp.zeros(acc_ref.shape, F32)
        new = new_ref[0].astype(BF16)
        online(lax.dot_general(q, new[:, :gw], nt, preferred_element_type=F32) + bnew_ref[...], new[:, gw:2 * gw])
        win = win_ref[0, 0].astype(BF16)
        s_w = lax.dot_general(q, win[:, :gw], nt, preferred_element_type=F32) + bwin_ref[...]
        s_n = lax.dot_general(q, new[:, 2 * gw:3 * gw], nt, preferred_element_type=F32) + bnew_ref[...]
        ow_ref[...] = _two_part_softmax_av(s_w, win[:, gw:], s_n, new[:, 3 * gw:])

    @pl.when(s > 0)
    def _():
        pg = n_pages - s * n_pg
        blk = jnp.concatenate([r[0, 0].astype(BF16) for r in page_refs], axis=0)
        keys = blk.shape[0]
        sc = lax.dot_general(q, blk[:, :gw], nt, preferred_element_type=F32)
        sc = sc + jnp.where(s == 1, blast_ref[...], bfar_ref[...])
        ns_pad = sel_ref.shape[2]
        blk_idx = lax.broadcasted_iota(jnp.int32, (ns_pad, keys), 0)
        key_blk = pg * (page // SEL_BLOCK) + lax.broadcasted_iota(jnp.int32, (ns_pad, keys), 1) // SEL_BLOCK
        expand = jnp.where(blk_idx == key_blk, 1.0, 0.0).astype(BF16)
        hit = jnp.dot(sel_ref[0], expand, preferred_element_type=F32)
        online(jnp.where(hit > 0.5, sc, NEG), blk[:, gw:])

    @pl.when(s == pl.num_programs(1) - 1)
    def _():
        g = jax.nn.sigmoid(gate_ref[0])
        o = g[:, 0:1] * oc_ref[0] + g[:, 1:2] * (acc_ref[...] / l_ref[...]) + g[:, 2:3] * ow_ref[...]
        r_grp = lax.broadcasted_iota(jnp.int32, o.shape, 0) // (rows // G)
        l_grp = lax.broadcasted_iota(jnp.int32, o.shape, 1) // d
        own = jnp.where(r_grp == l_grp, o, 0.0)
        out = own[:, 0:d]
        for gi in range(1, G):
            out = out + own[:, gi * d:(gi + 1) * d]
        o_ref[0] = out.astype(o_ref.dtype)


def nsa_sample(nq, ng, nkv, cache_nsa, cache_win_l, page_table, layer, cmp_pos, cmp_w1, cmp_w2, rpb):
    B, tn = nq.shape[:2]
    G, R, d = NSA_KV_HEADS, NSA_GROUP, NSA_D
    gw = G * d
    n_pages = page_table.shape[1]
    page = cache_nsa.shape[2]
    past_len = n_pages * page
    assert tn < CMP_STRIDE and page % SEL_BLOCK == 0 and past_len % SEL_BLOCK == 0
    wb = cache_win_l.shape[1]
    n_sub = past_len // CMP_STRIDE
    ns = past_len // SEL_BLOCK + 1
    ns_pad = -(-ns // LANES) * LANES
    rows = G * R * tn
    cache = cache_nsa.reshape(cache_nsa.shape[:2] + (page, 4 * gw))
    q = (nq * (d ** -0.5)).astype(BF16).reshape(B, tn, G, R, d).transpose(0, 2, 3, 1, 4)
    q_blk = (q[:, :, :, :, None, :] * jnp.eye(G, dtype=BF16)[None, :, None, None, :, None]).reshape(B, rows, gw)
    gates = ng.reshape(B, tn, G, R, 3).transpose(0, 2, 3, 1, 4).reshape(B, rows, 3)
    new = jnp.pad(nkv[:, :, 2:6].reshape(B, tn, 4 * gw), ((0, 0), (0, page - tn), (0, 0)))
    win = cache_win_l.reshape(B, 1, wb, 2 * gw)
    eye_g = jnp.eye(G, dtype=F32)
    w1 = jnp.einsum('tlde,gh->tlgdhe', cmp_w1, eye_g).reshape(2, CMP_LEN, gw, G * CMP_HID).astype(BF16)
    w2 = jnp.einsum('ted,gh->tgehd', cmp_w2, eye_g).reshape(2, G * CMP_HID, gw).astype(BF16)
    pos = jnp.tile(cmp_pos[:, :, None, :], (1, 1, G, 1)).reshape(2, CMP_LEN, 1, gw)
    pos_a, pos_b = pos[:, :CMP_STRIDE], pos[:, CMP_STRIDE:]
    table = rpb.astype(F32).reshape(N_BUCKETS, G, R).transpose(1, 2, 0)
    q_pos = past_len + jnp.arange(tn)

    def bias_rows(k_pos, ok):
        dist = q_pos[:, None] - k_pos[None, :]
        b = jnp.take(table, _rel_bucket(dist), axis=2)
        return jnp.where(ok(dist, k_pos[None, :]), b, NEG).reshape(rows, k_pos.shape[0])

    c_end = jnp.arange(n_sub) * CMP_STRIDE + (CMP_LEN - 1)
    bias_c = bias_rows(c_end, lambda dist, kp: dist >= 0)
    new_pos = past_len + jnp.arange(page)
    bias_new = bias_rows(new_pos, lambda dist, kp: (dist >= 0) & (kp < past_len + tn))
    win_pos = past_len - wb + jnp.arange(wb)
    bias_win = bias_rows(win_pos, lambda dist, kp: (dist >= 0) & (dist < WINDOW) & (kp >= 0))
    n_pg = next(n for n in (4, 2, 1) if n_pages % n == 0)
    bias_last = bias_rows(past_len - n_pg * page + jnp.arange(n_pg * page), lambda dist, kp: dist >= 0)
    bias_far = jnp.broadcast_to(table[:, :, None, N_BUCKETS - 1:], (G, R, tn, 1)).reshape(rows, 1)
    map_t = jnp.pad(_cmp_sel_map(n_sub, ns).T, ((0, ns_pad - ns), (0, 0))).astype(BF16)

    const = lambda a: pl.BlockSpec(a.shape, lambda b, s, pt: (0,) * a.ndim)
    per_b = lambda a: pl.BlockSpec((1,) + a.shape[1:], lambda b, s, pt: (b,) + (0,) * (a.ndim - 1))
    params = pltpu.CompilerParams(dimension_semantics=("parallel", "arbitrary"), vmem_limit_bytes=VMEM_LIMIT_BYTES)
    o_c, sel = pl.pallas_call(
        functools.partial(_nsa_sample_cmp_kernel, n_pg=n_pg, n_new=tn, past_len=past_len),
        grid_spec=pltpu.PrefetchScalarGridSpec(
            num_scalar_prefetch=1,
            grid=(B, n_pages // n_pg),
            in_specs=[per_b(q_blk)] + [
                pl.BlockSpec((1, 1, page, 2 * gw), lambda b, s, pt, i=i: (pt[b, s * n_pg + i], layer, 0, 0))
                for i in range(n_pg)] + [
                const(pos_a), const(pos_b), const(w1), const(w2), const(bias_c), const(map_t),
            ],
            out_specs=[pl.BlockSpec((1, rows, gw), lambda b, s, pt: (b, 0, 0)),
                       pl.BlockSpec((1, rows, ns_pad), lambda b, s, pt: (b, 0, 0))],
            scratch_shapes=[pltpu.VMEM((2 * gw // LANES, past_len, LANES), F32)],
        ),
        out_shape=[jax.ShapeDtypeStruct((B, rows, gw), F32), jax.ShapeDtypeStruct((B, rows, ns_pad), BF16)],
        compiler_params=params,
        name="nsa_sample_cmp",
    )(page_table, q_blk, *([cache] * n_pg), pos_a, pos_b, w1, w2, bias_c, map_t)
    o = pl.pallas_call(
        functools.partial(_nsa_sample_attn_kernel, n_pg=n_pg, n_pages=n_pages),
        grid_spec=pltpu.PrefetchScalarGridSpec(
            num_scalar_prefetch=1,
            grid=(B, n_pages // n_pg + 1),
            in_specs=[per_b(q_blk), per_b(sel), per_b(o_c), per_b(gates), per_b(new), per_b(win)] + [
                pl.BlockSpec((1, 1, page, 2 * gw),
                             lambda b, s, pt, i=i: (pt[b, n_pages - jnp.maximum(s, 1) * n_pg + i], layer, 0, 1))
                for i in range(n_pg)] + [
                const(bias_new), const(bias_win), const(bias_last), const(bias_far),
            ],
            out_specs=pl.BlockSpec((1, rows, d), lambda b, s, pt: (b, 0, 0)),
            scratch_shapes=[pltpu.VMEM((rows, 1), F32), pltpu.VMEM((rows, 1), F32), pltpu.VMEM((rows, gw), F32),
                            pltpu.VMEM((rows, gw), F32)],
        ),
        out_shape=jax.ShapeDtypeStruct((B, rows, d), BF16),
        compiler_params=params,
        name="nsa_sample_attn",
    )(page_table, q_blk, sel, o_c, gates, new, win, *([cache] * n_pg), bias_new, bias_win, bias_last, bias_far)
    return o.reshape(B, G * R, tn, d).transpose(0, 2, 1, 3).reshape(B, tn, G * R * d)


def _permute_w_in(w):
    c = [0] + [int(v) for v in np.cumsum(IN_SPLITS)]
    order = (0, 1, 2, 4, 5, 6, 8, 9, 10, 7, 3)
    parts = [w[:, c[i]:c[i + 1]] for i in order]
    return jnp.pad(jnp.concatenate(parts, axis=1), ((0, 0), (0, N_IN_PAD - N_IN)))


def _attn_parts(z, B, T):
    cut = lambda lo, w: z[:, lo:lo + w].reshape(B, T, w)
    nkv = cut(Z_NKV, 6 * NSA_KVW).reshape(B, T, 6, NSA_KV_HEADS, NSA_D)
    skv = cut(Z_SKV, 2 * SB_OUT).reshape(B, T, 2, SB_HEADS, SB_D)
    return cut(Z_NQ, NSA_OUT), nkv, cut(Z_NG, 3 * NSA_HEADS), cut(Z_SQ, SB_OUT), skv


def _mix_prompt(z, B, T, gla_w_decay, gla_b_decay, gla_out_norm, cmp_pos, cmp_w1, cmp_w2, rpb):
    nq, nkv, ng, sq, skv = _attn_parts(z[:B * T], B, T)
    s0 = jnp.zeros((B, GLA_HEADS, GLA_DK, GLA_DV), jnp.float32)
    o_gla, gla_state = gla_mixer(z, 0, B, T, s0, gla_w_decay, gla_b_decay, gla_out_norm)
    o_nsa = nsa_prompt(nq, ng, nkv, cmp_pos, cmp_w1, cmp_w2, rpb)
    o_sb = sb_prompt(sq, skv)
    keep = min(WINDOW, T)
    return (o_gla, o_nsa, o_sb), nkv[:, :, :4], skv, nkv[:, T - keep:, 4:], gla_state


def _mix_sample(z, row0, B, T, cache_nsa, cache_sb, page_table, layer, win_buf, s0, gla_w_decay, gla_b_decay,
                gla_out_norm, cmp_pos, cmp_w1, cmp_w2, rpb):
    nq, nkv, ng, sq, skv = _attn_parts(z[row0:row0 + B * T], B, T)
    o_gla, gla_state = gla_mixer(z, row0, B, T, s0, gla_w_decay, gla_b_decay, gla_out_norm)
    o_nsa = nsa_sample(nq, ng, nkv, cache_nsa, win_buf, page_table, layer, cmp_pos, cmp_w1, cmp_w2, rpb)
    win = jnp.concatenate([win_buf, nkv[:, :, 4:]], axis=1)
    wb = win_buf.shape[1]
    o_sb = sb_sample(sq, skv, cache_sb, page_table, layer)
    keep = min(WINDOW, wb + T)
    return (o_gla, o_nsa, o_sb), nkv[:, :, :4], skv, win[:, wb + T - keep:], gla_state


def kernel(x_prompt, x_sample, cache_nsa, cache_sb, cache_win, state_gla, page_table, ffn1_norm, ffn1_w_gate, ffn1_w_up, ffn1_w_down, mix_norm, w_in, gla_w_decay, gla_b_decay, gla_out_norm, nsa_cmp_pos, nsa_cmp_w1, nsa_cmp_w2, rpb_table, w_branch_gla, w_branch_nsa, w_branch_sb, w_out, ffn2_norm, ffn2_w_gate, ffn2_w_up, ffn2_w_down, final_norm):
    bp, tp, _ = x_prompt.shape
    bs, ts, _ = x_sample.shape
    mp, ms = bp * tp, bs * ts
    dec_batch, n_pages = page_table.shape
    past_len = n_pages * cache_nsa.shape[2]
    x = jnp.concatenate([x_prompt.reshape(mp, D_MODEL), x_sample.reshape(ms, D_MODEL)], axis=0)
    outs = [[] for _ in range(8)]
    for l in range(DEPTH):
        x = ffn_half_step(x, ffn1_norm[l], ffn1_w_gate[l].astype(BF16), ffn1_w_up[l].astype(BF16),
                          ffn1_w_down[l].astype(BF16))
        z = in_projection(x, mix_norm[l], _permute_w_in(w_in[l].astype(BF16)))
        mix_w = (gla_w_decay[l], gla_b_decay[l], gla_out_norm[l], nsa_cmp_pos[l], nsa_cmp_w1[l], nsa_cmp_w2[l],
                 rpb_table)
        br_p, r_nsa, r_sb, r_win, r_gla = _mix_prompt(z, bp, tp, *mix_w)
        for lst, val in zip(outs[0::2], (r_nsa, r_sb, r_win, r_gla)):
            lst.append(val)
        br_s, r_nsa, r_sb, r_win, r_gla = _mix_sample(z, mp, bs, ts, cache_nsa, cache_sb, page_table, l,
                                                      cache_win[:, l], state_gla[:, l], *mix_w)
        for lst, val in zip(outs[1::2], (r_nsa, r_sb, r_win, r_gla)):
            lst.append(val)
        o_gla, o_nsa, o_sb = (
            jnp.concatenate([a.reshape(mp, a.shape[-1]), b.reshape(ms, b.shape[-1])], axis=0)
            for a, b in zip(br_p, br_s))
        y = merge_branches(o_gla, o_nsa, o_sb, z,
                           w_branch_gla[l].astype(BF16), w_branch_nsa[l].astype(BF16), w_branch_sb[l].astype(BF16))
        x = out_projection(x, y, w_out[l].astype(BF16))
        x = ffn_half_step(x, ffn2_norm[l], ffn2_w_gate[l].astype(BF16), ffn2_w_up[l].astype(BF16),
                          ffn2_w_down[l].astype(BF16))
    y = final_rmsnorm(x, final_norm)
    return (y[:mp].reshape(bp, tp, D_MODEL), y[mp:].reshape(bs, ts, D_MODEL)) + tuple(
        jnp.stack(o, axis=1) for o in outs)
```
